```python
import math
import jax, jax.numpy as jnp
from jax import lax
import numpy as np

D_MODEL = 1024
BATCH = 16
SEQ = 256
DEPTH = 2
DEC_BATCH = 4
DEC_SEQ = 2048
PAST_LEN = 512

GRID_W = 64
GROUP_W = D_MODEL // 4
HEAD_DIM = 64
MLA_HEADS = GROUP_W // HEAD_DIM
MLA_NOPE = 64
MLA_ROPE = 32
MLA_VDIM = HEAD_DIM
MLA_KV_RANK = 128
MLA_QK = MLA_NOPE + MLA_ROPE
SWA_HEADS = GROUP_W // HEAD_DIM
SWA_KV_HEADS = 2
SWA_HD = HEAD_DIM
SWA_WINDOW = 128
SWA_BLOCK = 128
GDN_HEADS = GROUP_W // HEAD_DIM
GDN_DK = HEAD_DIM
GDN_DV = HEAD_DIM
GDN_CONV = 3
GDN_CHUNK = 64
HY_WIDTH = GROUP_W
HY_ORDER = 2
HY_CONV = 3
HY_BANDS = 8
HY_EMB = 1 + 2 * HY_BANDS
HY_FF = 64
D_FF = 4 * D_MODEL
Q_BLOCK = 128
ROPE_BASE = 10000.0
EPS = 1e-6
F32 = jnp.float32
IN_SIZES = (MLA_HEADS * MLA_QK, MLA_KV_RANK, MLA_ROPE,
            SWA_HEADS * SWA_HD, SWA_KV_HEADS * SWA_HD, SWA_KV_HEADS * SWA_HD,
            GDN_HEADS * (2 * GDN_DK + GDN_DV), GDN_HEADS * GDN_DV, 2 * GDN_HEADS, 2 * GDN_HEADS,
            (HY_ORDER + 1) * HY_WIDTH)
IN_COLS = sum(IN_SIZES)
SPLIT_POINTS = tuple(int(s) for s in np.cumsum(IN_SIZES)[:-1])

kernel_name = 'hybrid_diffusion_prefix_trunk'


def rms_norm(x, g):
    xf = x.astype(F32)
    y = xf * lax.rsqrt(jnp.mean(xf * xf, axis=-1, keepdims=True) + EPS)
    return (y * g.astype(F32)).astype(x.dtype)


def l2norm(x):
    xf = x.astype(F32)
    return xf * lax.rsqrt(jnp.sum(xf * xf, axis=-1, keepdims=True) + EPS)


def axial_rope_angles(row, col, dim):
    n_freq = dim // 4
    inv = ROPE_BASE ** (-jnp.arange(n_freq, dtype=F32) / n_freq)
    ang = jnp.concatenate([row.astype(F32)[:, None] * inv, col.astype(F32)[:, None] * inv], axis=-1)
    return jnp.cos(ang), jnp.sin(ang)


def apply_rope(x, cs):
    cos, sin = cs
    xf = x.astype(F32)
    x1, x2 = xf[..., 0::2], xf[..., 1::2]
    c = cos[None, :, None, :]
    s = sin[None, :, None, :]
    return jnp.stack([x1 * c - x2 * s, x1 * s + x2 * c], axis=-1).reshape(x.shape).astype(x.dtype)


def softmax_with_sink(s, sink):
    m = jnp.maximum(jnp.max(s, axis=-1, keepdims=True), sink)
    p = jnp.exp(s - m)
    return p / (jnp.sum(p, axis=-1, keepdims=True) + jnp.exp(sink - m))


def dense_attention(q, k, v, scale, sink):
    B, Lq, H, dq = q.shape
    Hkv = k.shape[2]
    G = H // Hkv
    nb = Lq // Q_BLOCK
    qb = jnp.moveaxis(q.reshape(B, nb, Q_BLOCK, Hkv, G, dq), 1, 0)

    def one_block(qi):
        s = jnp.einsum('bqhgd,bshd->bhgqs', qi, k, preferred_element_type=F32) * scale
        if sink is None:
            p = jax.nn.softmax(s, axis=-1)
        else:
            p = softmax_with_sink(s, sink.astype(F32).reshape(1, Hkv, G, 1, 1))
        return jnp.einsum('bhgqs,bshd->bqhgd', p.astype(v.dtype), v)

    o = lax.map(one_block, qb)
    return jnp.moveaxis(o, 0, 1).reshape(B, Lq, H, v.shape[-1])


def banded_attention(q, k, v, k_ctx, v_ctx, sink):
    B, L, H, d = q.shape
    Hkv = k.shape[2]
    G = H // Hkv
    W = SWA_BLOCK
    nb = L // W
    pad = ((0, 0), (W, W), (0, 0), (0, 0))
    kp = jnp.pad(k, pad)
    vp = jnp.pad(v, pad)
    idx = jnp.arange(nb)[:, None] * W + jnp.arange(3 * W)[None, :]
    kb = kp[:, idx]
    vb = vp[:, idx]
    qb = q.reshape(B, nb, W, Hkv, G, d)
    scale = d ** -0.5
    s_loc = jnp.einsum('bnqhgd,bnshd->bnhgqs', qb, kb, preferred_element_type=F32) * scale
    qpos = jnp.arange(L).reshape(nb, W)
    kpos = idx - W
    valid = (kpos[:, None, :] >= 0) & (kpos[:, None, :] < L) & (jnp.abs(qpos[:, :, None] - kpos[:, None, :]) <= SWA_WINDOW)
    s_loc = jnp.where(valid[None, :, None, None], s_loc, -jnp.inf)
    s_ctx = jnp.einsum('bnqhgd,bshd->bnhgqs', qb, k_ctx, preferred_element_type=F32) * scale
    s = jnp.concatenate([s_loc, s_ctx], axis=-1)
    p = softmax_with_sink(s, sink.astype(F32).reshape(1, 1, Hkv, G, 1, 1)).astype(v.dtype)
    o = (jnp.einsum('bnhgqs,bnshd->bnqhgd', p[..., :3 * W], vb)
         + jnp.einsum('bnhgqs,bshd->bnqhgd', p[..., 3 * W:], v_ctx))
    return o.reshape(B, L, H, d)


def short_conv(x, w):
    K, C = w.shape
    return lax.conv_general_dilated(x, w.astype(x.dtype)[:, None, :], window_strides=(1,),
                                    padding=[(K // 2, K // 2)],
                                    dimension_numbers=('NWC', 'WIO', 'NWC'),
                                    feature_group_count=C)


def gated_delta_chunked(q, k, v, g_log, beta, s0):
    B, L, H, dk = k.shape
    dv = v.shape[-1]
    C = GDN_CHUNK
    n = L // C
    f = lambda t: t.astype(F32).reshape(B, n, C, H, -1).transpose(1, 0, 3, 2, 4)
    q_, k_, v_ = f(q), f(k), f(v)
    g_ = g_log.astype(F32).reshape(B, n, C, H).transpose(1, 0, 3, 2)
    b_ = beta.astype(F32).reshape(B, n, C, H).transpose(1, 0, 3, 2)
    decay = jnp.cumsum(g_, axis=-1)
    tri = jnp.tril(jnp.ones((C, C), dtype=bool))
    tri_strict = jnp.tril(jnp.ones((C, C), dtype=bool), -1)
    diff = decay[..., :, None] - decay[..., None, :]
    gam = jnp.where(tri, jnp.exp(jnp.where(tri, diff, 0.0)), 0.0)
    kb = k_ * b_[..., None]
    a = jnp.where(tri_strict, jnp.einsum('nbhcd,nbhsd->nbhcs', kb, k_) * gam, 0.0)
    rhs = jnp.concatenate([v_ * b_[..., None], kb * jnp.exp(decay)[..., None]], axis=-1)
    sol = lax.linalg.triangular_solve(a + jnp.eye(C, dtype=F32), rhs, left_side=True, lower=True,
                                      unit_diagonal=True)
    u_v, w = sol[..., :dv], sol[..., dv:]
    attn_qk = jnp.einsum('nbhcd,nbhsd->nbhcs', q_, k_) * gam
    q_dec = q_ * jnp.exp(decay)[..., None]
    k_tail = k_ * jnp.exp(decay[..., -1:] - decay)[..., None]
    tail = jnp.exp(decay[..., -1])

    def step(S, xs):
        uv_c, w_c, aqk_c, qd_c, kt_c, tl_c = xs
        u = uv_c - jnp.einsum('bhck,bhkv->bhcv', w_c, S)
        o = jnp.einsum('bhck,bhkv->bhcv', qd_c, S) + jnp.einsum('bhcs,bhsv->bhcv', aqk_c, u)
        S = S * tl_c[..., None, None] + jnp.einsum('bhck,bhcv->bhkv', kt_c, u)
        return S, o

    s_final, o = lax.scan(step, s0.astype(F32), (u_v, w, attn_qk, q_dec, k_tail, tail))
    return o.transpose(1, 0, 3, 2, 4).reshape(B, L, H, dv), s_final


def gdn_mixer(gqkv, gz, ga, gb, P, s0_f, s0_b):
    B, L, _ = gqkv.shape
    qkv = jax.nn.silu(short_conv(gqkv, P['gdn_conv']))
    q, k, v = jnp.split(qkv, 3, axis=-1)
    q = l2norm(q.reshape(B, L, GDN_HEADS, GDN_DK)) * (GDN_DK ** -0.5)
    k = l2norm(k.reshape(B, L, GDN_HEADS, GDN_DK))
    v = v.reshape(B, L, GDN_HEADS, GDN_DV)
    a = ga.reshape(B, L, 2, GDN_HEADS).astype(F32)
    b = gb.reshape(B, L, 2, GDN_HEADS).astype(F32)
    g_log = -jnp.exp(P['gdn_a_log'].astype(F32)) * jax.nn.softplus(a + P['gdn_dt_bias'].astype(F32))
    beta = jax.nn.sigmoid(b)
    o_f, s_f = gated_delta_chunked(q, k, v, g_log[:, :, 0], beta[:, :, 0], s0_f)
    flip = lambda t: jnp.flip(t, axis=1)
    o_b, s_b = gated_delta_chunked(flip(q), flip(k), flip(v), flip(g_log[:, :, 1]), flip(beta[:, :, 1]), s0_b)
    o = o_f + flip(o_b)
    o = rms_norm(o, P['gdn_norm']) * jax.nn.silu(gz.reshape(B, L, GDN_HEADS, GDN_DV).astype(F32))
    return o.reshape(B, L, GDN_HEADS * GDN_DV), s_f, s_b


def hyena_filters(L, P):
    t = jnp.arange(L, dtype=F32)
    t01 = t / max(L - 1, 1)
    w = 2.0 * math.pi * t / L
    bands = jnp.linspace(1e-4, HY_BANDS - 1, HY_BANDS, dtype=F32)
    feats = jnp.concatenate([t01[:, None], jnp.cos(w[:, None] * bands), -jnp.sin(w[:, None] * bands)], axis=-1)
    freq = P['hy_freq'].astype(F32)
    h = jnp.sin(freq[0] * (feats @ P['hy_w1'].astype(F32) + P['hy_b1'].astype(F32)))
    h = jnp.sin(freq[1] * (h @ P['hy_w2'].astype(F32) + P['hy_b2'].astype(F32)))
    h = h @ P['hy_w3'].astype(F32)
    dist = jnp.abs(t - (L // 2)) / (L / 2)
    h = h * jnp.exp(-dist[:, None] * jnp.abs(P['hy_decay'].astype(F32)))
    return jnp.transpose(h.reshape(L, HY_ORDER, HY_WIDTH), (1, 0, 2))


def long_conv_centred(u, h, bias):
    L = u.shape[1]
    uf = jnp.fft.rfft(u.astype(F32), n=2 * L, axis=1)
    hf = jnp.fft.rfft(h, n=2 * L, axis=0)
    y = jnp.fft.irfft(uf * hf[None], n=2 * L, axis=1)[:, L // 2: L // 2 + L]
    return y + u.astype(F32) * bias.astype(F32)


def hyena_mixer(hu, P):
    L = hu.shape[1]
    u = short_conv(hu, P['hy_conv'])
    v, x1, x2 = jnp.split(u.astype(F32), 3, axis=-1)
    filt = hyena_filters(L, P)
    z = x1 * long_conv_centred(v, filt[0], P['hy_bias'][0])
    return x2 * long_conv_centred(z, filt[1], P['hy_bias'][1])


def project(h, w_in):
    return jnp.split(jnp.einsum('bld,de->ble', h, w_in), SPLIT_POINTS, axis=-1)


def mla_expand(ckv, kpe, w_ukv):
    B, L, _ = ckv.shape
    kv = jnp.einsum('blr,re->ble', ckv, w_ukv).reshape(B, L, MLA_HEADS, MLA_NOPE + MLA_VDIM)
    k = jnp.concatenate([kv[..., :MLA_NOPE], jnp.broadcast_to(kpe[:, :, None, :], (B, L, MLA_HEADS, MLA_ROPE))], axis=-1)
    return k, kv[..., MLA_NOPE:]


def merge_heads(o_a, o_b, o_c, o_d, w_out, dtype):
    B, L = o_a.shape[:2]
    o = jnp.concatenate([t.reshape(B, L, -1).astype(dtype) for t in (o_a, o_b, o_c, o_d)], axis=-1)
    return jnp.einsum('ble,ed->bld', o, w_out)


def mixer_context(h, P):
    B, L, _ = h.shape
    mq, ckv, kpe, sq, sk, sv, gqkv, gz, ga, gb, hu = project(h, P['w_in'])
    ckv = rms_norm(ckv, P['mla_kv_norm'])
    k_a, v_a = mla_expand(ckv, kpe, P['mla_w_ukv'])
    o_a = dense_attention(mq.reshape(B, L, MLA_HEADS, MLA_QK), k_a, v_a, MLA_QK ** -0.5, None)
    k_b = sk.reshape(B, L, SWA_KV_HEADS, SWA_HD)
    v_b = sv.reshape(B, L, SWA_KV_HEADS, SWA_HD)
    o_b = dense_attention(sq.reshape(B, L, SWA_HEADS, SWA_HD), k_b, v_b, SWA_HD ** -0.5, P['swa_sink'])
    zero = jnp.zeros((B, GDN_HEADS, GDN_DK, GDN_DV), F32)
    o_c, s_f, s_b = gdn_mixer(gqkv, gz, ga, gb, P, zero, zero)
    o_d = hyena_mixer(hu, P)
    out = merge_heads(o_a, o_b, o_c, o_d, P['w_out'], h.dtype)
    return out, (ckv, kpe, k_b, v_b, jnp.stack([s_f, s_b], axis=1))


def mixer_latent(h, P, rope_a, rope_b, ctx):
    B, L, _ = h.shape
    ckv_c, kpe_c, k_c, v_c, st = ctx
    mq, ckv, kpe, sq, sk, sv, gqkv, gz, ga, gb, hu = project(h, P['w_in'])
    ckv = rms_norm(ckv, P['mla_kv_norm'])
    kpe = apply_rope(kpe[:, :, None, :], rope_a)[:, :, 0, :]
    q_a = mq.reshape(B, L, MLA_HEADS, MLA_QK)
    q_a = jnp.concatenate([q_a[..., :MLA_NOPE], apply_rope(q_a[..., MLA_NOPE:], rope_a)], axis=-1)
    k_lat, v_lat = mla_expand(ckv, kpe, P['mla_w_ukv'])
    k_ctx, v_ctx = mla_expand(ckv_c, kpe_c, P['mla_w_ukv'])
    o_a = dense_attention(q_a, jnp.concatenate([k_lat, k_ctx], axis=1),
                          jnp.concatenate([v_lat, v_ctx], axis=1), MLA_QK ** -0.5, None)
    q_b = apply_rope(sq.reshape(B, L, SWA_HEADS, SWA_HD), rope_b)
    k_b = apply_rope(sk.reshape(B, L, SWA_KV_HEADS, SWA_HD), rope_b)
    v_b = sv.reshape(B, L, SWA_KV_HEADS, SWA_HD)
    o_b = banded_attention(q_b, k_b, v_b, k_c, v_c, P['swa_sink'])
    o_c, _, _ = gdn_mixer(gqkv, gz, ga, gb, P, st[:, 0], st[:, 1])
    o_d = hyena_mixer(hu, P)
    return merge_heads(o_a, o_b, o_c, o_d, P['w_out'], h.dtype), None


def trunk_layer(x, cond, P, mixer):
    mod = jnp.einsum('bd,de->be', jax.nn.silu(cond), P['w_ada']) + P['b_ada']
    sh1, sc1, gt1, sh2, sc2, gt2 = jnp.split(mod[:, None, :], 6, axis=-1)
    h = rms_norm(x, P['g_pre_mix']) * (1.0 + sc1) + sh1
    o, extra = mixer(h)
    x = x + gt1 * rms_norm(o, P['g_post_mix'])
    h = rms_norm(x, P['g_pre_mlp']) * (1.0 + sc2) + sh2
    m = jnp.einsum('blf,fd->bld', jnp.square(jax.nn.relu(jnp.einsum('bld,df->blf', h, P['mlp_w1']))), P['mlp_w2'])
    x = x + gt2 * rms_norm(m, P['g_post_mlp'])
    return x, extra


def setup_inputs(seed: int = 0) -> dict:
    key = jax.random.key(seed)
    ks = jax.random.split(key, 36)
    D = D_MODEL

    def nrm(i, shape, scale):
        return jax.random.normal(ks[i], shape, F32) * scale

    def unif(i, shape, lo, hi):
        return jax.random.uniform(ks[i], shape, F32, lo, hi)

    dt = jnp.exp(unif(22, (DEPTH, 2, GDN_HEADS), math.log(1e-3), math.log(1e-1)))
    return {
        'x_prompt': nrm(0, (BATCH, SEQ, D), 1.0),
        'x_sample': nrm(1, (DEC_BATCH, DEC_SEQ, D), 1.0),
        'cache_mla_ckv': nrm(2, (DEC_BATCH, DEPTH, PAST_LEN, MLA_KV_RANK), 1.0),
        'cache_mla_kpe': nrm(3, (DEC_BATCH, DEPTH, PAST_LEN, MLA_ROPE), 1.0),
        'cache_swa_k': nrm(4, (DEC_BATCH, DEPTH, PAST_LEN, SWA_KV_HEADS, SWA_HD), 1.0),
        'cache_swa_v': nrm(5, (DEC_BATCH, DEPTH, PAST_LEN, SWA_KV_HEADS, SWA_HD), 1.0),
        'state_gdn': nrm(6, (DEC_BATCH, DEPTH, 2, GDN_HEADS, GDN_DK, GDN_DV), 0.1),
        'c': nrm(7, (DEC_BATCH, D), 1.0),
        'c_ctx': nrm(8, (D,), 1.0),
        'w_ada': nrm(9, (DEPTH, D, 6 * D), 0.5 * D ** -0.5),
        'b_ada': nrm(10, (DEPTH, 6 * D), 0.02),
        'g_pre_mix': 1.0 + nrm(11, (DEPTH, D), 0.05),
        'g_post_mix': 1.0 + nrm(12, (DEPTH, D), 0.05),
        'g_pre_mlp': 1.0 + nrm(13, (DEPTH, D), 0.05),
        'g_post_mlp': 1.0 + nrm(14, (DEPTH, D), 0.05),
        'w_in': nrm(15, (DEPTH, D, IN_COLS), D ** -0.5),
        'w_out': nrm(16, (DEPTH, D, D), D ** -0.5),
        'mla_kv_norm': 1.0 + nrm(17, (DEPTH, MLA_KV_RANK), 0.05),
        'mla_w_ukv': nrm(18, (DEPTH, MLA_KV_RANK, MLA_HEADS * (MLA_NOPE + MLA_VDIM)), MLA_KV_RANK ** -0.5),
        'swa_sink': nrm(19, (DEPTH, SWA_HEADS), 0.5),
        'gdn_conv': nrm(20, (DEPTH, GDN_CONV, GDN_HEADS * (2 * GDN_DK + GDN_DV)), GDN_CONV ** -0.5),
        'gdn_a_log': jnp.log(unif(21, (DEPTH, 2, GDN_HEADS), 1.0, 16.0)),
        'gdn_dt_bias': dt + jnp.log(-jnp.expm1(-dt)),
        'gdn_norm': 1.0 + nrm(23, (DEPTH, GDN_DV), 0.05),
        'hy_conv': nrm(24, (DEPTH, HY_CONV, (HY_ORDER + 1) * HY_WIDTH), HY_CONV ** -0.5),
        'hy_w1': nrm(25, (DEPTH, HY_EMB, HY_FF), HY_EMB ** -0.5),
        'hy_b1': nrm(26, (DEPTH, HY_FF), 0.1),
        'hy_w2': nrm(27, (DEPTH, HY_FF, HY_FF), HY_FF ** -0.5),
        'hy_b2': nrm(28, (DEPTH, HY_FF), 0.1),
        'hy_w3': nrm(29, (DEPTH, HY_FF, HY_ORDER * HY_WIDTH), 0.05 * HY_FF ** -0.5),
        'hy_freq': 1.0 + nrm(30, (DEPTH, 2, HY_FF), 0.1),
        'hy_decay': unif(31, (DEPTH, HY_ORDER * HY_WIDTH), 3.0, 15.0),
        'hy_bias': nrm(32, (DEPTH, HY_ORDER, HY_WIDTH), 0.1),
        'mlp_w1': nrm(33, (DEPTH, D, D_FF), D ** -0.5),
        'mlp_w2': nrm(34, (DEPTH, D_FF, D), D_FF ** -0.5),
    }


def reference(x_prompt, x_sample, cache_mla_ckv, cache_mla_kpe, cache_swa_k, cache_swa_v, state_gdn,
              c, c_ctx, w_ada, b_ada, g_pre_mix, g_post_mix, g_pre_mlp, g_post_mlp, w_in, w_out,
              mla_kv_norm, mla_w_ukv, swa_sink, gdn_conv, gdn_a_log, gdn_dt_bias, gdn_norm,
              hy_conv, hy_w1, hy_b1, hy_w2, hy_b2, hy_w3, hy_freq, hy_decay, hy_bias, mlp_w1, mlp_w2):
    cond_ctx = jnp.broadcast_to(c_ctx[None, :], (x_prompt.shape[0], c_ctx.shape[0]))
    n_lat = x_sample.shape[1]
    rows = n_lat // GRID_W
    row = jnp.repeat(jnp.arange(rows), GRID_W)
    col = jnp.tile(jnp.arange(GRID_W), rows)
    rope_a = axial_rope_angles(row, col, MLA_ROPE)
    rope_b = axial_rope_angles(row, col, SWA_HD)
    xp = x_prompt
    xs = x_sample
    ckv_l, kpe_l, k_l, v_l, st_l = [], [], [], [], []
    for l in range(DEPTH):
        P = dict(w_ada=w_ada[l], b_ada=b_ada[l], g_pre_mix=g_pre_mix[l], g_post_mix=g_post_mix[l],
                 g_pre_mlp=g_pre_mlp[l], g_post_mlp=g_post_mlp[l], w_in=w_in[l], w_out=w_out[l],
                 mla_kv_norm=mla_kv_norm[l], mla_w_ukv=mla_w_ukv[l], swa_sink=swa_sink[l],
                 gdn_conv=gdn_conv[l], gdn_a_log=gdn_a_log[l], gdn_dt_bias=gdn_dt_bias[l],
                 gdn_norm=gdn_norm[l], hy_conv=hy_conv[l], hy_w1=hy_w1[l], hy_b1=hy_b1[l],
                 hy_w2=hy_w2[l], hy_b2=hy_b2[l], hy_w3=hy_w3[l], hy_freq=hy_freq[l],
                 hy_decay=hy_decay[l], hy_bias=hy_bias[l], mlp_w1=mlp_w1[l], mlp_w2=mlp_w2[l])
        xp, new = trunk_layer(xp, cond_ctx, P, lambda h: mixer_context(h, P))
        ckv_l.append(new[0])
        kpe_l.append(new[1])
        k_l.append(new[2])
        v_l.append(new[3])
        st_l.append(new[4])
        ctx = (cache_mla_ckv[:, l], cache_mla_kpe[:, l], cache_swa_k[:, l], cache_swa_v[:, l], state_gdn[:, l])
        xs, _ = trunk_layer(xs, c, P, lambda h: mixer_latent(h, P, rope_a, rope_b, ctx))
    new_mla_ckv = jnp.stack(ckv_l, axis=1)
    new_mla_kpe = jnp.stack(kpe_l, axis=1)
    new_swa_k = jnp.stack(k_l, axis=1)
    new_swa_v = jnp.stack(v_l, axis=1)
    new_gdn_state = jnp.stack(st_l, axis=1)
    return (xp, xs, new_mla_ckv, new_mla_kpe, new_swa_k, new_swa_v, new_gdn_state)
```

```python
import functools
import math

import jax
import jax.numpy as jnp
import numpy as np
from jax import lax
from jax.experimental import pallas as pl
from jax.experimental.pallas import tpu as pltpu

F32 = jnp.float32
BF16 = jnp.bfloat16
HIGHEST = lax.Precision.HIGHEST

D_MODEL = 1024
DEPTH = 2
GRID_W = 64
HEADS = 4
HEAD_DIM = 64
GROUP_W = 256
MLA_NOPE = 64
MLA_ROPE = 32
MLA_QK = 96
MLA_KV_RANK = 128
SWA_KV_HEADS = 2
SWA_WINDOW = 128
GDN_CHUNK = 64
HY_BANDS = 8
HY_EMB = 17
HY_FF = 64
D_FF = 4096
ROPE_BASE = 10000.0
EPS = 1e-6

W_MLA = 768
W_SWA = 512
W_GDN = 1152
W_HY = 768
W_ALL = W_MLA + W_SWA + W_GDN + W_HY

TOKEN_TILE = 512
VMEM_LIMIT = 56 * 1024 * 1024

_NT = (((1,), (1,)), ((), ()))
_TN = (((0,), (0,)), ((), ()))


def _params(sem):
    return pltpu.CompilerParams(dimension_semantics=sem, vmem_limit_bytes=VMEM_LIMIT)


def _const_spec(shape):
    nd = len(shape)
    return pl.BlockSpec(shape, lambda *_: (0,) * nd, pipeline_mode=pl.Buffered(1))


def _dot(a, b, precision=None):
    if precision is None:
        a, b = a.astype(BF16), b.astype(BF16)
    return jnp.dot(a, b, preferred_element_type=F32, precision=precision)


def _dot_g(a, b, dims):
    return lax.dot_general(a.astype(BF16), b.astype(BF16), dims, preferred_element_type=F32)


def _rms(x, g):
    return x * lax.rsqrt(jnp.mean(x * x, axis=-1, keepdims=True) + EPS) * g


def _silu(x):
    return x * jax.nn.sigmoid(x)


def _swap_pairs(x):
    n = x.shape[-1]
    nxt = pltpu.roll(x, n - 1, axis=1)
    prv = pltpu.roll(x, 1, axis=1)
    lane = lax.broadcasted_iota(jnp.int32, x.shape, 1)
    return jnp.where((lane & 1) == 0, nxt, prv)


def _rope(x, cos, sin_signed):
    return x * cos + _swap_pairs(x) * sin_signed


def _mod_body(c_ref, w_ref, b_ref, o_ref):
    s = _silu(c_ref[...]).astype(BF16)
    o_ref[...] = _dot(s, w_ref[...].astype(BF16)) + b_ref[...]


def _modulation(cond8, w_ada, b_ada):
    n = 6 * D_MODEL
    tn = 1536
    return pl.pallas_call(
        _mod_body,
        out_shape=jax.ShapeDtypeStruct((DEPTH, 8, n), F32),
        grid=(DEPTH, n // tn),
        in_specs=[
            pl.BlockSpec((8, D_MODEL), lambda l, j: (0, 0)),
            pl.BlockSpec((None, D_MODEL, tn), lambda l, j: (l, 0, j)),
            pl.BlockSpec((None, 1, tn), lambda l, j: (l, 0, j)),
        ],
        out_specs=pl.BlockSpec((None, 8, tn), lambda l, j: (l, 0, j)),
        compiler_params=_params(("arbitrary", "arbitrary")),
        name="modulation",
    )(cond8, w_ada, b_ada.reshape(DEPTH, 1, n))


def _mod_spec(row0, tiles_per_row):
    if tiles_per_row is None:
        return pl.BlockSpec((None, 6, 1, D_MODEL), lambda i: (row0, 0, 0, 0))
    return pl.BlockSpec((None, 6, 1, D_MODEL), lambda i: (row0 + i // tiles_per_row, 0, 0, 0))


def _inproj_body(x_ref, mod_ref, g_ref, w_ref, o_mla, o_swa, o_gdn, o_hy):
    h = _rms(x_ref[...], g_ref[...]) * (1.0 + mod_ref[1]) + mod_ref[0]
    hb = h.astype(BF16)
    off = 0
    for o in (o_mla, o_swa, o_gdn, o_hy):
        n = o.shape[-1]
        o[...] = _dot(hb, w_ref[:, off:off + n])
        off += n


def _inproj(x, mod, mod_spec, g, w):
    t = x.shape[0]
    tm = TOKEN_TILE
    widths = (W_MLA, W_SWA, W_GDN, W_HY)
    return pl.pallas_call(
        _inproj_body,
        out_shape=[jax.ShapeDtypeStruct((t, n), F32) for n in widths],
        grid=(t // tm,),
        in_specs=[
            pl.BlockSpec((tm, D_MODEL), lambda i: (i, 0)),
            mod_spec,
            _const_spec((1, D_MODEL)),
            _const_spec((D_MODEL, W_ALL)),
        ],
        out_specs=[pl.BlockSpec((tm, n), lambda i: (i, 0)) for n in widths],
        compiler_params=_params(("arbitrary",)),
        name="inproj",
    )(x, mod, g, w)


def _outmlp_body(oa, ob, oc, od, x_ref, mod_ref, g_post_mix, g_pre_mlp, g_post_mlp,
                 wo_ref, w1_ref, w2_ref, out_ref):
    o = (_dot(oa[...], wo_ref[0]) + _dot(ob[...], wo_ref[1])
         + _dot(oc[...], wo_ref[2]) + _dot(od[...], wo_ref[3]))
    x = x_ref[...] + mod_ref[2] * _rms(o, g_post_mix[...])
    hb = (_rms(x, g_pre_mlp[...]) * (1.0 + mod_ref[4]) + mod_ref[3]).astype(BF16)
    acc = jnp.zeros(x.shape, F32)
    fc = 1024
    for c in range(D_FF // fc):
        a = _dot(hb, w1_ref[:, c * fc:(c + 1) * fc])
        a = jnp.square(jnp.maximum(a, 0.0)).astype(BF16)
        acc = acc + _dot(a, w2_ref[c * fc:(c + 1) * fc, :])
    out_ref[...] = x + mod_ref[5] * _rms(acc, g_post_mlp[...])


def _outmlp(o_parts, x, mod, mod_spec, g_post_mix, g_pre_mlp, g_post_mlp, wo, w1, w2):
    t = x.shape[0]
    tm = TOKEN_TILE
    part_spec = pl.BlockSpec((tm, GROUP_W), lambda i: (i, 0))
    return pl.pallas_call(
        _outmlp_body,
        out_shape=jax.ShapeDtypeStruct((t, D_MODEL), F32),
        grid=(t // tm,),
        in_specs=[part_spec] * 4 + [
            pl.BlockSpec((tm, D_MODEL), lambda i: (i, 0)),
            mod_spec,
            _const_spec((1, D_MODEL)), _const_spec((1, D_MODEL)), _const_spec((1, D_MODEL)),
            _const_spec((HEADS, GROUP_W, D_MODEL)),
            _const_spec((D_MODEL, D_FF)),
            _const_spec((D_FF, D_MODEL)),
        ],
        out_specs=pl.BlockSpec((tm, D_MODEL), lambda i: (i, 0)),
        compiler_params=_params(("arbitrary",)),
        name="outproj_mlp",
    )(*o_parts, x, mod, g_post_mix, g_pre_mlp, g_post_mlp, wo, w1, w2)


def _mla_body(*refs, seq, n_ctx, rope, emit_ckv, row_chunk):
    it = iter(refs)
    q_ref, kv_ref = next(it), next(it)
    ctx_ref = next(it) if n_ctx else None
    g_ref, w_ref = next(it), next(it)
    if rope:
        cq_ref, sq_ref, ck_ref, sk_ref = next(it), next(it), next(it), next(it)
    o_ref = next(it)
    ckv_out = next(it) if emit_ckv else None
    k_s, v_s = next(it), next(it)

    def expand(kin, r0, n):
        kv = _dot(kin.astype(BF16), w_ref[...])
        k_s[r0:r0 + n, :] = kv[:, :HEADS * 128].astype(BF16)
        for h in range(HEADS):
            c0 = HEADS * 128 + h * GROUP_W
            v_s[h, r0:r0 + n, :] = kv[:, c0:c0 + GROUP_W].astype(BF16)

    @pl.when(pl.program_id(1) == 0)
    def _prep():
        for r0 in range(0, seq, row_chunk):
            blk = kv_ref[r0:r0 + row_chunk, :]
            cn = _rms(blk[:, :MLA_KV_RANK], g_ref[...])
            pe = blk[:, MLA_KV_RANK:]
            if rope:
                pe = _rope(pe, ck_ref[r0:r0 + row_chunk, :], sk_ref[r0:r0 + row_chunk, :])
            if emit_ckv:
                ckv_out[r0:r0 + row_chunk, :] = cn
            expand(jnp.concatenate([cn, pe], axis=1), r0, row_chunk)
        if n_ctx:
            expand(ctx_ref[...], seq, n_ctx)

    scale = MLA_QK ** -0.5
    acc = jnp.zeros(o_ref.shape, F32)
    for h in range(HEADS):
        qh = q_ref[:, h * 128:(h + 1) * 128]
        if rope:
            qh = _rope(qh, cq_ref[...], sq_ref[...])
        s = _dot_g(qh, k_s[:, h * 128:(h + 1) * 128], _NT) * scale
        m = jnp.max(s, axis=-1, keepdims=True)
        p = jnp.exp(s - m)
        inv = 1.0 / jnp.sum(p, axis=-1, keepdims=True)
        acc = acc + _dot(p.astype(BF16), v_s[h]) * inv
    o_ref[...] = acc.astype(BF16)


def _mla(proj, n_batch, seq, g, w, ctx=None, tables=None, emit_ckv=False):
    tq = 256
    nq = seq // tq
    n_ctx = 0 if ctx is None else ctx.shape[1]
    rope = tables is not None
    lk = seq + n_ctx
    in_specs = [
        pl.BlockSpec((tq, HEADS * 128), lambda b, i: (b * nq + i, 0)),
        pl.BlockSpec((seq, 256), lambda b, i: (b, 2)),
    ]
    args = [proj, proj]
    if n_ctx:
        in_specs.append(pl.BlockSpec((None, n_ctx, 256), lambda b, i: (b, 0, 0)))
        args.append(ctx)
    in_specs += [_const_spec((1, MLA_KV_RANK)), _const_spec(w.shape)]
    args += [g, w]
    if rope:
        in_specs += [pl.BlockSpec((tq, 128), lambda b, i: (i, 0))] * 2 + [_const_spec((seq, 128))] * 2
        args += list(tables)
    out_shape = [jax.ShapeDtypeStruct((n_batch * seq, GROUP_W), BF16)]
    out_specs = [pl.BlockSpec((tq, GROUP_W), lambda b, i: (b * nq + i, 0))]
    if emit_ckv:
        out_shape.append(jax.ShapeDtypeStruct((n_batch * seq, MLA_KV_RANK), F32))
        out_specs.append(pl.BlockSpec((seq, MLA_KV_RANK), lambda b, i: (b, 0)))
    body = functools.partial(_mla_body, seq=seq, n_ctx=n_ctx, rope=rope, emit_ckv=emit_ckv,
                             row_chunk=min(seq, 512))
    return pl.pallas_call(
        body, out_shape=out_shape, grid=(n_batch, nq), in_specs=in_specs, out_specs=out_specs,
        scratch_shapes=[pltpu.VMEM((lk, HEADS * 128), BF16), pltpu.VMEM((HEADS, lk, GROUP_W), BF16)],
        compiler_params=_params(("arbitrary", "arbitrary")),
        name="mla_latent" if rope else "mla_context",
    )(*args)


def _swa_body(*refs, seq, n_ctx, local, row_chunk):
    it = iter(refs)
    q_ref, kv_ref = next(it), next(it)
    if n_ctx:
        kc_ref, vc_ref = next(it), next(it)
    sink_ref, pk_ref, pv_ref = next(it), next(it), next(it)
    if local:
        cq_ref, sq_ref, ck_ref, sk_ref = next(it), next(it), next(it), next(it)
    o_ref = next(it)
    k_s, v_s = next(it), next(it)
    tq = q_ref.shape[0]
    blk_id = pl.program_id(1)

    def expand(k, v, r0, n):
        k_s[r0:r0 + n, :] = _dot(k.astype(BF16), pk_ref[...]).astype(BF16)
        vb = v.astype(BF16)
        for h in range(HEADS):
            v_s[h, r0:r0 + n, :] = _dot(vb, pv_ref[:, h * GROUP_W:(h + 1) * GROUP_W]).astype(BF16)

    @pl.when(blk_id == 0)
    def _prep():
        for r0 in range(0, seq, row_chunk):
            k = kv_ref[r0:r0 + row_chunk, 0:128]
            v = kv_ref[r0:r0 + row_chunk, 128:256]
            if local:
                k = _rope(k, ck_ref[r0:r0 + row_chunk, :], sk_ref[r0:r0 + row_chunk, :])
            expand(k, v, r0, row_chunk)
        if n_ctx:
            expand(kc_ref[...], vc_ref[...], seq, n_ctx)

    scale = HEAD_DIM ** -0.5
    q = q_ref[...]
    if local:
        q = _rope(q, cq_ref[...], sq_ref[...])
        win = 3 * SWA_WINDOW
        start = jnp.clip((blk_id - 1) * SWA_WINDOW, 0, seq - win)
        start = pl.multiple_of(start, SWA_WINDOW)
        qpos = blk_id * tq + lax.broadcasted_iota(jnp.int32, (tq, win), 0)
        kpos = start + lax.broadcasted_iota(jnp.int32, (tq, win), 1)
        valid = jnp.abs(qpos - kpos) <= SWA_WINDOW
    lane_head = lax.broadcasted_iota(jnp.int32, q.shape, 1) // HEAD_DIM
    acc = jnp.zeros(o_ref.shape, F32)
    for h in range(HEADS):
        qm = jnp.where(lane_head == h, q, 0.0).astype(BF16)
        sink = sink_ref[:, h:h + 1]
        if local:
            s_loc = _dot_g(qm, k_s[pl.ds(start, win), :], _NT) * scale
            s_loc = jnp.where(valid, s_loc, -jnp.inf)
            s_ctx = _dot_g(qm, k_s[seq:seq + n_ctx, :], _NT) * scale
            m = jnp.maximum(jnp.maximum(jnp.max(s_loc, axis=-1, keepdims=True),
                                        jnp.max(s_ctx, axis=-1, keepdims=True)), sink)
            p_loc = jnp.exp(s_loc - m)
            p_ctx = jnp.exp(s_ctx - m)
            den = (jnp.sum(p_loc, axis=-1, keepdims=True) + jnp.sum(p_ctx, axis=-1, keepdims=True)
                   + jnp.exp(sink - m))
            o = (_dot(p_loc.astype(BF16), v_s[h, pl.ds(start, win), :])
                 + _dot(p_ctx.astype(BF16), v_s[h, seq:seq + n_ctx, :]))
        else:
            s = _dot_g(qm, k_s[...], _NT) * scale
            m = jnp.maximum(jnp.max(s, axis=-1, keepdims=True), sink)
            p = jnp.exp(s - m)
            den = jnp.sum(p, axis=-1, keepdims=True) + jnp.exp(sink - m)
            o = _dot(p.astype(BF16), v_s[h])
        acc = acc + o * (1.0 / den)
    o_ref[...] = acc.astype(BF16)


def _swa(proj, n_batch, seq, sink, pk, pv, ctx_kv=None, tables=None):
    local = tables is not None
    tq = SWA_WINDOW if local else seq
    nq = seq // tq
    n_ctx = 0 if ctx_kv is None else ctx_kv[0].shape[1]
    lk = seq + n_ctx
    in_specs = [
        pl.BlockSpec((tq, GROUP_W), lambda b, i: (b * nq + i, 0)),
        pl.BlockSpec((seq, 256), lambda b, i: (b, 1)),
    ]
    args = [proj, proj]
    if n_ctx:
        in_specs += [pl.BlockSpec((None, n_ctx, 128), lambda b, i: (b, 0, 0))] * 2
        args += list(ctx_kv)
    in_specs += [_const_spec((1, 128)), _const_spec(pk.shape), _const_spec(pv.shape)]
    args += [sink, pk, pv]
    if local:
        in_specs += [pl.BlockSpec((tq, GROUP_W), lambda b, i: (i, 0))] * 2 + [_const_spec((seq, 128))] * 2
        args += list(tables)
    body = functools.partial(_swa_body, seq=seq, n_ctx=n_ctx, local=local, row_chunk=min(seq, 512))
    return pl.pallas_call(
        body,
        out_shape=jax.ShapeDtypeStruct((n_batch * seq, GROUP_W), BF16),
        grid=(n_batch, nq), in_specs=in_specs,
        out_specs=pl.BlockSpec((tq, GROUP_W), lambda b, i: (b * nq + i, 0)),
        scratch_shapes=[pltpu.VMEM((lk, GROUP_W), BF16), pltpu.VMEM((HEADS, lk, GROUP_W), BF16)],
        compiler_params=_params(("arbitrary", "arbitrary")),
        name="swa_latent" if local else "swa_context",
    )(*args)


def _conv3(x, w_ref):
    n = x.shape[0]
    row = lax.broadcasted_iota(jnp.int32, x.shape, 0)
    prev = jnp.where(row == 0, 0.0, pltpu.roll(x, 1, axis=0))
    nxt = jnp.where(row == n - 1, 0.0, pltpu.roll(x, n - 1, axis=0))
    return prev * w_ref[0:1, :] + x * w_ref[1:2, :] + nxt * w_ref[2:3, :]


def _gdn_body(*refs, seq, has_state, emit_state):
    it = iter(refs)
    x_ref, gt_ref = next(it), next(it)
    s0_ref = next(it) if has_state else None
    cw_ref, gpar_ref, gpart_ref, gnorm_ref, ones_ref, tri_ref = (next(it) for _ in range(6))
    o_ref = next(it)
    st_ref = next(it) if emit_state else None
    q_s, k_s, v_s, of_s, ob_s, dec_s, dect_s, gate_s, s_s = (next(it) for _ in range(9))
    pair = 2 * GDN_CHUNK
    n_pair = seq // pair

    def conv_act(g):
        cols = slice(g * GROUP_W, (g + 1) * GROUP_W)
        return _silu(_conv3(x_ref[:, cols], cw_ref.at[:, cols]))

    def head_l2(x):
        return x * lax.rsqrt(_dot(x * x, ones_ref[...], HIGHEST) + EPS)

    q_s[...] = head_l2(conv_act(0)) * (HEAD_DIM ** -0.5)
    k_s[...] = head_l2(conv_act(1))
    v_s[...] = conv_act(2)
    gab = x_ref[:, 4 * GROUP_W:4 * GROUP_W + 128]
    glog = -jnp.exp(gpar_ref[0:1, :]) * jax.nn.softplus(gab + gpar_ref[1:2, :])
    gate_s[...] = jax.nn.sigmoid(gab)
    lane = lax.broadcasted_iota(jnp.int32, (pair, 128), 1)
    glogt = -jnp.exp(gpart_ref[:, 0:1]) * jax.nn.softplus(gt_ref[...] + gpart_ref[:, 1:2])
    rowi = lax.broadcasted_iota(jnp.int32, (16, pair), 0)
    for p in range(n_pair):
        r = slice(p * pair, (p + 1) * pair)
        lo = _dot(tri_ref[0], glog[r, :], HIGHEST)
        up = _dot(tri_ref[1], glog[r, :], HIGHEST)
        dec_s[r, :] = jnp.where(lane < HEADS, lo, up)
        lot = _dot(glogt[:, r], tri_ref[1], HIGHEST)
        upt = _dot(glogt[:, r], tri_ref[0], HIGHEST)
        dect_s[:, r] = jnp.where(rowi < HEADS, lot, upt)

    if has_state:
        s_s[...] = s0_ref[...]
    else:
        s_s[...] = jnp.zeros(s_s.shape, F32)

    ri = lax.broadcasted_iota(jnp.int32, (pair, pair), 0)
    ci = lax.broadcasted_iota(jnp.int32, (pair, pair), 1)
    same = (ri // GDN_CHUNK) == (ci // GDN_CHUNK)
    masks = ((same & (ri >= ci), same & (ri > ci)), (same & (ri <= ci), same & (ri < ci)))
    eye = (ri == ci).astype(F32)
    merge = [((ri // (2 * s)) == (ci // (2 * s))) & ((ri // s) != (ci // s))
             for s in (1, 2, 4, 8, 16, 32)]

    def pair_step(p, d, h):
        r = pl.ds(pl.multiple_of(p * pair, pair), pair)
        col = d * HEADS + h
        hl = slice(h * HEAD_DIM, (h + 1) * HEAD_DIM)
        q, k, v = q_s[r, hl], k_s[r, hl], v_s[r, hl]
        beta = gate_s[r, 8 + col:9 + col]
        dcol = dec_s[r, col:col + 1]
        drow = dect_s[col:col + 1, r]
        incl, strict = masks[d]
        gam = jnp.where(incl, jnp.exp(jnp.where(incl, dcol - drow, 0.0)), 0.0)
        kb = k * beta
        a = jnp.where(strict, _dot_g(kb, k, _NT) * gam, 0.0)
        attn = _dot_g(q, k, _NT) * gam
        t = eye - jnp.where(merge[0], a, 0.0)
        for m in merge[1:]:
            t = t - _dot(_dot(t, jnp.where(m, a, 0.0)), t)
        edec = jnp.exp(dcol)
        u_v = _dot(t, v * beta)
        w = _dot(t, kb * edec)
        q_dec = q * edec
        halves = (0, 1) if d == 0 else (1, 0)
        last = GDN_CHUNK - 1 if d == 0 else 0
        s = s_s[d, h]
        outs = [None, None]
        us = [None, None]
        for hf in halves:
            rows = slice(hf * GDN_CHUNK, (hf + 1) * GDN_CHUNK)
            dl = dcol[hf * GDN_CHUNK + last:hf * GDN_CHUNK + last + 1, :]
            u = u_v[rows] - _dot(w[rows], s)
            us[hf] = u
            outs[hf] = _dot(q_dec[rows], s)
            k_tail = k[rows] * jnp.exp(dl - dcol[rows])
            s = s * jnp.exp(dl) + _dot_g(k_tail, u, _TN)
        s_s[d, h] = s
        o = jnp.concatenate(outs, axis=0) + _dot(attn, jnp.concatenate(us, axis=0))
        if d == 0:
            of_s[r, hl] = o
        else:
            ob_s[r, hl] = o

    def loop_body(i, carry):
        for h in range(HEADS):
            pair_step(i, 0, h)
            pair_step(n_pair - 1 - i, 1, h)
        return carry

    lax.fori_loop(0, n_pair, loop_body, 0)

    o = of_s[...] + ob_s[...]
    ms = _dot(o * o, ones_ref[...], HIGHEST) * (1.0 / HEAD_DIM)
    o = o * lax.rsqrt(ms + EPS) * gnorm_ref[...]
    o_ref[...] = (o * _silu(x_ref[:, 3 * GROUP_W:4 * GROUP_W])).astype(BF16)
    if emit_state:
        st_ref[...] = s_s[...]


def _gdn(proj, gab_t, n_batch, seq, consts, state=None, emit_state=False):
    cw, gpar, gpart, gnorm, ones_bd, tri = consts
    has_state = state is not None
    in_specs = [pl.BlockSpec((seq, W_GDN), lambda b: (b, 0), pipeline_mode=pl.Buffered(1)),
                pl.BlockSpec((None, 16, seq), lambda b: (b, 0, 0))]
    args = [proj, gab_t]
    state_block = (None, 2, HEADS, HEAD_DIM, HEAD_DIM)
    if has_state:
        in_specs.append(pl.BlockSpec(state_block, lambda b: (b, 0, 0, 0, 0)))
        args.append(state)
    for c in (cw, gpar, gpart, gnorm, ones_bd, tri):
        in_specs.append(_const_spec(c.shape))
        args.append(c)
    out_shape = [jax.ShapeDtypeStruct((n_batch * seq, GROUP_W), BF16)]
    out_specs = [pl.BlockSpec((seq, GROUP_W), lambda b: (b, 0))]
    if emit_state:
        out_shape.append(jax.ShapeDtypeStruct((n_batch, 2, HEADS, HEAD_DIM, HEAD_DIM), F32))
        out_specs.append(pl.BlockSpec(state_block, lambda b: (b, 0, 0, 0, 0)))
    lane_dense = pltpu.VMEM((seq, GROUP_W), F32)
    scratch = [lane_dense] * 5 + [
        pltpu.VMEM((seq, 128), F32), pltpu.VMEM((16, seq), F32), pltpu.VMEM((seq, 128), F32),
        pltpu.VMEM((2, HEADS, HEAD_DIM, HEAD_DIM), F32)]
    body = functools.partial(_gdn_body, seq=seq, has_state=has_state, emit_state=emit_state)
    return pl.pallas_call(
        body, out_shape=out_shape, grid=(n_batch,), in_specs=in_specs, out_specs=out_specs,
        scratch_shapes=scratch, compiler_params=_params(("arbitrary",)),
        name="gdn_latent" if has_state else "gdn_context",
    )(*args)


def _hy_filter_body(feat_ref, dist_ref, w1_ref, b1_ref, w2_ref, b2_ref, w3_ref, freq_ref, decay_ref,
                    cf_ref, sf_ref, hr_ref, hi_ref, nyq_ref, *, seq):
    h = jnp.sin(freq_ref[0:1, :] * (_dot(feat_ref[...], w1_ref[...], HIGHEST) + b1_ref[...]))
    h = jnp.sin(freq_ref[1:2, :] * (_dot(h, w2_ref[...], HIGHEST) + b2_ref[...]))
    h = _dot(h, w3_ref[...], HIGHEST)
    filt = h * jnp.exp(-dist_ref[...] * jnp.abs(decay_ref[...]))
    n = 2 * seq
    t = lax.broadcasted_iota(jnp.int32, (seq, 1), 0)
    alt = jnp.where((t & 1) == 0, 1.0, -1.0)
    nyq_ref[...] = jnp.sum(filt * alt, axis=0, keepdims=True) * (1.0 / n)
    fb = filt.astype(BF16)
    tk = min(seq, 512)
    for k0 in range(0, seq, tk):
        hr = _dot(cf_ref[k0:k0 + tk, :], fb)
        hs = _dot(sf_ref[k0:k0 + tk, :], fb)
        k = k0 + lax.broadcasted_iota(jnp.int32, (tk, 1), 0)
        m4 = k & 3
        c4 = jnp.where(m4 == 0, 1.0, jnp.where(m4 == 2, -1.0, 0.0))
        s4 = jnp.where(m4 == 1, 1.0, jnp.where(m4 == 3, -1.0, 0.0))
        wgt = jnp.where(k == 0, 1.0 / n, 2.0 / n)
        hr_ref[k0:k0 + tk, :] = wgt * (hr * c4 + hs * s4)
        hi_ref[k0:k0 + tk, :] = wgt * (hr * s4 - hs * c4)


def _hy_filter(seq, feats, dist, w1, b1, w2, b2, w3, freq, decay, cf, sf):
    args = (feats, dist, w1, b1, w2, b2, w3, freq, decay, cf, sf)
    return pl.pallas_call(
        functools.partial(_hy_filter_body, seq=seq),
        out_shape=[jax.ShapeDtypeStruct((seq, 2 * GROUP_W), F32), jax.ShapeDtypeStruct((seq, 2 * GROUP_W), F32),
                   jax.ShapeDtypeStruct((1, 2 * GROUP_W), F32)],
        compiler_params=pltpu.CompilerParams(vmem_limit_bytes=VMEM_LIMIT),
        name="hyena_filter",
    )(*args)


def _hy_conv_body(x_ref, cw_ref, bias_ref, cf_ref, sf_ref, hr_ref, hi_ref, nyq_ref, o_ref, yr_s, yi_s, *, seq):
    t = lax.broadcasted_iota(jnp.int32, (seq, 1), 0)
    alt = jnp.where((t & 1) == 0, 1.0, -1.0)
    tk = min(seq, 512)

    def long_conv(a, o):
        cols = slice(o * GROUP_W, (o + 1) * GROUP_W)
        ab = a.astype(BF16)
        for k0 in range(0, seq, tk):
            ur = _dot(cf_ref[k0:k0 + tk, :], ab)
            us = _dot(sf_ref[k0:k0 + tk, :], ab)
            hr = hr_ref[k0:k0 + tk, cols]
            hi = hi_ref[k0:k0 + tk, cols]
            yr_s[k0:k0 + tk, :] = (ur * hr + us * hi).astype(BF16)
            yi_s[k0:k0 + tk, :] = (ur * hi - us * hr).astype(BF16)
        u_nyq = jnp.sum(a * alt, axis=0, keepdims=True)
        y = _dot(cf_ref[...], yr_s[...]) - _dot(sf_ref[...], yi_s[...])
        return y + alt * (u_nyq * nyq_ref[:, cols])

    v = _conv3(x_ref[:, 0:GROUP_W], cw_ref.at[:, 0:GROUP_W])
    x1 = _conv3(x_ref[:, GROUP_W:2 * GROUP_W], cw_ref.at[:, GROUP_W:2 * GROUP_W])
    z = x1 * (long_conv(v, 0) + v * bias_ref[0:1, :])
    x2 = _conv3(x_ref[:, 2 * GROUP_W:3 * GROUP_W], cw_ref.at[:, 2 * GROUP_W:3 * GROUP_W])
    o_ref[...] = (x2 * (long_conv(z, 1) + z * bias_ref[1:2, :])).astype(BF16)


def _hy_conv(proj, n_batch, seq, cw, bias, cf, sf, hr, hi, nyq):
    return pl.pallas_call(
        functools.partial(_hy_conv_body, seq=seq),
        out_shape=jax.ShapeDtypeStruct((n_batch * seq, GROUP_W), BF16),
        grid=(n_batch,),
        in_specs=[pl.BlockSpec((seq, W_HY), lambda b: (b, 0))]
        + [_const_spec(a.shape) for a in (cw, bias, cf, sf, hr, hi, nyq)],
        out_specs=pl.BlockSpec((seq, GROUP_W), lambda b: (b, 0)),
        scratch_shapes=[pltpu.VMEM((seq, GROUP_W), BF16)] * 2,
        compiler_params=_params(("arbitrary",)),
        name="hyena_conv",
    )(proj, cw, bias, cf, sf, hr, hi, nyq)


def _rope_tables(seq):
    rows = seq // GRID_W
    row = jnp.repeat(jnp.arange(rows), GRID_W).astype(F32)
    col = jnp.tile(jnp.arange(GRID_W), rows).astype(F32)

    def pair_tables(dim):
        n_freq = dim // 4
        inv = ROPE_BASE ** (-jnp.arange(n_freq, dtype=F32) / n_freq)
        ang = jnp.concatenate([row[:, None] * inv, col[:, None] * inv], axis=-1)
        cos = jnp.repeat(jnp.cos(ang), 2, axis=-1)
        sign = jnp.tile(jnp.array([-1.0, 1.0], F32), dim // 2)
        sin = jnp.repeat(jnp.sin(ang), 2, axis=-1) * sign
        return cos, sin

    ca, sa = pair_tables(MLA_ROPE)
    one = lambda n: jnp.ones((seq, n), F32)
    zero = lambda n: jnp.zeros((seq, n), F32)
    mla_q = (jnp.concatenate([one(MLA_NOPE), ca, one(32)], 1), jnp.concatenate([zero(MLA_NOPE), sa, zero(32)], 1))
    mla_k = (jnp.concatenate([ca, one(96)], 1), jnp.concatenate([sa, zero(96)], 1))
    cb, sb = pair_tables(HEAD_DIM)
    swa_q = (jnp.tile(cb, (1, HEADS)), jnp.tile(sb, (1, HEADS)))
    swa_k = (jnp.tile(cb, (1, SWA_KV_HEADS)), jnp.tile(sb, (1, SWA_KV_HEADS)))
    return mla_q + mla_k, swa_q + swa_k


def _dft_tables(seq):
    k = jnp.arange(seq, dtype=jnp.int32)
    m = (k[:, None] * k[None, :]) % (2 * seq)
    ang = m.astype(F32) * (math.pi / seq)
    return jnp.cos(ang).astype(BF16), jnp.sin(ang).astype(BF16)


def _hy_features(seq):
    t = jnp.arange(seq, dtype=F32)
    t01 = t / max(seq - 1, 1)
    w = 2.0 * math.pi * t / seq
    bands = jnp.linspace(1e-4, HY_BANDS - 1, HY_BANDS, dtype=F32)
    feats = jnp.concatenate([t01[:, None], jnp.cos(w[:, None] * bands), -jnp.sin(w[:, None] * bands)], axis=-1)
    feats = jnp.pad(feats, ((0, 0), (0, 128 - HY_EMB)))
    dist = (jnp.abs(t - (seq // 2)) / (seq / 2))[:, None]
    return feats, dist


def _swa_placement():
    pk = np.zeros((128, GROUP_W), np.float32)
    pv = np.zeros((128, HEADS, GROUP_W), np.float32)
    group = HEADS // SWA_KV_HEADS
    for h in range(HEADS):
        for e in range(HEAD_DIM):
            src = (h // group) * HEAD_DIM + e
            pk[src, h * HEAD_DIM + e] = 1.0
            pv[src, h, h * HEAD_DIM + e] = 1.0
    return jnp.asarray(pk, BF16), jnp.asarray(pv.reshape(128, HEADS * GROUP_W), BF16)


def _gdn_tables():
    pair = 2 * GDN_CHUNK
    i = np.arange(pair)
    same = (i[:, None] // GDN_CHUNK) == (i[None, :] // GDN_CHUNK)
    lower = (same & (i[:, None] >= i[None, :])).astype(np.float32)
    upper = (same & (i[:, None] <= i[None, :])).astype(np.float32)
    j = np.arange(GROUP_W)
    ones_bd = ((j[:, None] // HEAD_DIM) == (j[None, :] // HEAD_DIM)).astype(np.float32)
    return jnp.asarray(ones_bd), jnp.asarray(np.stack([lower, upper]))


def _layout_w_in(w):
    d = w.shape[0]
    mq = jnp.pad(w[:, :384].reshape(d, HEADS, MLA_QK), ((0, 0), (0, 0), (0, 128 - MLA_QK))).reshape(d, HEADS * 128)
    mla = jnp.concatenate([mq, w[:, 384:544], jnp.zeros((d, 96), F32)], axis=1)
    gdn = jnp.concatenate([w[:, 1056:2096], jnp.zeros((d, W_GDN - 1040), F32)], axis=1)
    return jnp.concatenate([mla, w[:, 544:1056], gdn, w[:, 2096:2864]], axis=1).astype(BF16)


def _layout_w_ukv(w):
    w = w.reshape(MLA_KV_RANK, HEADS, 128)
    wk = jnp.zeros((256, HEADS, 128), F32)
    wk = wk.at[:MLA_KV_RANK, :, :MLA_NOPE].set(w[:, :, :MLA_NOPE])
    eye = jnp.broadcast_to(jnp.eye(MLA_ROPE, dtype=F32)[:, None, :], (MLA_ROPE, HEADS, MLA_ROPE))
    wk = wk.at[MLA_KV_RANK:MLA_KV_RANK + MLA_ROPE, :, MLA_NOPE:MLA_QK].set(eye)
    wv = jnp.zeros((256, HEADS, HEADS, HEAD_DIM), F32)
    for h in range(HEADS):
        wv = wv.at[:MLA_KV_RANK, h, h, :].set(w[:, h, MLA_NOPE:])
    return jnp.concatenate([wk.reshape(256, HEADS * 128), wv.reshape(256, HEADS * GROUP_W)], axis=1).astype(BF16)


def _pad_lanes(x, n=128):
    return jnp.pad(x, [(0, 0)] * (x.ndim - 1) + [(0, n - x.shape[-1])])


def _layer_pass(x, n_batch, seq, mod, mod_spec, P, mla_ctx=None, swa_ctx=None, state=None, tables=None):
    is_ctx = tables is None
    p_mla, p_swa, p_gdn, p_hy = _inproj(x, mod, mod_spec, P["g_pre_mix"], P["w_in"])
    mla_out = _mla(p_mla, n_batch, seq, P["mla_kv_norm"], P["w_ukv"], ctx=mla_ctx,
                   tables=None if is_ctx else tables[0], emit_ckv=is_ctx)
    if is_ctx:
        o_a, ckv_n = mla_out
    else:
        (o_a,) = mla_out
    o_b = _swa(p_swa, n_batch, seq, P["swa_sink"], P["swa_pk"], P["swa_pv"], ctx_kv=swa_ctx,
               tables=None if is_ctx else tables[1])
    gab_t = jnp.swapaxes(p_gdn[:, 4 * GROUP_W:4 * GROUP_W + 16].reshape(n_batch, seq, 16), 1, 2)
    gdn_out = _gdn(p_gdn, gab_t, n_batch, seq, P["gdn_consts"], state=state, emit_state=is_ctx)
    o_c = gdn_out[0]
    hy = P["hy_ctx"] if is_ctx else P["hy_lat"]
    o_d = _hy_conv(p_hy, n_batch, seq, P["hy_conv"], P["hy_bias"], *hy)
    y = _outmlp((o_a, o_b, o_c, o_d), x, mod, mod_spec, P["g_post_mix"], P["g_pre_mlp"], P["g_post_mlp"],
                P["w_out"], P["mlp_w1"], P["mlp_w2"])
    if not is_ctx:
        return y, None
    new = (ckv_n.reshape(n_batch, seq, MLA_KV_RANK),
           p_mla[:, HEADS * 128 + MLA_KV_RANK:HEADS * 128 + MLA_KV_RANK + MLA_ROPE].reshape(n_batch, seq, MLA_ROPE),
           p_swa[:, 256:384].reshape(n_batch, seq, SWA_KV_HEADS, HEAD_DIM),
           p_swa[:, 384:512].reshape(n_batch, seq, SWA_KV_HEADS, HEAD_DIM),
           gdn_out[1])
    return y, new


def kernel(x_prompt, x_sample, cache_mla_ckv, cache_mla_kpe, cache_swa_k, cache_swa_v, state_gdn, c, c_ctx, w_ada, b_ada, g_pre_mix, g_post_mix, g_pre_mlp, g_post_mlp, w_in, w_out, mla_kv_norm, mla_w_ukv, swa_sink, gdn_conv, gdn_a_log, gdn_dt_bias, gdn_norm, hy_conv, hy_w1, hy_b1, hy_w2, hy_b2, hy_w3, hy_freq, hy_decay, hy_bias, mlp_w1, mlp_w2):
    n_ctx_b, seq_ctx, d = x_prompt.shape
    n_lat_b, seq_lat, _ = x_sample.shape
    past = cache_mla_ckv.shape[2]

    cond8 = jnp.concatenate([c_ctx[None, :], c, jnp.zeros((8 - 1 - n_lat_b, d), F32)], axis=0)
    mod = _modulation(cond8, w_ada, b_ada).reshape(DEPTH, 8, 6, 1, d)
    ctx_spec = _mod_spec(0, None)
    lat_spec = _mod_spec(1, seq_lat // TOKEN_TILE)

    tables = _rope_tables(seq_lat)
    swa_pk, swa_pv = _swa_placement()
    ones_bd, tri = _gdn_tables()
    dft = {s: _dft_tables(s) for s in (seq_ctx, seq_lat)}
    hy_feat = {s: _hy_features(s) for s in (seq_ctx, seq_lat)}

    xp = x_prompt.reshape(n_ctx_b * seq_ctx, d)
    xs = x_sample.reshape(n_lat_b * seq_lat, d)
    news = []
    for l in range(DEPTH):
        gpar = jnp.stack([_pad_lanes(gdn_a_log[l].reshape(1, 8))[0], _pad_lanes(gdn_dt_bias[l].reshape(1, 8))[0]])
        gpart = jnp.zeros((16, 128), F32).at[:8, 0].set(gdn_a_log[l].reshape(8)).at[:8, 1].set(gdn_dt_bias[l].reshape(8))
        P = dict(
            g_pre_mix=g_pre_mix[l][None], g_post_mix=g_post_mix[l][None],
            g_pre_mlp=g_pre_mlp[l][None], g_post_mlp=g_post_mlp[l][None],
            w_in=_layout_w_in(w_in[l]), w_out=w_out[l].reshape(HEADS, GROUP_W, d).astype(BF16),
            mlp_w1=mlp_w1[l].astype(BF16), mlp_w2=mlp_w2[l].astype(BF16),
            mla_kv_norm=mla_kv_norm[l][None], w_ukv=_layout_w_ukv(mla_w_ukv[l]),
            swa_sink=_pad_lanes(swa_sink[l][None]), swa_pk=swa_pk, swa_pv=swa_pv,
            gdn_consts=(gdn_conv[l], gpar, gpart, jnp.tile(gdn_norm[l], HEADS)[None], ones_bd, tri),
            hy_conv=hy_conv[l], hy_bias=hy_bias[l],
        )
        for name, s in (("hy_ctx", seq_ctx), ("hy_lat", seq_lat)):
            feats, dist = hy_feat[s]
            cf, sf = dft[s]
            hr, hi, nyq = _hy_filter(
                s, feats, dist, jnp.pad(hy_w1[l], ((0, 128 - HY_EMB), (0, 0))), hy_b1[l][None], hy_w2[l],
                hy_b2[l][None], hy_w3[l], hy_freq[l], hy_decay[l][None], cf, sf)
            P[name] = (cf, sf, hr, hi, nyq)

        xp, new = _layer_pass(xp, n_ctx_b, seq_ctx, mod[l], ctx_spec, P)
        news.append(new)
        mla_ctx = jnp.concatenate([cache_mla_ckv[:, l], _pad_lanes(cache_mla_kpe[:, l])], axis=-1)
        swa_ctx = (cache_swa_k[:, l].reshape(n_lat_b, past, 128), cache_swa_v[:, l].reshape(n_lat_b, past, 128))
        xs, _ = _layer_pass(xs, n_lat_b, seq_lat, mod[l], lat_spec, P, mla_ctx=mla_ctx, swa_ctx=swa_ctx,
                            state=state_gdn[:, l], tables=tables)

    stacked = tuple(jnp.stack([news[l][i] for l in range(DEPTH)], axis=1) for i in range(5))
    return (xp.reshape(n_ctx_b, seq_ctx, d), xs.reshape(n_lat_b, seq_lat, d)) + stacked
```

```python
import functools
import math

import jax
import jax.numpy as jnp
import numpy as np
from jax import lax
from jax.experimental import pallas as pl
from jax.experimental.pallas import tpu as pltpu

F32 = jnp.float32
BF16 = jnp.bfloat16
HIGHEST = lax.Precision.HIGHEST

D_MODEL = 1024
DEPTH = 2
GRID_W = 64
HEADS = 4
HEAD_DIM = 64
GROUP_W = 256
MLA_NOPE = 64
MLA_ROPE = 32
MLA_QK = 96
MLA_KV_RANK = 128
SWA_KV_HEADS = 2
SWA_WINDOW = 128
GDN_CHUNK = 64
HY_BANDS = 8
HY_EMB = 17
HY_FF = 64
D_FF = 4096
ROPE_BASE = 10000.0
EPS = 1e-6

W_MLA = 768
W_SWA = 512
W_GDN = 1152
W_HY = 768
W_ALL = W_MLA + W_SWA + W_GDN + W_HY

TOKEN_TILE = 512
VMEM_LIMIT = 56 * 1024 * 1024

_NT = (((1,), (1,)), ((), ()))


def _params(sem):
    return pltpu.CompilerParams(dimension_semantics=sem, vmem_limit_bytes=VMEM_LIMIT)


def _const_spec(shape):
    nd = len(shape)
    return pl.BlockSpec(shape, lambda *_: (0,) * nd, pipeline_mode=pl.Buffered(1))


def _dot(a, b, precision=None):
    if precision is None:
        a, b = a.astype(BF16), b.astype(BF16)
    return jnp.dot(a, b, preferred_element_type=F32, precision=precision)


def _dot_g(a, b, dims):
    return lax.dot_general(a.astype(BF16), b.astype(BF16), dims, preferred_element_type=F32)


_B_NN = (((2,), (1,)), ((0,), (0,)))
_B_NT = (((2,), (2,)), ((0,), (0,)))
_B_TN = (((1,), (1,)), ((0,), (0,)))


def _bmm(a, b, dims):
    return lax.dot_general(a.astype(BF16), b.astype(BF16), dims, preferred_element_type=F32)


def _split_bf16(x, parts):
    out = []
    for _ in range(parts):
        p = x.astype(BF16)
        out.append(p)
        x = x - p.astype(F32)
    return out


def _rms(x, g):
    return x * lax.rsqrt(jnp.mean(x * x, axis=-1, keepdims=True) + EPS) * g


def _silu(x):
    return x * jax.nn.sigmoid(x)


def _swap_pairs(x):
    n = x.shape[-1]
    nxt = pltpu.roll(x, n - 1, axis=1)
    prv = pltpu.roll(x, 1, axis=1)
    lane = lax.broadcasted_iota(jnp.int32, x.shape, 1)
    return jnp.where((lane & 1) == 0, nxt, prv)


def _rope(x, cos, sin_signed):
    return x * cos + _swap_pairs(x) * sin_signed


def _mod_body(c_ref, w_ref, b_ref, o_ref):
    s = _silu(c_ref[...]).astype(BF16)
    o_ref[...] = _dot(s, w_ref[...].astype(BF16)) + b_ref[...]


def _modulation(cond8, w_ada, b_ada):
    n = 6 * D_MODEL
    tn = 1536
    return pl.pallas_call(
        _mod_body,
        out_shape=jax.ShapeDtypeStruct((DEPTH, 8, n), F32),
        grid=(DEPTH, n // tn),
        in_specs=[
            pl.BlockSpec((8, D_MODEL), lambda l, j: (0, 0)),
            pl.BlockSpec((None, D_MODEL, tn), lambda l, j: (l, 0, j)),
            pl.BlockSpec((None, 1, tn), lambda l, j: (l, 0, j)),
        ],
        out_specs=pl.BlockSpec((None, 8, tn), lambda l, j: (l, 0, j)),
        compiler_params=_params(("arbitrary", "arbitrary")),
        name="modulation",
    )(cond8, w_ada, b_ada.reshape(DEPTH, 1, n))


def _mod_spec(row0, tiles_per_row):
    if tiles_per_row is None:
        return pl.BlockSpec((None, 6, 1, D_MODEL), lambda i: (row0, 0, 0, 0))
    return pl.BlockSpec((None, 6, 1, D_MODEL), lambda i: (row0 + i // tiles_per_row, 0, 0, 0))


def _inproj_body(x_ref, mod_ref, g_ref, w_ref, o_mla, o_swa, o_gdn, o_hy):
    h = _rms(x_ref[...], g_ref[...]) * (1.0 + mod_ref[1]) + mod_ref[0]
    hb = h.astype(BF16)
    off = 0
    for o in (o_mla, o_swa, o_gdn, o_hy):
        n = o.shape[-1]
        o[...] = _dot(hb, w_ref[:, off:off + n])
        off += n


def _inproj(x, mod, mod_spec, g, w):
    t = x.shape[0]
    tm = TOKEN_TILE
    widths = (W_MLA, W_SWA, W_GDN, W_HY)
    return pl.pallas_call(
        _inproj_body,
        out_shape=[jax.ShapeDtypeStruct((t, n), F32) for n in widths],
        grid=(t // tm,),
        in_specs=[
            pl.BlockSpec((tm, D_MODEL), lambda i: (i, 0)),
            mod_spec,
            _const_spec((1, D_MODEL)),
            _const_spec((D_MODEL, W_ALL)),
        ],
        out_specs=[pl.BlockSpec((tm, n), lambda i: (i, 0)) for n in widths],
        compiler_params=_params(("arbitrary",)),
        name="inproj",
    )(x, mod, g, w)


def _outmlp_body(oa, ob, oc, od, x_ref, mod_ref, g_post_mix, g_pre_mlp, g_post_mlp,
                 wo_ref, w1_ref, w2_ref, out_ref):
    o = (_dot(oa[...], wo_ref[0]) + _dot(ob[...], wo_ref[1])
         + _dot(oc[...], wo_ref[2]) + _dot(od[...], wo_ref[3]))
    x = x_ref[...] + mod_ref[2] * _rms(o, g_post_mix[...])
    hb = (_rms(x, g_pre_mlp[...]) * (1.0 + mod_ref[4]) + mod_ref[3]).astype(BF16)
    acc = jnp.zeros(x.shape, F32)
    fc = 1024
    for c in range(D_FF // fc):
        a = _dot(hb, w1_ref[:, c * fc:(c + 1) * fc])
        a = jnp.square(jnp.maximum(a, 0.0)).astype(BF16)
        acc = acc + _dot(a, w2_ref[c * fc:(c + 1) * fc, :])
    out_ref[...] = x + mod_ref[5] * _rms(acc, g_post_mlp[...])


def _outmlp(o_parts, x, mod, mod_spec, g_post_mix, g_pre_mlp, g_post_mlp, wo, w1, w2):
    t = x.shape[0]
    tm = TOKEN_TILE
    part_spec = pl.BlockSpec((tm, GROUP_W), lambda i: (i, 0))
    return pl.pallas_call(
        _outmlp_body,
        out_shape=jax.ShapeDtypeStruct((t, D_MODEL), F32),
        grid=(t // tm,),
        in_specs=[part_spec] * 4 + [
            pl.BlockSpec((tm, D_MODEL), lambda i: (i, 0)),
            mod_spec,
            _const_spec((1, D_MODEL)), _const_spec((1, D_MODEL)), _const_spec((1, D_MODEL)),
            _const_spec((HEADS, GROUP_W, D_MODEL)),
            _const_spec((D_MODEL, D_FF)),
            _const_spec((D_FF, D_MODEL)),
        ],
        out_specs=pl.BlockSpec((tm, D_MODEL), lambda i: (i, 0)),
        compiler_params=_params(("arbitrary",)),
        name="outproj_mlp",
    )(*o_parts, x, mod, g_post_mix, g_pre_mlp, g_post_mlp, wo, w1, w2)


def _mla_body(*refs, seq, n_ctx, rope, emit_ckv, row_chunk):
    it = iter(refs)
    q_ref, kv_ref = next(it), next(it)
    ctx_ref = next(it) if n_ctx else None
    g_ref, w_ref = next(it), next(it)
    if rope:
        cq_ref, sq_ref, ck_ref, sk_ref = next(it), next(it), next(it), next(it)
    o_ref = next(it)
    ckv_out = next(it) if emit_ckv else None
    k_s, v_s = next(it), next(it)

    def expand(kin, r0, n):
        kv = _dot(kin.astype(BF16), w_ref[...])
        k_s[r0:r0 + n, :] = kv[:, :HEADS * 128].astype(BF16)
        for h in range(HEADS):
            c0 = HEADS * 128 + h * GROUP_W
            v_s[h, r0:r0 + n, :] = kv[:, c0:c0 + GROUP_W].astype(BF16)

    @pl.when(pl.program_id(1) == 0)
    def _prep():
        for r0 in range(0, seq, row_chunk):
            blk = kv_ref[r0:r0 + row_chunk, :]
            cn = _rms(blk[:, :MLA_KV_RANK], g_ref[...])
            pe = blk[:, MLA_KV_RANK:]
            if rope:
                pe = _rope(pe, ck_ref[r0:r0 + row_chunk, :], sk_ref[r0:r0 + row_chunk, :])
            if emit_ckv:
                ckv_out[r0:r0 + row_chunk, :] = cn
            expand(jnp.concatenate([cn, pe], axis=1), r0, row_chunk)
        if n_ctx:
            expand(ctx_ref[...], seq, n_ctx)

    scale = MLA_QK ** -0.5
    acc = jnp.zeros(o_ref.shape, F32)
    for h in range(HEADS):
        qh = q_ref[:, h * 128:(h + 1) * 128]
        if rope:
            qh = _rope(qh, cq_ref[...], sq_ref[...])
        s = _dot_g(qh, k_s[:, h * 128:(h + 1) * 128], _NT) * scale
        m = jnp.max(s, axis=-1, keepdims=True)
        p = jnp.exp(s - m)
        inv = 1.0 / jnp.sum(p, axis=-1, keepdims=True)
        acc = acc + _dot(p.astype(BF16), v_s[h]) * inv
    o_ref[...] = acc.astype(BF16)


def _mla(proj, n_batch, seq, g, w, ctx=None, tables=None, emit_ckv=False):
    tq = 256
    nq = seq // tq
    n_ctx = 0 if ctx is None else ctx.shape[1]
    rope = tables is not None
    lk = seq + n_ctx
    in_specs = [
        pl.BlockSpec((tq, HEADS * 128), lambda b, i: (b * nq + i, 0)),
        pl.BlockSpec((seq, 256), lambda b, i: (b, 2)),
    ]
    args = [proj, proj]
    if n_ctx:
        in_specs.append(pl.BlockSpec((None, n_ctx, 256), lambda b, i: (b, 0, 0)))
        args.append(ctx)
    in_specs += [_const_spec((1, MLA_KV_RANK)), _const_spec(w.shape)]
    args += [g, w]
    if rope:
        in_specs += [pl.BlockSpec((tq, 128), lambda b, i: (i, 0))] * 2 + [_const_spec((seq, 128))] * 2
        args += list(tables)
    out_shape = [jax.ShapeDtypeStruct((n_batch * seq, GROUP_W), BF16)]
    out_specs = [pl.BlockSpec((tq, GROUP_W), lambda b, i: (b * nq + i, 0))]
    if emit_ckv:
        out_shape.append(jax.ShapeDtypeStruct((n_batch * seq, MLA_KV_RANK), F32))
        out_specs.append(pl.BlockSpec((seq, MLA_KV_RANK), lambda b, i: (b, 0)))
    body = functools.partial(_mla_body, seq=seq, n_ctx=n_ctx, rope=rope, emit_ckv=emit_ckv,
                             row_chunk=min(seq, 512))
    return pl.pallas_call(
        body, out_shape=out_shape, grid=(n_batch, nq), in_specs=in_specs, out_specs=out_specs,
        scratch_shapes=[pltpu.VMEM((lk, HEADS * 128), BF16), pltpu.VMEM((HEADS, lk, GROUP_W), BF16)],
        compiler_params=_params(("arbitrary", "arbitrary")),
        name="mla_latent" if rope else "mla_context",
    )(*args)


def _swa_body(*refs, seq, n_ctx, local, row_chunk):
    it = iter(refs)
    q_ref, kv_ref = next(it), next(it)
    if n_ctx:
        kc_ref, vc_ref = next(it), next(it)
    sink_ref, pk_ref, pv_ref = next(it), next(it), next(it)
    if local:
        cq_ref, sq_ref, ck_ref, sk_ref = next(it), next(it), next(it), next(it)
    o_ref = next(it)
    k_s, v_s = next(it), next(it)
    tq = q_ref.shape[0]
    blk_id = pl.program_id(1)

    def expand(k, v, r0, n):
        k_s[r0:r0 + n, :] = _dot(k.astype(BF16), pk_ref[...]).astype(BF16)
        vb = v.astype(BF16)
        for h in range(HEADS):
            v_s[h, r0:r0 + n, :] = _dot(vb, pv_ref[:, h * GROUP_W:(h + 1) * GROUP_W]).astype(BF16)

    @pl.when(blk_id == 0)
    def _prep():
        for r0 in range(0, seq, row_chunk):
            k = kv_ref[r0:r0 + row_chunk, 0:128]
            v = kv_ref[r0:r0 + row_chunk, 128:256]
            if local:
                k = _rope(k, ck_ref[r0:r0 + row_chunk, :], sk_ref[r0:r0 + row_chunk, :])
            expand(k, v, r0, row_chunk)
        if n_ctx:
            expand(kc_ref[...], vc_ref[...], seq, n_ctx)

    scale = HEAD_DIM ** -0.5
    q = q_ref[...]
    if local:
        q = _rope(q, cq_ref[...], sq_ref[...])
        win = tq + 2 * SWA_WINDOW
        start = jnp.clip(blk_id * tq - SWA_WINDOW, 0, seq - win)
        start = pl.multiple_of(start, SWA_WINDOW)
        qpos = blk_id * tq + lax.broadcasted_iota(jnp.int32, (tq, win), 0)
        kpos = start + lax.broadcasted_iota(jnp.int32, (tq, win), 1)
        valid = jnp.abs(qpos - kpos) <= SWA_WINDOW
    lane_head = lax.broadcasted_iota(jnp.int32, q.shape, 1) // HEAD_DIM
    acc = jnp.zeros(o_ref.shape, F32)
    for h in range(HEADS):
        qm = jnp.where(lane_head == h, q, 0.0).astype(BF16)
        sink = sink_ref[:, h:h + 1]
        if local:
            s_loc = _dot_g(qm, k_s[pl.ds(start, win), :], _NT) * scale
            s_loc = jnp.where(valid, s_loc, -jnp.inf)
            s_ctx = _dot_g(qm, k_s[seq:seq + n_ctx, :], _NT) * scale
            m = jnp.maximum(jnp.maximum(jnp.max(s_loc, axis=-1, keepdims=True),
                                        jnp.max(s_ctx, axis=-1, keepdims=True)), sink)
            p_loc = jnp.exp(s_loc - m)
            p_ctx = jnp.exp(s_ctx - m)
            den = (jnp.sum(p_loc, axis=-1, keepdims=True) + jnp.sum(p_ctx, axis=-1, keepdims=True)
                   + jnp.exp(sink - m))
            o = (_dot(p_loc.astype(BF16), v_s[h, pl.ds(start, win), :])
                 + _dot(p_ctx.astype(BF16), v_s[h, seq:seq + n_ctx, :]))
        else:
            s = _dot_g(qm, k_s[...], _NT) * scale
            m = jnp.maximum(jnp.max(s, axis=-1, keepdims=True), sink)
            p = jnp.exp(s - m)
            den = jnp.sum(p, axis=-1, keepdims=True) + jnp.exp(sink - m)
            o = _dot(p.astype(BF16), v_s[h])
        acc = acc + o * (1.0 / den)
    o_ref[...] = acc.astype(BF16)


def _swa(proj, n_batch, seq, sink, pk, pv, ctx_kv=None, tables=None):
    local = tables is not None
    tq = 2 * SWA_WINDOW if local else seq
    nq = seq // tq
    n_ctx = 0 if ctx_kv is None else ctx_kv[0].shape[1]
    lk = seq + n_ctx
    in_specs = [
        pl.BlockSpec((tq, GROUP_W), lambda b, i: (b * nq + i, 0)),
        pl.BlockSpec((seq, 256), lambda b, i: (b, 1)),
    ]
    args = [proj, proj]
    if n_ctx:
        in_specs += [pl.BlockSpec((None, n_ctx, 128), lambda b, i: (b, 0, 0))] * 2
        args += list(ctx_kv)
    in_specs += [_const_spec((1, 128)), _const_spec(pk.shape), _const_spec(pv.shape)]
    args += [sink, pk, pv]
    if local:
        in_specs += [pl.BlockSpec((tq, GROUP_W), lambda b, i: (i, 0))] * 2 + [_const_spec((seq, 128))] * 2
        args += list(tables)
    body = functools.partial(_swa_body, seq=seq, n_ctx=n_ctx, local=local, row_chunk=min(seq, 512))
    return pl.pallas_call(
        body,
        out_shape=jax.ShapeDtypeStruct((n_batch * seq, GROUP_W), BF16),
        grid=(n_batch, nq), in_specs=in_specs,
        out_specs=pl.BlockSpec((tq, GROUP_W), lambda b, i: (b * nq + i, 0)),
        scratch_shapes=[pltpu.VMEM((lk, GROUP_W), BF16), pltpu.VMEM((HEADS, lk, GROUP_W), BF16)],
        compiler_params=_params(("arbitrary", "arbitrary")),
        name="swa_latent" if local else "swa_context",
    )(*args)


def _conv3(x, w_ref):
    n = x.shape[0]
    row = lax.broadcasted_iota(jnp.int32, x.shape, 0)
    prev = jnp.where(row == 0, 0.0, pltpu.roll(x, 1, axis=0))
    nxt = jnp.where(row == n - 1, 0.0, pltpu.roll(x, n - 1, axis=0))
    return prev * w_ref[0:1, :] + x * w_ref[1:2, :] + nxt * w_ref[2:3, :]


def _gdn_body(*refs, seq, has_state, emit_state):
    it = iter(refs)
    x_ref, gt_ref = next(it), next(it)
    s0_ref = next(it) if has_state else None
    cw_ref, gpar_ref, gpart_ref, gnorm_ref, ones_ref, tri_ref, trit_ref = (next(it) for _ in range(7))
    o_ref = next(it)
    st_ref = next(it) if emit_state else None
    q_s, k_s, v_s, of_s, ob_s, dec_s, dect_s, gate_s, s_s = (next(it) for _ in range(9))
    pair = 2 * GDN_CHUNK
    n_pair = seq // pair

    def conv_act(g):
        cols = slice(g * GROUP_W, (g + 1) * GROUP_W)
        return _silu(_conv3(x_ref[:, cols], cw_ref.at[:, cols]))

    def head_sums(x):
        return _dot(jnp.concatenate(_split_bf16(x, 2), axis=1), ones_ref[...])

    def head_l2(x):
        return x * lax.rsqrt(head_sums(x * x) + EPS)

    q_s[...] = head_l2(conv_act(0)) * (HEAD_DIM ** -0.5)
    k_s[...] = head_l2(conv_act(1))
    v_s[...] = conv_act(2)
    gab = x_ref[:, 4 * GROUP_W:4 * GROUP_W + 128]
    glog = -jnp.exp(gpar_ref[0:1, :]) * jax.nn.softplus(gab + gpar_ref[1:2, :])
    gate_s[...] = jax.nn.sigmoid(gab)
    lane = lax.broadcasted_iota(jnp.int32, (pair, 128), 1)
    glogt = -jnp.exp(gpart_ref[:, 0:1]) * jax.nn.softplus(gt_ref[...] + gpart_ref[:, 1:2])
    rowi = lax.broadcasted_iota(jnp.int32, (16, pair), 0)
    g3 = _split_bf16(glog, 3)
    gt3 = _split_bf16(glogt, 3)
    for p in range(n_pair):
        r = slice(p * pair, (p + 1) * pair)
        gcol = jnp.concatenate([g[r, :] for g in g3], axis=0)
        grow = jnp.concatenate([g[:, r] for g in gt3], axis=1)
        dec_s[r, :] = jnp.where(lane < HEADS, _dot(tri_ref[0], gcol), _dot(tri_ref[1], gcol))
        dect_s[:, r] = jnp.where(rowi < HEADS, _dot(grow, trit_ref[1]), _dot(grow, trit_ref[0]))

    if has_state:
        s_s[...] = s0_ref[...]
    else:
        s_s[...] = jnp.zeros(s_s.shape, F32)

    ri = lax.broadcasted_iota(jnp.int32, (pair, pair), 0)
    ci = lax.broadcasted_iota(jnp.int32, (pair, pair), 1)
    same = (ri // GDN_CHUNK) == (ci // GDN_CHUNK)
    nb = 2 * HEADS
    both = lambda f, b: jnp.concatenate([jnp.broadcast_to(f, (HEADS, pair, pair)),
                                         jnp.broadcast_to(b, (HEADS, pair, pair))], axis=0)
    incl = both(same & (ri >= ci), same & (ri <= ci))
    strict = both(same & (ri > ci), same & (ri < ci))
    eye = (ri == ci).astype(F32)
    merge = [((ri // (2 * s)) == (ci // (2 * s))) & ((ri // s) != (ci // s))
             for s in (1, 2, 4, 8, 16, 32)]
    c = GDN_CHUNK

    def first_second(x):
        return (jnp.concatenate([x[:HEADS, :c], x[HEADS:, c:]], axis=0),
                jnp.concatenate([x[:HEADS, c:], x[HEADS:, :c]], axis=0))

    def row_order(first, second):
        return jnp.concatenate([jnp.concatenate([first[:HEADS], second[:HEADS]], axis=1),
                                jnp.concatenate([second[HEADS:], first[HEADS:]], axis=1)], axis=0)

    def loop_body(i, carry):
        rf = pl.ds(pl.multiple_of(i * pair, pair), pair)
        rb = pl.ds(pl.multiple_of((n_pair - 1 - i) * pair, pair), pair)

        def heads(ref):
            xf, xb = ref[rf, :], ref[rb, :]
            return jnp.stack([x[:, h * HEAD_DIM:(h + 1) * HEAD_DIM] for x in (xf, xb) for h in range(HEADS)])

        def cols(ref, off):
            xf, xb = ref[rf, :], ref[rb, :]
            return jnp.stack([xf[:, off + j:off + j + 1] for j in range(HEADS)]
                             + [xb[:, off + HEADS + j:off + HEADS + j + 1] for j in range(HEADS)])

        q, k, v = heads(q_s), heads(k_s), heads(v_s)
        beta = cols(gate_s, 8)
        dcol = cols(dec_s, 0)
        tf, tb = dect_s[:, rf], dect_s[:, rb]
        drow = jnp.stack([tf[j:j + 1, :] for j in range(HEADS)]
                         + [tb[HEADS + j:HEADS + j + 1, :] for j in range(HEADS)])
        gam = jnp.where(incl, jnp.exp(jnp.where(incl, dcol - drow, 0.0)), 0.0)
        kb = k * beta
        a = jnp.where(strict, _bmm(kb, k, _B_NT) * gam, 0.0)
        attn = _bmm(q, k, _B_NT) * gam
        t = eye - jnp.where(merge[0], a, 0.0)
        for m in merge[1:]:
            t = t - _bmm(_bmm(t, jnp.where(m, a, 0.0), _B_NN), t, _B_NN)
        edec = jnp.exp(dcol)
        uv1, uv2 = first_second(_bmm(t, v * beta, _B_NN))
        w1, w2 = first_second(_bmm(t, kb * edec, _B_NN))
        qd1, qd2 = first_second(q * edec)
        k1, k2 = first_second(k)
        d1, d2 = first_second(dcol)
        dl1 = jnp.concatenate([d1[:HEADS, c - 1:c], d1[HEADS:, 0:1]], axis=0)
        dl2 = jnp.concatenate([d2[:HEADS, c - 1:c], d2[HEADS:, 0:1]], axis=0)
        s = s_s[...].reshape(nb, HEAD_DIM, HEAD_DIM)
        u1 = uv1 - _bmm(w1, s, _B_NN)
        o1 = _bmm(qd1, s, _B_NN)
        s = s * jnp.exp(dl1) + _bmm(k1 * jnp.exp(dl1 - d1), u1, _B_TN)
        u2 = uv2 - _bmm(w2, s, _B_NN)
        o2 = _bmm(qd2, s, _B_NN)
        s = s * jnp.exp(dl2) + _bmm(k2 * jnp.exp(dl2 - d2), u2, _B_TN)
        s_s[...] = s.reshape(2, HEADS, HEAD_DIM, HEAD_DIM)
        o = row_order(o1, o2) + _bmm(attn, row_order(u1, u2), _B_NN)
        of_s[rf, :] = jnp.concatenate([o[h] for h in range(HEADS)], axis=1)
        ob_s[rb, :] = jnp.concatenate([o[HEADS + h] for h in range(HEADS)], axis=1)
        return carry

    lax.fori_loop(0, n_pair, loop_body, 0)

    o = of_s[...] + ob_s[...]
    ms = head_sums(o * o) * (1.0 / HEAD_DIM)
    o = o * lax.rsqrt(ms + EPS) * gnorm_ref[...]
    o_ref[...] = (o * _silu(x_ref[:, 3 * GROUP_W:4 * GROUP_W])).astype(BF16)
    if emit_state:
        st_ref[...] = s_s[...]


def _gdn(proj, gab_t, n_batch, seq, consts, state=None, emit_state=False):
    has_state = state is not None
    in_specs = [pl.BlockSpec((seq, W_GDN), lambda b: (b, 0), pipeline_mode=pl.Buffered(1)),
                pl.BlockSpec((None, 16, seq), lambda b: (b, 0, 0))]
    args = [proj, gab_t]
    state_block = (None, 2, HEADS, HEAD_DIM, HEAD_DIM)
    if has_state:
        in_specs.append(pl.BlockSpec(state_block, lambda b: (b, 0, 0, 0, 0)))
        args.append(state)
    for c in consts:
        in_specs.append(_const_spec(c.shape))
        args.append(c)
    out_shape = [jax.ShapeDtypeStruct((n_batch * seq, GROUP_W), BF16)]
    out_specs = [pl.BlockSpec((seq, GROUP_W), lambda b: (b, 0))]
    if emit_state:
        out_shape.append(jax.ShapeDtypeStruct((n_batch, 2, HEADS, HEAD_DIM, HEAD_DIM), F32))
        out_specs.append(pl.BlockSpec(state_block, lambda b: (b, 0, 0, 0, 0)))
    lane_dense = pltpu.VMEM((seq, GROUP_W), F32)
    scratch = [lane_dense] * 5 + [
        pltpu.VMEM((seq, 128), F32), pltpu.VMEM((16, seq), F32), pltpu.VMEM((seq, 128), F32),
        pltpu.VMEM((2, HEADS, HEAD_DIM, HEAD_DIM), F32)]
    body = functools.partial(_gdn_body, seq=seq, has_state=has_state, emit_state=emit_state)
    return pl.pallas_call(
        body, out_shape=out_shape, grid=(n_batch,), in_specs=in_specs, out_specs=out_specs,
        scratch_shapes=scratch, compiler_params=_params(("arbitrary",)),
        name="gdn_latent" if has_state else "gdn_context",
    )(*args)


def _hy_filter_body(feat_ref, dist_ref, w1_ref, b1_ref, w2_ref, b2_ref, w3_ref, freq_ref, decay_ref,
                    cf_ref, sf_ref, hr_ref, hi_ref, nyq_ref, *, seq):
    h = jnp.sin(freq_ref[0:1, :] * (_dot(feat_ref[...], w1_ref[...], HIGHEST) + b1_ref[...]))
    h = jnp.sin(freq_ref[1:2, :] * (_dot(h, w2_ref[...], HIGHEST) + b2_ref[...]))
    h = _dot(h, w3_ref[...], HIGHEST)
    filt = h * jnp.exp(-dist_ref[...] * jnp.abs(decay_ref[...]))
    n = 2 * seq
    t = lax.broadcasted_iota(jnp.int32, (seq, 1), 0)
    alt = jnp.where((t & 1) == 0, 1.0, -1.0)
    nyq_ref[...] = jnp.sum(filt * alt, axis=0, keepdims=True) * (1.0 / n)
    fb = filt.astype(BF16)
    tk = min(seq, 512)
    for k0 in range(0, seq, tk):
        hr = _dot(cf_ref[k0:k0 + tk, :], fb)
        hs = _dot(sf_ref[k0:k0 + tk, :], fb)
        k = k0 + lax.broadcasted_iota(jnp.int32, (tk, 1), 0)
        m4 = k & 3
        c4 = jnp.where(m4 == 0, 1.0, jnp.where(m4 == 2, -1.0, 0.0))
        s4 = jnp.where(m4 == 1, 1.0, jnp.where(m4 == 3, -1.0, 0.0))
        wgt = jnp.where(k == 0, 1.0 / n, 2.0 / n)
        hr_ref[k0:k0 + tk, :] = wgt * (hr * c4 + hs * s4)
        hi_ref[k0:k0 + tk, :] = wgt * (hr * s4 - hs * c4)


def _hy_filter(seq, feats, dist, w1, b1, w2, b2, w3, freq, decay, cf, sf):
    args = (feats, dist, w1, b1, w2, b2, w3, freq, decay, cf, sf)
    return pl.pallas_call(
        functools.partial(_hy_filter_body, seq=seq),
        out_shape=[jax.ShapeDtypeStruct((seq, 2 * GROUP_W), F32), jax.ShapeDtypeStruct((seq, 2 * GROUP_W), F32),
                   jax.ShapeDtypeStruct((1, 2 * GROUP_W), F32)],
        compiler_params=pltpu.CompilerParams(vmem_limit_bytes=VMEM_LIMIT),
        name="hyena_filter",
    )(*args)


def _hy_conv_body(x_ref, cw_ref, bias_ref, cf_ref, sf_ref, hr_ref, hi_ref, nyq_ref, o_ref, yr_s, yi_s, *, seq):
    t = lax.broadcasted_iota(jnp.int32, (seq, 1), 0)
    alt = jnp.where((t & 1) == 0, 1.0, -1.0)
    tk = min(seq, 512)

    def long_conv(a, o):
        cols = slice(o * GROUP_W, (o + 1) * GROUP_W)
        ab = a.astype(BF16)
        for k0 in range(0, seq, tk):
            ur = _dot(cf_ref[k0:k0 + tk, :], ab)
            us = _dot(sf_ref[k0:k0 + tk, :], ab)
            hr = hr_ref[k0:k0 + tk, cols]
            hi = hi_ref[k0:k0 + tk, cols]
            yr_s[k0:k0 + tk, :] = (ur * hr + us * hi).astype(BF16)
            yi_s[k0:k0 + tk, :] = (ur * hi - us * hr).astype(BF16)
        u_nyq = jnp.sum(a * alt, axis=0, keepdims=True)
        y = _dot(cf_ref[...], yr_s[...]) - _dot(sf_ref[...], yi_s[...])
        return y + alt * (u_nyq * nyq_ref[:, cols])

    v = _conv3(x_ref[:, 0:GROUP_W], cw_ref.at[:, 0:GROUP_W])
    x1 = _conv3(x_ref[:, GROUP_W:2 * GROUP_W], cw_ref.at[:, GROUP_W:2 * GROUP_W])
    z = x1 * (long_conv(v, 0) + v * bias_ref[0:1, :])
    x2 = _conv3(x_ref[:, 2 * GROUP_W:3 * GROUP_W], cw_ref.at[:, 2 * GROUP_W:3 * GROUP_W])
    o_ref[...] = (x2 * (long_conv(z, 1) + z * bias_ref[1:2, :])).astype(BF16)


def _hy_conv(proj, n_batch, seq, cw, bias, cf, sf, hr, hi, nyq):
    return pl.pallas_call(
        functools.partial(_hy_conv_body, seq=seq),
        out_shape=jax.ShapeDtypeStruct((n_batch * seq, GROUP_W), BF16),
        grid=(n_batch,),
        in_specs=[pl.BlockSpec((seq, W_HY), lambda b: (b, 0))]
        + [_const_spec(a.shape) for a in (cw, bias, cf, sf, hr, hi, nyq)],
        out_specs=pl.BlockSpec((seq, GROUP_W), lambda b: (b, 0)),
        scratch_shapes=[pltpu.VMEM((seq, GROUP_W), BF16)] * 2,
        compiler_params=_params(("arbitrary",)),
        name="hyena_conv",
    )(proj, cw, bias, cf, sf, hr, hi, nyq)


def _rope_tables(seq):
    rows = seq // GRID_W
    row = jnp.repeat(jnp.arange(rows), GRID_W).astype(F32)
    col = jnp.tile(jnp.arange(GRID_W), rows).astype(F32)

    def pair_tables(dim):
        n_freq = dim // 4
        inv = ROPE_BASE ** (-jnp.arange(n_freq, dtype=F32) / n_freq)
        ang = jnp.concatenate([row[:, None] * inv, col[:, None] * inv], axis=-1)
        cos = jnp.repeat(jnp.cos(ang), 2, axis=-1)
        sign = jnp.tile(jnp.array([-1.0, 1.0], F32), dim // 2)
        sin = jnp.repeat(jnp.sin(ang), 2, axis=-1) * sign
        return cos, sin

    ca, sa = pair_tables(MLA_ROPE)
    one = lambda n: jnp.ones((seq, n), F32)
    zero = lambda n: jnp.zeros((seq, n), F32)
    mla_q = (jnp.concatenate([one(MLA_NOPE), ca, one(32)], 1), jnp.concatenate([zero(MLA_NOPE), sa, zero(32)], 1))
    mla_k = (jnp.concatenate([ca, one(96)], 1), jnp.concatenate([sa, zero(96)], 1))
    cb, sb = pair_tables(HEAD_DIM)
    swa_q = (jnp.tile(cb, (1, HEADS)), jnp.tile(sb, (1, HEADS)))
    swa_k = (jnp.tile(cb, (1, SWA_KV_HEADS)), jnp.tile(sb, (1, SWA_KV_HEADS)))
    return mla_q + mla_k, swa_q + swa_k


def _dft_tables(seq):
    k = np.arange(seq, dtype=np.int64)
    ang = ((k[:, None] * k[None, :]) % (2 * seq)).astype(np.float64) * (math.pi / seq)
    return (jnp.asarray(np.cos(ang), F32).astype(BF16), jnp.asarray(np.sin(ang), F32).astype(BF16))


def _hy_features(seq):
    t = jnp.arange(seq, dtype=F32)
    t01 = t / max(seq - 1, 1)
    w = 2.0 * math.pi * t / seq
    bands = jnp.linspace(1e-4, HY_BANDS - 1, HY_BANDS, dtype=F32)
    feats = jnp.concatenate([t01[:, None], jnp.cos(w[:, None] * bands), -jnp.sin(w[:, None] * bands)], axis=-1)
    feats = jnp.pad(feats, ((0, 0), (0, 128 - HY_EMB)))
    dist = (jnp.abs(t - (seq // 2)) / (seq / 2))[:, None]
    return feats, dist


def _swa_placement():
    pk = np.zeros((128, GROUP_W), np.float32)
    pv = np.zeros((128, HEADS, GROUP_W), np.float32)
    group = HEADS // SWA_KV_HEADS
    for h in range(HEADS):
        for e in range(HEAD_DIM):
            src = (h // group) * HEAD_DIM + e
            pk[src, h * HEAD_DIM + e] = 1.0
            pv[src, h, h * HEAD_DIM + e] = 1.0
    return jnp.asarray(pk, BF16), jnp.asarray(pv.reshape(128, HEADS * GROUP_W), BF16)


def _gdn_tables():
    pair = 2 * GDN_CHUNK
    i = np.arange(pair)
    same = (i[:, None] // GDN_CHUNK) == (i[None, :] // GDN_CHUNK)
    lower = (same & (i[:, None] >= i[None, :])).astype(np.float32)
    upper = (same & (i[:, None] <= i[None, :])).astype(np.float32)
    j = np.arange(GROUP_W)
    ones_bd = ((j[:, None] // HEAD_DIM) == (j[None, :] // HEAD_DIM)).astype(np.float32)
    ones2 = np.concatenate([ones_bd] * 2, axis=0)
    tri3 = np.stack([np.concatenate([m] * 3, axis=1) for m in (lower, upper)])
    trit3 = np.stack([np.concatenate([m] * 3, axis=0) for m in (lower, upper)])
    return jnp.asarray(ones2, BF16), jnp.asarray(tri3, BF16), jnp.asarray(trit3, BF16)


def _layout_w_in(w):
    d = w.shape[0]
    mq = jnp.pad(w[:, :384].reshape(d, HEADS, MLA_QK), ((0, 0), (0, 0), (0, 128 - MLA_QK))).reshape(d, HEADS * 128)
    mla = jnp.concatenate([mq, w[:, 384:544], jnp.zeros((d, 96), F32)], axis=1)
    gdn = jnp.concatenate([w[:, 1056:2096], jnp.zeros((d, W_GDN - 1040), F32)], axis=1)
    return jnp.concatenate([mla, w[:, 544:1056], gdn, w[:, 2096:2864]], axis=1).astype(BF16)


def _layout_w_ukv(w):
    w = w.reshape(MLA_KV_RANK, HEADS, 128)
    wk = jnp.zeros((256, HEADS, 128), F32)
    wk = wk.at[:MLA_KV_RANK, :, :MLA_NOPE].set(w[:, :, :MLA_NOPE])
    eye = jnp.broadcast_to(jnp.eye(MLA_ROPE, dtype=F32)[:, None, :], (MLA_ROPE, HEADS, MLA_ROPE))
    wk = wk.at[MLA_KV_RANK:MLA_KV_RANK + MLA_ROPE, :, MLA_NOPE:MLA_QK].set(eye)
    wv = jnp.zeros((256, HEADS, HEADS, HEAD_DIM), F32)
    for h in range(HEADS):
        wv = wv.at[:MLA_KV_RANK, h, h, :].set(w[:, h, MLA_NOPE:])
    return jnp.concatenate([wk.reshape(256, HEADS * 128), wv.reshape(256, HEADS * GROUP_W)], axis=1).astype(BF16)


def _pad_lanes(x, n=128):
    return jnp.pad(x, [(0, 0)] * (x.ndim - 1) + [(0, n - x.shape[-1])])


def _layer_pass(x, n_batch, seq, mod, mod_spec, P, mla_ctx=None, swa_ctx=None, state=None, tables=None):
    is_ctx = tables is None
    p_mla, p_swa, p_gdn, p_hy = _inproj(x, mod, mod_spec, P["g_pre_mix"], P["w_in"])
    mla_out = _mla(p_mla, n_batch, seq, P["mla_kv_norm"], P["w_ukv"], ctx=mla_ctx,
                   tables=None if is_ctx else tables[0], emit_ckv=is_ctx)
    if is_ctx:
        o_a, ckv_n = mla_out
    else:
        (o_a,) = mla_out
    o_b = _swa(p_swa, n_batch, seq, P["swa_sink"], P["swa_pk"], P["swa_pv"], ctx_kv=swa_ctx,
               tables=None if is_ctx else tables[1])
    gab_t = jnp.swapaxes(p_gdn[:, 4 * GROUP_W:4 * GROUP_W + 16].reshape(n_batch, seq, 16), 1, 2)
    gdn_out = _gdn(p_gdn, gab_t, n_batch, seq, P["gdn_consts"], state=state, emit_state=is_ctx)
    o_c = gdn_out[0]
    hy = P["hy_ctx"] if is_ctx else P["hy_lat"]
    o_d = _hy_conv(p_hy, n_batch, seq, P["hy_conv"], P["hy_bias"], *hy)
    y = _outmlp((o_a, o_b, o_c, o_d), x, mod, mod_spec, P["g_post_mix"], P["g_pre_mlp"], P["g_post_mlp"],
                P["w_out"], P["mlp_w1"], P["mlp_w2"])
    if not is_ctx:
        return y, None
    new = (ckv_n.reshape(n_batch, seq, MLA_KV_RANK),
           p_mla[:, HEADS * 128 + MLA_KV_RANK:HEADS * 128 + MLA_KV_RANK + MLA_ROPE].reshape(n_batch, seq, MLA_ROPE),
           p_swa[:, 256:384].reshape(n_batch, seq, SWA_KV_HEADS, HEAD_DIM),
           p_swa[:, 384:512].reshape(n_batch, seq, SWA_KV_HEADS, HEAD_DIM),
           gdn_out[1])
    return y, new


def kernel(x_prompt, x_sample, cache_mla_ckv, cache_mla_kpe, cache_swa_k, cache_swa_v, state_gdn, c, c_ctx, w_ada, b_ada, g_pre_mix, g_post_mix, g_pre_mlp, g_post_mlp, w_in, w_out, mla_kv_norm, mla_w_ukv, swa_sink, gdn_conv, gdn_a_log, gdn_dt_bias, gdn_norm, hy_conv, hy_w1, hy_b1, hy_w2, hy_b2, hy_w3, hy_freq, hy_decay, hy_bias, mlp_w1, mlp_w2):
    n_ctx_b, seq_ctx, d = x_prompt.shape
    n_lat_b, seq_lat, _ = x_sample.shape
    past = cache_mla_ckv.shape[2]

    cond8 = jnp.concatenate([c_ctx[None, :], c, jnp.zeros((8 - 1 - n_lat_b, d), F32)], axis=0)
    mod = _modulation(cond8, w_ada, b_ada).reshape(DEPTH, 8, 6, 1, d)
    ctx_spec = _mod_spec(0, None)
    lat_spec = _mod_spec(1, seq_lat // TOKEN_TILE)

    tables = _rope_tables(seq_lat)
    swa_pk, swa_pv = _swa_placement()
    gdn_tables = _gdn_tables()
    dft = {s: _dft_tables(s) for s in (seq_ctx, seq_lat)}
    hy_feat = {s: _hy_features(s) for s in (seq_ctx, seq_lat)}

    xp = x_prompt.reshape(n_ctx_b * seq_ctx, d)
    xs = x_sample.reshape(n_lat_b * seq_lat, d)
    news = []
    for l in range(DEPTH):
        gpar = jnp.stack([_pad_lanes(gdn_a_log[l].reshape(1, 8))[0], _pad_lanes(gdn_dt_bias[l].reshape(1, 8))[0]])
        gpart = jnp.zeros((16, 128), F32).at[:8, 0].set(gdn_a_log[l].reshape(8)).at[:8, 1].set(gdn_dt_bias[l].reshape(8))
        P = dict(
            g_pre_mix=g_pre_mix[l][None], g_post_mix=g_post_mix[l][None],
            g_pre_mlp=g_pre_mlp[l][None], g_post_mlp=g_post_mlp[l][None],
            w_in=_layout_w_in(w_in[l]), w_out=w_out[l].reshape(HEADS, GROUP_W, d).astype(BF16),
            mlp_w1=mlp_w1[l].astype(BF16), mlp_w2=mlp_w2[l].astype(BF16),
            mla_kv_norm=mla_kv_norm[l][None], w_ukv=_layout_w_ukv(mla_w_ukv[l]),
            swa_sink=_pad_lanes(swa_sink[l][None]), swa_pk=swa_pk, swa_pv=swa_pv,
            gdn_consts=(gdn_conv[l], gpar, gpart, jnp.tile(gdn_norm[l], HEADS)[None]) + gdn_tables,
            hy_conv=hy_conv[l], hy_bias=hy_bias[l],
        )
        for name, s in (("hy_ctx", seq_ctx), ("hy_lat", seq_lat)):
            feats, dist = hy_feat[s]
            cf, sf = dft[s]
            hr, hi, nyq = _hy_filter(
                s, feats, dist, jnp.pad(hy_w1[l], ((0, 128 - HY_EMB), (0, 0))), hy_b1[l][None], hy_w2[l],
                hy_b2[l][None], hy_w3[l], hy_freq[l], hy_decay[l][None], cf, sf)
            P[name] = (cf, sf, hr, hi, nyq)

        xp, new = _layer_pass(xp, n_ctx_b, seq_ctx, mod[l], ctx_spec, P)
        news.append(new)
        mla_ctx = jnp.concatenate([cache_mla_ckv[:, l], _pad_lanes(cache_mla_kpe[:, l])], axis=-1)
        swa_ctx = (cache_swa_k[:, l].reshape(n_lat_b, past, 128), cache_swa_v[:, l].reshape(n_lat_b, past, 128))
        xs, _ = _layer_pass(xs, n_lat_b, seq_lat, mod[l], lat_spec, P, mla_ctx=mla_ctx, swa_ctx=swa_ctx,
                            state=state_gdn[:, l], tables=tables)

    stacked = tuple(jnp.stack([news[l][i] for l in range(DEPTH)], axis=1) for i in range(5))
    return (xp.reshape(n_ctx_b, seq_ctx, d), xs.reshape(n_lat_b, seq_lat, d)) + stacked
```

```python
import functools
import math

import jax
import jax.numpy as jnp
import numpy as np
from jax import lax
from jax.experimental import pallas as pl
from jax.experimental.pallas import tpu as pltpu

F32 = jnp.float32
BF16 = jnp.bfloat16
HIGHEST = lax.Precision.HIGHEST

D_MODEL = 1024
DEPTH = 2
GRID_W = 64
HEADS = 4
HEAD_DIM = 64
GROUP_W = 256
MLA_NOPE = 64
MLA_ROPE = 32
MLA_QK = 96
MLA_KV_RANK = 128
SWA_KV_HEADS = 2
SWA_WINDOW = 128
GDN_CHUNK = 64
HY_BANDS = 8
HY_EMB = 17
HY_FF = 64
D_FF = 4096
ROPE_BASE = 10000.0
EPS = 1e-6

W_MLA = 768
W_SWA = 512
W_GDN = 1152
W_HY = 768
W_ALL = W_MLA + W_SWA + W_GDN + W_HY

TOKEN_TILE = 512
VMEM_LIMIT = 56 * 1024 * 1024

_NT = (((1,), (1,)), ((), ()))


def _params(sem):
    return pltpu.CompilerParams(dimension_semantics=sem, vmem_limit_bytes=VMEM_LIMIT)


def _const_spec(shape):
    nd = len(shape)
    return pl.BlockSpec(shape, lambda *_: (0,) * nd, pipeline_mode=pl.Buffered(1))


def _layer_spec(arr, l):
    shape = arr.shape[1:]
    return pl.BlockSpec((None,) + shape, lambda *_: (l,) + (0,) * len(shape), pipeline_mode=pl.Buffered(1))


def _dot(a, b, precision=None):
    if precision is None:
        a, b = a.astype(BF16), b.astype(BF16)
    return jnp.dot(a, b, preferred_element_type=F32, precision=precision)


def _dot_g(a, b, dims):
    return lax.dot_general(a.astype(BF16), b.astype(BF16), dims, preferred_element_type=F32)


_B_NN = (((2,), (1,)), ((0,), (0,)))
_B_NT = (((2,), (2,)), ((0,), (0,)))
_B_TN = (((1,), (1,)), ((0,), (0,)))


def _bmm(a, b, dims):
    return lax.dot_general(a.astype(BF16), b.astype(BF16), dims, preferred_element_type=F32)


def _split_bf16(x, parts):
    out = []
    for _ in range(parts):
        p = x.astype(BF16)
        out.append(p)
        x = x - p.astype(F32)
    return out


def _rms(x, g):
    return x * lax.rsqrt(jnp.mean(x * x, axis=-1, keepdims=True) + EPS) * g


def _silu(x):
    return x * jax.nn.sigmoid(x)


def _swap_pairs(x):
    n = x.shape[-1]
    nxt = pltpu.roll(x, n - 1, axis=1)
    prv = pltpu.roll(x, 1, axis=1)
    lane = lax.broadcasted_iota(jnp.int32, x.shape, 1)
    return jnp.where((lane & 1) == 0, nxt, prv)


def _rope(x, cos, sin_signed):
    return x * cos + _swap_pairs(x) * sin_signed


def _mod_body(c_ref, w_ref, b_ref, o_ref):
    s = _silu(c_ref[...]).astype(BF16)
    o_ref[...] = _dot(s, w_ref[...].astype(BF16)) + b_ref[...]


def _modulation(cond8, w_ada, b_ada):
    n = 6 * D_MODEL
    tn = 1536
    return pl.pallas_call(
        _mod_body,
        out_shape=jax.ShapeDtypeStruct((DEPTH, 8, n), F32),
        grid=(DEPTH, n // tn),
        in_specs=[
            pl.BlockSpec((8, D_MODEL), lambda l, j: (0, 0)),
            pl.BlockSpec((None, D_MODEL, tn), lambda l, j: (l, 0, j)),
            pl.BlockSpec((None, 1, tn), lambda l, j: (l, 0, j)),
        ],
        out_specs=pl.BlockSpec((None, 8, tn), lambda l, j: (l, 0, j)),
        compiler_params=_params(("arbitrary", "arbitrary")),
        name="modulation",
    )(cond8, w_ada, b_ada.reshape(DEPTH, 1, n))


def _mod_spec(l, row0, tiles_per_row):
    blk = (None, None, 6, 1, D_MODEL)
    if tiles_per_row is None:
        return pl.BlockSpec(blk, lambda i: (l, row0, 0, 0, 0))
    return pl.BlockSpec(blk, lambda i: (l, row0 + i // tiles_per_row, 0, 0, 0))


def _inproj_body(x_ref, mod_ref, g_ref, w_ref, wgt_ref, o_mla, o_swa, o_gdn, o_hy, o_gate_t):
    h = _rms(x_ref[...], g_ref[...]) * (1.0 + mod_ref[1]) + mod_ref[0]
    hb = h.astype(BF16)
    off = 0
    for o in (o_mla, o_swa, o_gdn, o_hy):
        n = o.shape[-1]
        o[...] = _dot(hb, w_ref[:, off:off + n])
        off += n
    o_gate_t[...] = _dot_g(wgt_ref[...], hb, _NT)


def _inproj(x, mod, mod_spec, l, g, w, w_gate_t):
    t = x.shape[0]
    tm = TOKEN_TILE
    widths = (W_MLA, W_SWA, W_GDN, W_HY)
    return pl.pallas_call(
        _inproj_body,
        out_shape=[jax.ShapeDtypeStruct((t, n), F32) for n in widths] + [jax.ShapeDtypeStruct((16, t), F32)],
        grid=(t // tm,),
        in_specs=[
            pl.BlockSpec((tm, D_MODEL), lambda i: (i, 0)),
            mod_spec,
            _layer_spec(g, l), _layer_spec(w, l), _layer_spec(w_gate_t, l),
        ],
        out_specs=[pl.BlockSpec((tm, n), lambda i: (i, 0)) for n in widths]
        + [pl.BlockSpec((16, tm), lambda i: (0, i))],
        compiler_params=_params(("arbitrary",)),
        name="inproj",
    )(x, mod, g, w, w_gate_t)


def _outmlp_body(oa, ob, oc, od, x_ref, mod_ref, g_post_mix, g_pre_mlp, g_post_mlp,
                 wo_ref, w1_ref, w2_ref, out_ref):
    o = (_dot(oa[...], wo_ref[0]) + _dot(ob[...], wo_ref[1])
         + _dot(oc[...], wo_ref[2]) + _dot(od[...], wo_ref[3]))
    x = x_ref[...] + mod_ref[2] * _rms(o, g_post_mix[...])
    hb = (_rms(x, g_pre_mlp[...]) * (1.0 + mod_ref[4]) + mod_ref[3]).astype(BF16)
    acc = jnp.zeros(x.shape, F32)
    fc = 1024
    for c in range(D_FF // fc):
        a = _dot(hb, w1_ref[:, c * fc:(c + 1) * fc])
        a = jnp.square(jnp.maximum(a, 0.0)).astype(BF16)
        acc = acc + _dot(a, w2_ref[c * fc:(c + 1) * fc, :])
    out_ref[...] = x + mod_ref[5] * _rms(acc, g_post_mlp[...])


def _outmlp(o_parts, x, mod, mod_spec, l, g_post_mix, g_pre_mlp, g_post_mlp, wo, w1, w2):
    t = x.shape[0]
    tm = TOKEN_TILE
    part_spec = pl.BlockSpec((tm, GROUP_W), lambda i: (i, 0))
    return pl.pallas_call(
        _outmlp_body,
        out_shape=jax.ShapeDtypeStruct((t, D_MODEL), F32),
        grid=(t // tm,),
        in_specs=[part_spec] * 4 + [
            pl.BlockSpec((tm, D_MODEL), lambda i: (i, 0)),
            mod_spec,
        ] + [_layer_spec(a, l) for a in (g_post_mix, g_pre_mlp, g_post_mlp, wo, w1, w2)],
        out_specs=pl.BlockSpec((tm, D_MODEL), lambda i: (i, 0)),
        compiler_params=_params(("arbitrary",)),
        name="outproj_mlp",
    )(*o_parts, x, mod, g_post_mix, g_pre_mlp, g_post_mlp, wo, w1, w2)


def _mla_body(*refs, seq, n_ctx, rope, emit_ckv, row_chunk):
    it = iter(refs)
    q_ref, kv_ref = next(it), next(it)
    ctx_ref = next(it) if n_ctx else None
    g_ref, w_ref = next(it), next(it)
    if rope:
        cq_ref, sq_ref, ck_ref, sk_ref = next(it), next(it), next(it), next(it)
    o_ref = next(it)
    ckv_out = next(it) if emit_ckv else None
    k_s, v_s = next(it), next(it)

    def expand(kin, r0, n):
        kv = _dot(kin.astype(BF16), w_ref[...])
        k_s[r0:r0 + n, :] = kv[:, :HEADS * 128].astype(BF16)
        for h in range(HEADS):
            c0 = HEADS * 128 + h * GROUP_W
            v_s[h, r0:r0 + n, :] = kv[:, c0:c0 + GROUP_W].astype(BF16)

    @pl.when(pl.program_id(1) == 0)
    def _prep():
        for r0 in range(0, seq, row_chunk):
            blk = kv_ref[r0:r0 + row_chunk, :]
            cn = _rms(blk[:, :MLA_KV_RANK], g_ref[...])
            pe = blk[:, MLA_KV_RANK:]
            if rope:
                pe = _rope(pe, ck_ref[r0:r0 + row_chunk, :], sk_ref[r0:r0 + row_chunk, :])
            if emit_ckv:
                ckv_out[r0:r0 + row_chunk, :] = cn
            expand(jnp.concatenate([cn, pe], axis=1), r0, row_chunk)
        if n_ctx:
            expand(ctx_ref[...], seq, n_ctx)

    scale = MLA_QK ** -0.5
    acc = jnp.zeros(o_ref.shape, F32)
    for h in range(HEADS):
        qh = q_ref[:, h * 128:(h + 1) * 128]
        if rope:
            qh = _rope(qh, cq_ref[...], sq_ref[...])
        s = _dot_g(qh, k_s[:, h * 128:(h + 1) * 128], _NT) * scale
        m = jnp.max(s, axis=-1, keepdims=True)
        p = jnp.exp(s - m)
        inv = 1.0 / jnp.sum(p, axis=-1, keepdims=True)
        acc = acc + _dot(p.astype(BF16), v_s[h]) * inv
    o_ref[...] = acc.astype(BF16)


def _mla(proj, n_batch, seq, l, g, w, ctx=None, tables=None, emit_ckv=False):
    tq = 256
    nq = seq // tq
    n_ctx = 0 if ctx is None else ctx.shape[2]
    rope = tables is not None
    lk = seq + n_ctx
    in_specs = [
        pl.BlockSpec((tq, HEADS * 128), lambda b, i: (b * nq + i, 0)),
        pl.BlockSpec((seq, 256), lambda b, i: (b, 2)),
    ]
    args = [proj, proj]
    if n_ctx:
        in_specs.append(pl.BlockSpec((None, None, n_ctx, 256), lambda b, i: (b, l, 0, 0)))
        args.append(ctx)
    in_specs += [_layer_spec(g, l), _layer_spec(w, l)]
    args += [g, w]
    if rope:
        in_specs += [pl.BlockSpec((tq, 128), lambda b, i: (i, 0))] * 2 + [_const_spec((seq, 128))] * 2
        args += list(tables)
    out_shape = [jax.ShapeDtypeStruct((n_batch * seq, GROUP_W), BF16)]
    out_specs = [pl.BlockSpec((tq, GROUP_W), lambda b, i: (b * nq + i, 0))]
    if emit_ckv:
        out_shape.append(jax.ShapeDtypeStruct((n_batch * seq, MLA_KV_RANK), F32))
        out_specs.append(pl.BlockSpec((seq, MLA_KV_RANK), lambda b, i: (b, 0)))
    body = functools.partial(_mla_body, seq=seq, n_ctx=n_ctx, rope=rope, emit_ckv=emit_ckv,
                             row_chunk=min(seq, 512))
    return pl.pallas_call(
        body, out_shape=out_shape, grid=(n_batch, nq), in_specs=in_specs, out_specs=out_specs,
        scratch_shapes=[pltpu.VMEM((lk, HEADS * 128), BF16), pltpu.VMEM((HEADS, lk, GROUP_W), BF16)],
        compiler_params=_params(("arbitrary", "arbitrary")),
        name="mla_latent" if rope else "mla_context",
    )(*args)


def _swa_body(*refs, seq, n_ctx, local, row_chunk):
    it = iter(refs)
    q_ref, kv_ref = next(it), next(it)
    if n_ctx:
        kc_ref, vc_ref = next(it), next(it)
    sink_ref, pk_ref, pv_ref = next(it), next(it), next(it)
    if local:
        cq_ref, sq_ref, ck_ref, sk_ref = next(it), next(it), next(it), next(it)
    o_ref = next(it)
    k_s, v_s = next(it), next(it)
    tq = q_ref.shape[0]
    blk_id = pl.program_id(1)

    def expand(k, v, r0, n):
        k_s[r0:r0 + n, :] = _dot(k.astype(BF16), pk_ref[...]).astype(BF16)
        vb = v.astype(BF16)
        for h in range(HEADS):
            v_s[h, r0:r0 + n, :] = _dot(vb, pv_ref[:, h * GROUP_W:(h + 1) * GROUP_W]).astype(BF16)

    @pl.when(blk_id == 0)
    def _prep():
        for r0 in range(0, seq, row_chunk):
            k = kv_ref[r0:r0 + row_chunk, 0:128]
            v = kv_ref[r0:r0 + row_chunk, 128:256]
            if local:
                k = _rope(k, ck_ref[r0:r0 + row_chunk, :], sk_ref[r0:r0 + row_chunk, :])
            expand(k, v, r0, row_chunk)
        if n_ctx:
            expand(kc_ref[...], vc_ref[...], seq, n_ctx)

    scale = HEAD_DIM ** -0.5
    q = q_ref[...]
    if local:
        q = _rope(q, cq_ref[...], sq_ref[...])
        win = tq + 2 * SWA_WINDOW
        start = jnp.clip(blk_id * tq - SWA_WINDOW, 0, seq - win)
        start = pl.multiple_of(start, SWA_WINDOW)
        qpos = blk_id * tq + lax.broadcasted_iota(jnp.int32, (tq, win), 0)
        kpos = start + lax.broadcasted_iota(jnp.int32, (tq, win), 1)
        valid = jnp.abs(qpos - kpos) <= SWA_WINDOW
    lane_head = lax.broadcasted_iota(jnp.int32, q.shape, 1) // HEAD_DIM
    acc = jnp.zeros(o_ref.shape, F32)
    for h in range(HEADS):
        qm = jnp.where(lane_head == h, q, 0.0).astype(BF16)
        sink = sink_ref[:, h:h + 1]
        if local:
            s_loc = _dot_g(qm, k_s[pl.ds(start, win), :], _NT) * scale
            s_loc = jnp.where(valid, s_loc, -jnp.inf)
            s_ctx = _dot_g(qm, k_s[seq:seq + n_ctx, :], _NT) * scale
            m = jnp.maximum(jnp.maximum(jnp.max(s_loc, axis=-1, keepdims=True),
                                        jnp.max(s_ctx, axis=-1, keepdims=True)), sink)
            p_loc = jnp.exp(s_loc - m)
            p_ctx = jnp.exp(s_ctx - m)
            den = (jnp.sum(p_loc, axis=-1, keepdims=True) + jnp.sum(p_ctx, axis=-1, keepdims=True)
                   + jnp.exp(sink - m))
            o = (_dot(p_loc.astype(BF16), v_s[h, pl.ds(start, win), :])
                 + _dot(p_ctx.astype(BF16), v_s[h, seq:seq + n_ctx, :]))
        else:
            s = _dot_g(qm, k_s[...], _NT) * scale
            m = jnp.maximum(jnp.max(s, axis=-1, keepdims=True), sink)
            p = jnp.exp(s - m)
            den = jnp.sum(p, axis=-1, keepdims=True) + jnp.exp(sink - m)
            o = _dot(p.astype(BF16), v_s[h])
        acc = acc + o * (1.0 / den)
    o_ref[...] = acc.astype(BF16)


def _swa(proj, n_batch, seq, l, sink, pk, pv, ctx_kv=None, tables=None):
    local = tables is not None
    tq = 2 * SWA_WINDOW if local else seq
    nq = seq // tq
    n_ctx = 0 if ctx_kv is None else ctx_kv[0].shape[2]
    lk = seq + n_ctx
    in_specs = [
        pl.BlockSpec((tq, GROUP_W), lambda b, i: (b * nq + i, 0)),
        pl.BlockSpec((seq, 256), lambda b, i: (b, 1)),
    ]
    args = [proj, proj]
    if n_ctx:
        in_specs += [pl.BlockSpec((None, None, n_ctx, 128), lambda b, i: (b, l, 0, 0))] * 2
        args += list(ctx_kv)
    in_specs += [_layer_spec(sink, l), _const_spec(pk.shape), _const_spec(pv.shape)]
    args += [sink, pk, pv]
    if local:
        in_specs += [pl.BlockSpec((tq, GROUP_W), lambda b, i: (i, 0))] * 2 + [_const_spec((seq, 128))] * 2
        args += list(tables)
    body = functools.partial(_swa_body, seq=seq, n_ctx=n_ctx, local=local, row_chunk=min(seq, 512))
    return pl.pallas_call(
        body,
        out_shape=jax.ShapeDtypeStruct((n_batch * seq, GROUP_W), BF16),
        grid=(n_batch, nq), in_specs=in_specs,
        out_specs=pl.BlockSpec((tq, GROUP_W), lambda b, i: (b * nq + i, 0)),
        scratch_shapes=[pltpu.VMEM((lk, GROUP_W), BF16), pltpu.VMEM((HEADS, lk, GROUP_W), BF16)],
        compiler_params=_params(("arbitrary", "arbitrary")),
        name="swa_latent" if local else "swa_context",
    )(*args)


def _conv3(x, w_ref):
    n = x.shape[0]
    row = lax.broadcasted_iota(jnp.int32, x.shape, 0)
    prev = jnp.where(row == 0, 0.0, pltpu.roll(x, 1, axis=0))
    nxt = jnp.where(row == n - 1, 0.0, pltpu.roll(x, n - 1, axis=0))
    return prev * w_ref[0:1, :] + x * w_ref[1:2, :] + nxt * w_ref[2:3, :]


def _gdn_body(*refs, seq, has_state, emit_state):
    it = iter(refs)
    x_ref, gt_ref = next(it), next(it)
    s0_ref = next(it) if has_state else None
    cw_ref, gpar_ref, gpart_ref, gnorm_ref, ones_ref, tri_ref, trit_ref = (next(it) for _ in range(7))
    o_ref = next(it)
    st_ref = next(it) if emit_state else None
    q_s, k_s, v_s, of_s, ob_s, dec_s, dect_s, gate_s, s_s = (next(it) for _ in range(9))
    st_uv, st_k, st_d, st_w, st_qd, st_attn = (next(it) for _ in range(6))
    pair = 2 * GDN_CHUNK
    n_pair = seq // pair

    def conv_act(g):
        cols = slice(g * GROUP_W, (g + 1) * GROUP_W)
        return _silu(_conv3(x_ref[:, cols], cw_ref.at[:, cols]))

    def head_sums(x):
        return _dot(jnp.concatenate(_split_bf16(x, 2), axis=1), ones_ref[...])

    def head_l2(x):
        return x * lax.rsqrt(head_sums(x * x) + EPS)

    q_s[...] = head_l2(conv_act(0)) * (HEAD_DIM ** -0.5)
    k_s[...] = head_l2(conv_act(1))
    v_s[...] = conv_act(2)
    gab = x_ref[:, 4 * GROUP_W:4 * GROUP_W + 128]
    glog = -jnp.exp(gpar_ref[0:1, :]) * jax.nn.softplus(gab + gpar_ref[1:2, :])
    gate_s[...] = jax.nn.sigmoid(gab)
    lane = lax.broadcasted_iota(jnp.int32, (pair, 128), 1)
    glogt = -jnp.exp(gpart_ref[:, 0:1]) * jax.nn.softplus(gt_ref[...] + gpart_ref[:, 1:2])
    rowi = lax.broadcasted_iota(jnp.int32, (16, pair), 0)
    g3 = _split_bf16(glog, 3)
    gt3 = _split_bf16(glogt, 3)
    for p in range(n_pair):
        r = slice(p * pair, (p + 1) * pair)
        gcol = jnp.concatenate([g[r, :] for g in g3], axis=0)
        grow = jnp.concatenate([g[:, r] for g in gt3], axis=1)
        dec_s[r, :] = jnp.where(lane < HEADS, _dot(tri_ref[0], gcol), _dot(tri_ref[1], gcol))
        dect_s[:, r] = jnp.where(rowi < HEADS, _dot(grow, trit_ref[1]), _dot(grow, trit_ref[0]))

    if has_state:
        s_s[...] = s0_ref[...]
    else:
        s_s[...] = jnp.zeros(s_s.shape, F32)

    ri = lax.broadcasted_iota(jnp.int32, (pair, pair), 0)
    ci = lax.broadcasted_iota(jnp.int32, (pair, pair), 1)
    same = (ri // GDN_CHUNK) == (ci // GDN_CHUNK)
    nb = 2 * HEADS
    both = lambda f, b: jnp.concatenate([jnp.broadcast_to(f, (HEADS, pair, pair)),
                                         jnp.broadcast_to(b, (HEADS, pair, pair))], axis=0)
    incl = both(same & (ri >= ci), same & (ri <= ci))
    strict = both(same & (ri > ci), same & (ri < ci))
    eye = (ri == ci).astype(F32)
    merge = [((ri // (2 * s)) == (ci // (2 * s))) & ((ri // s) != (ci // s))
             for s in (1, 2, 4, 8, 16, 32)]
    c = GDN_CHUNK

    def first_second(x):
        return (jnp.concatenate([x[:HEADS, :c], x[HEADS:, c:]], axis=0),
                jnp.concatenate([x[:HEADS, c:], x[HEADS:, :c]], axis=0))

    def row_order(first, second):
        return jnp.concatenate([jnp.concatenate([first[:HEADS], second[:HEADS]], axis=1),
                                jnp.concatenate([second[HEADS:], first[HEADS:]], axis=1)], axis=0)

    def rows_of(i):
        fwd, bwd = i * pair, (n_pair - 1 - i) * pair
        if not isinstance(i, int):
            fwd, bwd = pl.multiple_of(fwd, pair), pl.multiple_of(bwd, pair)
        return pl.ds(fwd, pair), pl.ds(bwd, pair)

    def solve_stage(i):
        rf, rb = rows_of(i)
        slot = i % 2

        def heads(ref):
            xf, xb = ref[rf, :], ref[rb, :]
            return jnp.stack([x[:, h * HEAD_DIM:(h + 1) * HEAD_DIM] for x in (xf, xb) for h in range(HEADS)])

        def cols(ref, off):
            xf, xb = ref[rf, :], ref[rb, :]
            return jnp.stack([xf[:, off + j:off + j + 1] for j in range(HEADS)]
                             + [xb[:, off + HEADS + j:off + HEADS + j + 1] for j in range(HEADS)])

        q, k, v = heads(q_s), heads(k_s), heads(v_s)
        beta = cols(gate_s, 8)
        dcol = cols(dec_s, 0)
        tf, tb = dect_s[:, rf], dect_s[:, rb]
        drow = jnp.stack([tf[j:j + 1, :] for j in range(HEADS)]
                         + [tb[HEADS + j:HEADS + j + 1, :] for j in range(HEADS)])
        gam = jnp.where(incl, jnp.exp(jnp.where(incl, dcol - drow, 0.0)), 0.0)
        kb = k * beta
        a = jnp.where(strict, _bmm(kb, k, _B_NT) * gam, 0.0)
        attn = _bmm(q, k, _B_NT) * gam
        t = eye - jnp.where(merge[0], a, 0.0)
        for m in merge[1:]:
            t = t - _bmm(_bmm(t, jnp.where(m, a, 0.0), _B_NN), t, _B_NN)
        edec = jnp.exp(dcol)
        st_uv[slot] = _bmm(t, v * beta, _B_NN)
        st_w[slot] = _bmm(t, kb * edec, _B_NN).astype(BF16)
        st_qd[slot] = (q * edec).astype(BF16)
        st_attn[slot] = attn.astype(BF16)
        st_k[slot] = k
        st_d[slot] = dcol

    def state_stage(i):
        rf, rb = rows_of(i)
        slot = i % 2
        uv1, uv2 = first_second(st_uv[slot])
        w1, w2 = first_second(st_w[slot])
        qd1, qd2 = first_second(st_qd[slot])
        k1, k2 = first_second(st_k[slot])
        d1, d2 = first_second(st_d[slot])
        dl1 = jnp.concatenate([d1[:HEADS, c - 1:c], d1[HEADS:, 0:1]], axis=0)
        dl2 = jnp.concatenate([d2[:HEADS, c - 1:c], d2[HEADS:, 0:1]], axis=0)
        s = s_s[...].reshape(nb, HEAD_DIM, HEAD_DIM)
        u1 = uv1 - _bmm(w1, s, _B_NN)
        o1 = _bmm(qd1, s, _B_NN)
        s = s * jnp.exp(dl1) + _bmm(k1 * jnp.exp(dl1 - d1), u1, _B_TN)
        u2 = uv2 - _bmm(w2, s, _B_NN)
        o2 = _bmm(qd2, s, _B_NN)
        s = s * jnp.exp(dl2) + _bmm(k2 * jnp.exp(dl2 - d2), u2, _B_TN)
        s_s[...] = s.reshape(2, HEADS, HEAD_DIM, HEAD_DIM)
        o = row_order(o1, o2) + _bmm(st_attn[slot], row_order(u1, u2), _B_NN)
        of_s[rf, :] = jnp.concatenate([o[h] for h in range(HEADS)], axis=1)
        ob_s[rb, :] = jnp.concatenate([o[HEADS + h] for h in range(HEADS)], axis=1)

    def loop_body(i, carry):
        state_stage(i)
        solve_stage(i + 1)
        return carry

    solve_stage(0)
    lax.fori_loop(0, n_pair - 1, loop_body, 0)
    state_stage(n_pair - 1)

    o = of_s[...] + ob_s[...]
    ms = head_sums(o * o) * (1.0 / HEAD_DIM)
    o = o * lax.rsqrt(ms + EPS) * gnorm_ref[...]
    o_ref[...] = (o * _silu(x_ref[:, 3 * GROUP_W:4 * GROUP_W])).astype(BF16)
    if emit_state:
        st_ref[...] = s_s[...]


def _gdn(proj, gate_t, n_batch, seq, l, layer_params, tables, state=None, emit_state=False):
    has_state = state is not None
    in_specs = [pl.BlockSpec((seq, W_GDN), lambda b: (b, 0), pipeline_mode=pl.Buffered(1)),
                pl.BlockSpec((16, seq), lambda b: (0, b))]
    args = [proj, gate_t]
    state_block = (None, 2, HEADS, HEAD_DIM, HEAD_DIM)
    if has_state:
        in_specs.append(pl.BlockSpec((None,) + state_block, lambda b: (b, l, 0, 0, 0, 0)))
        args.append(state)
    for p in layer_params:
        in_specs.append(_layer_spec(p, l))
        args.append(p)
    for c in tables:
        in_specs.append(_const_spec(c.shape))
        args.append(c)
    out_shape = [jax.ShapeDtypeStruct((n_batch * seq, GROUP_W), BF16)]
    out_specs = [pl.BlockSpec((seq, GROUP_W), lambda b: (b, 0))]
    if emit_state:
        out_shape.append(jax.ShapeDtypeStruct((n_batch, 2, HEADS, HEAD_DIM, HEAD_DIM), F32))
        out_specs.append(pl.BlockSpec(state_block, lambda b: (b, 0, 0, 0, 0)))
    lane_dense = pltpu.VMEM((seq, GROUP_W), F32)
    scratch = [lane_dense] * 5 + [
        pltpu.VMEM((seq, 128), F32), pltpu.VMEM((16, seq), F32), pltpu.VMEM((seq, 128), F32),
        pltpu.VMEM((2, HEADS, HEAD_DIM, HEAD_DIM), F32)]
    nb, pair = 2 * HEADS, 2 * GDN_CHUNK
    scratch += [pltpu.VMEM((2, nb, pair, HEAD_DIM), F32), pltpu.VMEM((2, nb, pair, HEAD_DIM), F32),
                pltpu.VMEM((2, nb, pair, 1), F32), pltpu.VMEM((2, nb, pair, HEAD_DIM), BF16),
                pltpu.VMEM((2, nb, pair, HEAD_DIM), BF16), pltpu.VMEM((2, nb, pair, pair), BF16)]
    body = functools.partial(_gdn_body, seq=seq, has_state=has_state, emit_state=emit_state)
    return pl.pallas_call(
        body, out_shape=out_shape, grid=(n_batch,), in_specs=in_specs, out_specs=out_specs,
        scratch_shapes=scratch, compiler_params=_params(("arbitrary",)),
        name="gdn_latent" if has_state else "gdn_context",
    )(*args)


def _hy_filter_body(feat_ref, dist_ref, w1_ref, b1_ref, w2_ref, b2_ref, w3_ref, freq_ref, decay_ref,
                    cf_ref, sf_ref, hr_ref, hi_ref, nyq_ref, *, seq):
    h = jnp.sin(freq_ref[0:1, :] * (_dot(feat_ref[...], w1_ref[...], HIGHEST) + b1_ref[...]))
    h = jnp.sin(freq_ref[1:2, :] * (_dot(h, w2_ref[...], HIGHEST) + b2_ref[...]))
    h = _dot(h, w3_ref[...], HIGHEST)
    filt = h * jnp.exp(-dist_ref[...] * jnp.abs(decay_ref[...]))
    n = 2 * seq
    t = lax.broadcasted_iota(jnp.int32, (seq, 1), 0)
    alt = jnp.where((t & 1) == 0, 1.0, -1.0)
    nyq_ref[...] = jnp.sum(filt * alt, axis=0, keepdims=True) * (1.0 / n)
    fb = filt.astype(BF16)
    tk = min(seq, 512)
    for k0 in range(0, seq, tk):
        hr = _dot(cf_ref[k0:k0 + tk, :], fb)
        hs = _dot(sf_ref[k0:k0 + tk, :], fb)
        k = k0 + lax.broadcasted_iota(jnp.int32, (tk, 1), 0)
        m4 = k & 3
        c4 = jnp.where(m4 == 0, 1.0, jnp.where(m4 == 2, -1.0, 0.0))
        s4 = jnp.where(m4 == 1, 1.0, jnp.where(m4 == 3, -1.0, 0.0))
        wgt = jnp.where(k == 0, 1.0 / n, 2.0 / n)
        hr_ref[k0:k0 + tk, :] = wgt * (hr * c4 + hs * s4)
        hi_ref[k0:k0 + tk, :] = wgt * (hr * s4 - hs * c4)


def _hy_filter(seq, feats, dist, layer_params, cf, sf):
    per_layer = lambda a: pl.BlockSpec((None,) + a.shape[1:], lambda l: (l,) + (0,) * (a.ndim - 1))
    out_block = lambda rows: pl.BlockSpec((None, rows, 2 * GROUP_W), lambda l: (l, 0, 0))
    return pl.pallas_call(
        functools.partial(_hy_filter_body, seq=seq),
        out_shape=[jax.ShapeDtypeStruct((DEPTH, seq, 2 * GROUP_W), F32),
                   jax.ShapeDtypeStruct((DEPTH, seq, 2 * GROUP_W), F32),
                   jax.ShapeDtypeStruct((DEPTH, 1, 2 * GROUP_W), F32)],
        grid=(DEPTH,),
        in_specs=[_const_spec(feats.shape), _const_spec(dist.shape)] + [per_layer(a) for a in layer_params]
        + [_const_spec(cf.shape), _const_spec(sf.shape)],
        out_specs=[out_block(seq), out_block(seq), out_block(1)],
        compiler_params=_params(("arbitrary",)),
        name="hyena_filter",
    )(feats, dist, *layer_params, cf, sf)


def _hy_conv_body(x_ref, cw_ref, bias_ref, cf_ref, sf_ref, hr_ref, hi_ref, nyq_ref, o_ref, yr_s, yi_s, *, seq):
    t = lax.broadcasted_iota(jnp.int32, (seq, 1), 0)
    alt = jnp.where((t & 1) == 0, 1.0, -1.0)
    tk = min(seq, 512)

    def long_conv(a, o):
        cols = slice(o * GROUP_W, (o + 1) * GROUP_W)
        ab = a.astype(BF16)
        for k0 in range(0, seq, tk):
            ur = _dot(cf_ref[k0:k0 + tk, :], ab)
            us = _dot(sf_ref[k0:k0 + tk, :], ab)
            hr = hr_ref[k0:k0 + tk, cols]
            hi = hi_ref[k0:k0 + tk, cols]
            yr_s[k0:k0 + tk, :] = (ur * hr + us * hi).astype(BF16)
            yi_s[k0:k0 + tk, :] = (ur * hi - us * hr).astype(BF16)
        u_nyq = jnp.sum(a * alt, axis=0, keepdims=True)
        y = _dot(cf_ref[...], yr_s[...]) - _dot(sf_ref[...], yi_s[...])
        return y + alt * (u_nyq * nyq_ref[:, cols])

    v = _conv3(x_ref[:, 0:GROUP_W], cw_ref.at[:, 0:GROUP_W])
    x1 = _conv3(x_ref[:, GROUP_W:2 * GROUP_W], cw_ref.at[:, GROUP_W:2 * GROUP_W])
    z = x1 * (long_conv(v, 0) + v * bias_ref[0:1, :])
    x2 = _conv3(x_ref[:, 2 * GROUP_W:3 * GROUP_W], cw_ref.at[:, 2 * GROUP_W:3 * GROUP_W])
    o_ref[...] = (x2 * (long_conv(z, 1) + z * bias_ref[1:2, :])).astype(BF16)


def _hy_conv(proj, n_batch, seq, l, cw, bias, cf, sf, hr, hi, nyq):
    return pl.pallas_call(
        functools.partial(_hy_conv_body, seq=seq),
        out_shape=jax.ShapeDtypeStruct((n_batch * seq, GROUP_W), BF16),
        grid=(n_batch,),
        in_specs=[pl.BlockSpec((seq, W_HY), lambda b: (b, 0)),
                  _layer_spec(cw, l), _layer_spec(bias, l), _const_spec(cf.shape), _const_spec(sf.shape),
                  _layer_spec(hr, l), _layer_spec(hi, l), _layer_spec(nyq, l)],
        out_specs=pl.BlockSpec((seq, GROUP_W), lambda b: (b, 0)),
        scratch_shapes=[pltpu.VMEM((seq, GROUP_W), BF16)] * 2,
        compiler_params=_params(("arbitrary",)),
        name="hyena_conv",
    )(proj, cw, bias, cf, sf, hr, hi, nyq)


def _rope_tables(seq):
    rows = seq // GRID_W
    row = np.repeat(np.arange(rows), GRID_W).astype(np.float64)
    col = np.tile(np.arange(GRID_W), rows).astype(np.float64)

    def pair_tables(dim):
        n_freq = dim // 4
        inv = (ROPE_BASE ** (-np.arange(n_freq, dtype=np.float32) / n_freq)).astype(np.float64)
        ang = np.concatenate([row[:, None] * inv, col[:, None] * inv], axis=-1).astype(np.float32)
        cos = np.repeat(np.cos(ang), 2, axis=-1)
        sin = np.repeat(np.sin(ang), 2, axis=-1) * np.tile(np.array([-1.0, 1.0], np.float32), dim // 2)
        return cos, sin

    ca, sa = pair_tables(MLA_ROPE)
    one = lambda n: np.ones((seq, n), np.float32)
    zero = lambda n: np.zeros((seq, n), np.float32)
    mla = (np.concatenate([one(MLA_NOPE), ca, one(32)], 1), np.concatenate([zero(MLA_NOPE), sa, zero(32)], 1),
           np.concatenate([ca, one(96)], 1), np.concatenate([sa, zero(96)], 1))
    cb, sb = pair_tables(HEAD_DIM)
    swa = (np.tile(cb, (1, HEADS)), np.tile(sb, (1, HEADS)),
           np.tile(cb, (1, SWA_KV_HEADS)), np.tile(sb, (1, SWA_KV_HEADS)))
    as_f32 = lambda ts: tuple(jnp.asarray(t, F32) for t in ts)
    return as_f32(mla), as_f32(swa)


def _dft_tables(seq):
    k = np.arange(seq, dtype=np.int64)
    ang = ((k[:, None] * k[None, :]) % (2 * seq)).astype(np.float64) * (math.pi / seq)
    return (jnp.asarray(np.cos(ang), F32).astype(BF16), jnp.asarray(np.sin(ang), F32).astype(BF16))


def _hy_features(seq):
    t = np.arange(seq, dtype=np.float32)
    t01 = t / np.float32(max(seq - 1, 1))
    w = (np.float32(2.0 * math.pi) * t / np.float32(seq)).astype(np.float64)
    bands = np.linspace(1e-4, HY_BANDS - 1, HY_BANDS, dtype=np.float32).astype(np.float64)
    feats = np.concatenate([t01[:, None].astype(np.float64), np.cos(w[:, None] * bands), -np.sin(w[:, None] * bands)],
                           axis=-1)
    feats = np.pad(feats, ((0, 0), (0, 128 - HY_EMB)))
    dist = (np.abs(t - (seq // 2)) / np.float32(seq / 2))[:, None]
    return jnp.asarray(feats, F32), jnp.asarray(dist, F32)


def _swa_placement():
    pk = np.zeros((128, GROUP_W), np.float32)
    pv = np.zeros((128, HEADS, GROUP_W), np.float32)
    group = HEADS // SWA_KV_HEADS
    for h in range(HEADS):
        for e in range(HEAD_DIM):
            src = (h // group) * HEAD_DIM + e
            pk[src, h * HEAD_DIM + e] = 1.0
            pv[src, h, h * HEAD_DIM + e] = 1.0
    return jnp.asarray(pk, BF16), jnp.asarray(pv.reshape(128, HEADS * GROUP_W), BF16)


def _gdn_tables():
    pair = 2 * GDN_CHUNK
    i = np.arange(pair)
    same = (i[:, None] // GDN_CHUNK) == (i[None, :] // GDN_CHUNK)
    lower = (same & (i[:, None] >= i[None, :])).astype(np.float32)
    upper = (same & (i[:, None] <= i[None, :])).astype(np.float32)
    j = np.arange(GROUP_W)
    ones_bd = ((j[:, None] // HEAD_DIM) == (j[None, :] // HEAD_DIM)).astype(np.float32)
    ones2 = np.concatenate([ones_bd] * 2, axis=0)
    tri3 = np.stack([np.concatenate([m] * 3, axis=1) for m in (lower, upper)])
    trit3 = np.stack([np.concatenate([m] * 3, axis=0) for m in (lower, upper)])
    return jnp.asarray(ones2, BF16), jnp.asarray(tri3, BF16), jnp.asarray(trit3, BF16)


def _pad_last(x, n):
    return jnp.pad(x, [(0, 0)] * (x.ndim - 1) + [(0, n - x.shape[-1])])


def _layout_w_in(w):
    dep, d, _ = w.shape
    mq = _pad_last(w[..., :384].reshape(dep, d, HEADS, MLA_QK), 128).reshape(dep, d, HEADS * 128)
    mla = _pad_last(jnp.concatenate([mq, w[..., 384:544]], axis=-1), W_MLA)
    gdn = _pad_last(w[..., 1056:2096], W_GDN)
    w_all = jnp.concatenate([mla, w[..., 544:1056], gdn, w[..., 2096:2864]], axis=-1).astype(BF16)
    w_gate_t = jnp.swapaxes(w[..., 2080:2096], 1, 2).astype(BF16)
    return w_all, w_gate_t


def _layout_w_ukv(w):
    dep = w.shape[0]
    w = w.reshape(dep, MLA_KV_RANK, HEADS, 128)
    wk = _pad_last(w[..., :MLA_NOPE], 128).reshape(dep, MLA_KV_RANK, HEADS * 128)
    place = np.zeros((128, HEADS, 128), np.float32)
    for j in range(MLA_ROPE):
        place[j, :, MLA_NOPE + j] = 1.0
    place = jnp.broadcast_to(jnp.asarray(place.reshape(128, HEADS * 128)), (dep, 128, HEADS * 128))
    wk = jnp.concatenate([wk, place], axis=1)
    eye = jnp.asarray(np.eye(HEADS, dtype=np.float32))
    wv = (w[..., MLA_NOPE:][:, :, :, None, :] * eye[None, None, :, :, None]).reshape(dep, MLA_KV_RANK, HEADS * GROUP_W)
    wv = jnp.pad(wv, ((0, 0), (0, 128), (0, 0)))
    return jnp.concatenate([wk, wv], axis=-1).astype(BF16)


def _layer_pass(x, n_batch, seq, l, mod, mod_spec, P, mla_ctx=None, swa_ctx=None, state=None, tables=None):
    is_ctx = tables is None
    p_mla, p_swa, p_gdn, p_hy, gate_t = _inproj(x, mod, mod_spec, l, P["g_pre_mix"], P["w_in"], P["w_gate_t"])
    mla_out = _mla(p_mla, n_batch, seq, l, P["mla_kv_norm"], P["w_ukv"], ctx=mla_ctx,
                   tables=None if is_ctx else tables[0], emit_ckv=is_ctx)
    o_a = mla_out[0]
    o_b = _swa(p_swa, n_batch, seq, l, P["swa_sink"], P["swa_pk"], P["swa_pv"], ctx_kv=swa_ctx,
               tables=None if is_ctx else tables[1])
    gdn_out = _gdn(p_gdn, gate_t, n_batch, seq, l, P["gdn_params"], P["gdn_tables"], state=state, emit_state=is_ctx)
    o_c = gdn_out[0]
    o_d = _hy_conv(p_hy, n_batch, seq, l, P["hy_conv"], P["hy_bias"], *(P["hy_ctx"] if is_ctx else P["hy_lat"]))
    y = _outmlp((o_a, o_b, o_c, o_d), x, mod, mod_spec, l, P["g_post_mix"], P["g_pre_mlp"], P["g_post_mlp"],
                P["w_out"], P["mlp_w1"], P["mlp_w2"])
    if not is_ctx:
        return y, None
    kpe0 = HEADS * 128 + MLA_KV_RANK
    new = (mla_out[1].reshape(n_batch, seq, MLA_KV_RANK),
           p_mla[:, kpe0:kpe0 + MLA_ROPE].reshape(n_batch, seq, MLA_ROPE),
           p_swa[:, 256:384].reshape(n_batch, seq, SWA_KV_HEADS, HEAD_DIM),
           p_swa[:, 384:512].reshape(n_batch, seq, SWA_KV_HEADS, HEAD_DIM),
           gdn_out[1])
    return y, new


def kernel(x_prompt, x_sample, cache_mla_ckv, cache_mla_kpe, cache_swa_k, cache_swa_v, state_gdn, c, c_ctx, w_ada, b_ada, g_pre_mix, g_post_mix, g_pre_mlp, g_post_mlp, w_in, w_out, mla_kv_norm, mla_w_ukv, swa_sink, gdn_conv, gdn_a_log, gdn_dt_bias, gdn_norm, hy_conv, hy_w1, hy_b1, hy_w2, hy_b2, hy_w3, hy_freq, hy_decay, hy_bias, mlp_w1, mlp_w2):
    n_ctx_b, seq_ctx, d = x_prompt.shape
    n_lat_b, seq_lat, _ = x_sample.shape
    past = cache_mla_ckv.shape[2]

    cond8 = jnp.concatenate([c_ctx[None, :], c, jnp.zeros((8 - 1 - n_lat_b, d), F32)], axis=0)
    mod = _modulation(cond8, w_ada, b_ada).reshape(DEPTH, 8, 6, 1, d)

    w_in_all, w_gate_t = _layout_w_in(w_in)
    gates = jnp.stack([gdn_a_log.reshape(DEPTH, 8), gdn_dt_bias.reshape(DEPTH, 8)], axis=1)
    gates_t = jnp.pad(jnp.swapaxes(gates, 1, 2), ((0, 0), (0, 8), (0, 0)))
    swa_pk, swa_pv = _swa_placement()
    P = dict(
        g_pre_mix=g_pre_mix[:, None], g_post_mix=g_post_mix[:, None],
        g_pre_mlp=g_pre_mlp[:, None], g_post_mlp=g_post_mlp[:, None],
        w_in=w_in_all, w_gate_t=w_gate_t,
        w_out=w_out.reshape(DEPTH, HEADS, GROUP_W, d).astype(BF16),
        mlp_w1=mlp_w1.astype(BF16), mlp_w2=mlp_w2.astype(BF16),
        mla_kv_norm=mla_kv_norm[:, None], w_ukv=_layout_w_ukv(mla_w_ukv),
        swa_sink=_pad_last(swa_sink, 128)[:, None], swa_pk=swa_pk, swa_pv=swa_pv,
        gdn_params=(gdn_conv, _pad_last(gates, 128), _pad_last(gates_t, 128), jnp.tile(gdn_norm, (1, HEADS))[:, None]),
        gdn_tables=_gdn_tables(),
        hy_conv=hy_conv, hy_bias=hy_bias,
    )
    hy_params = (jnp.pad(hy_w1, ((0, 0), (0, 128 - HY_EMB), (0, 0))), hy_b1[:, None], hy_w2, hy_b2[:, None], hy_w3,
                 hy_freq, hy_decay[:, None])
    for name, s in (("hy_ctx", seq_ctx), ("hy_lat", seq_lat)):
        cf, sf = _dft_tables(s)
        P[name] = (cf, sf) + tuple(_hy_filter(s, *_hy_features(s), hy_params, cf, sf))

    tables = _rope_tables(seq_lat)
    mla_ctx = jnp.concatenate([cache_mla_ckv, _pad_last(cache_mla_kpe, 128)], axis=-1)
    swa_ctx = (cache_swa_k.reshape(n_lat_b, DEPTH, past, 128), cache_swa_v.reshape(n_lat_b, DEPTH, past, 128))
    lat_tiles = seq_lat // TOKEN_TILE

    xp = x_prompt.reshape(n_ctx_b * seq_ctx, d)
    xs = x_sample.reshape(n_lat_b * seq_lat, d)
    news = []
    for l in range(DEPTH):
        xp, new = _layer_pass(xp, n_ctx_b, seq_ctx, l, mod, _mod_spec(l, 0, None), P)
        news.append(new)
        xs, _ = _layer_pass(xs, n_lat_b, seq_lat, l, mod, _mod_spec(l, 1, lat_tiles), P, mla_ctx=mla_ctx,
                            swa_ctx=swa_ctx, state=state_gdn, tables=tables)

    stacked = tuple(jnp.stack([news[l][i] for l in range(DEPTH)], axis=1) for i in range(5))
    return (xp.reshape(n_ctx_b, seq_ctx, d), xs.reshape(n_lat_b, seq_lat, d)) + stacked
```

```python
import functools
import math

import jax
import jax.numpy as jnp
import numpy as np
from jax import lax
from jax.experimental import pallas as pl
from jax.experimental.pallas import tpu as pltpu

F32 = jnp.float32
BF16 = jnp.bfloat16
HIGHEST = lax.Precision.HIGHEST

D_MODEL = 1024
DEPTH = 2
GRID_W = 64
HEADS = 4
HEAD_DIM = 64
GROUP_W = 256
MLA_NOPE = 64
MLA_ROPE = 32
MLA_QK = 96
MLA_KV_RANK = 128
SWA_KV_HEADS = 2
SWA_WINDOW = 128
GDN_CHUNK = 64
HY_BANDS = 8
HY_EMB = 17
HY_FF = 64
D_FF = 4096
ROPE_BASE = 10000.0
EPS = 1e-6

W_MLA = 768
W_SWA = 512
W_GDN = 1152
W_HY = 768
W_ALL = W_MLA + W_SWA + W_GDN + W_HY

TOKEN_TILE = 512
VMEM_LIMIT = 56 * 1024 * 1024

_NT = (((1,), (1,)), ((), ()))


def _params(sem):
    return pltpu.CompilerParams(dimension_semantics=sem, vmem_limit_bytes=VMEM_LIMIT)


def _const_spec(shape):
    nd = len(shape)
    return pl.BlockSpec(shape, lambda *_: (0,) * nd, pipeline_mode=pl.Buffered(1))


def _layer_spec(arr, l):
    shape = arr.shape[1:]
    return pl.BlockSpec((None,) + shape, lambda *_: (l,) + (0,) * len(shape), pipeline_mode=pl.Buffered(1))


def _dot(a, b, precision=None):
    if precision is None:
        a, b = a.astype(BF16), b.astype(BF16)
    return jnp.dot(a, b, preferred_element_type=F32, precision=precision)


def _dot_g(a, b, dims):
    return lax.dot_general(a.astype(BF16), b.astype(BF16), dims, preferred_element_type=F32)


_B_NN = (((2,), (1,)), ((0,), (0,)))
_B_NT = (((2,), (2,)), ((0,), (0,)))
_B_TN = (((1,), (1,)), ((0,), (0,)))


def _bmm(a, b, dims):
    return lax.dot_general(a.astype(BF16), b.astype(BF16), dims, preferred_element_type=F32)


def _split_bf16(x, parts):
    out = []
    for _ in range(parts):
        p = x.astype(BF16)
        out.append(p)
        x = x - p.astype(F32)
    return out


def _rms(x, g):
    return x * lax.rsqrt(jnp.mean(x * x, axis=-1, keepdims=True) + EPS) * g


def _silu(x):
    return x * jax.nn.sigmoid(x)


def _swap_pairs(x):
    n = x.shape[-1]
    nxt = pltpu.roll(x, n - 1, axis=1)
    prv = pltpu.roll(x, 1, axis=1)
    lane = lax.broadcasted_iota(jnp.int32, x.shape, 1)
    return jnp.where((lane & 1) == 0, nxt, prv)


def _rope(x, cos, sin_signed):
    return x * cos + _swap_pairs(x) * sin_signed


def _mod_body(c_ref, w_ref, b_ref, o_ref):
    s = _silu(c_ref[...]).astype(BF16)
    o_ref[...] = _dot(s, w_ref[...].astype(BF16)) + b_ref[...]


def _modulation(cond8, w_ada, b_ada):
    n = 6 * D_MODEL
    tn = 1536
    return pl.pallas_call(
        _mod_body,
        out_shape=jax.ShapeDtypeStruct((DEPTH, 8, n), F32),
        grid=(DEPTH, n // tn),
        in_specs=[
            pl.BlockSpec((8, D_MODEL), lambda l, j: (0, 0)),
            pl.BlockSpec((None, D_MODEL, tn), lambda l, j: (l, 0, j)),
            pl.BlockSpec((None, 1, tn), lambda l, j: (l, 0, j)),
        ],
        out_specs=pl.BlockSpec((None, 8, tn), lambda l, j: (l, 0, j)),
        compiler_params=_params(("arbitrary", "arbitrary")),
        name="modulation",
    )(cond8, w_ada, b_ada.reshape(DEPTH, 1, n))


def _mod_spec(l, row0, tiles_per_row):
    blk = (None, None, 6, 1, D_MODEL)
    if tiles_per_row is None:
        return pl.BlockSpec(blk, lambda i: (l, row0, 0, 0, 0))
    return pl.BlockSpec(blk, lambda i: (l, row0 + i // tiles_per_row, 0, 0, 0))


def _inproj_body(x_ref, mod_ref, g_ref, w_ref, wgt_ref, o_mla, o_swa, o_gdn, o_hy, o_gate_t):
    h = _rms(x_ref[...], g_ref[...]) * (1.0 + mod_ref[1]) + mod_ref[0]
    hb = h.astype(BF16)
    off = 0
    for o in (o_mla, o_swa, o_gdn, o_hy):
        n = o.shape[-1]
        o[...] = _dot(hb, w_ref[:, off:off + n])
        off += n
    o_gate_t[...] = _dot_g(wgt_ref[...], hb, _NT)


def _inproj(x, mod, mod_spec, l, g, w, w_gate_t):
    t = x.shape[0]
    tm = TOKEN_TILE
    widths = (W_MLA, W_SWA, W_GDN, W_HY)
    return pl.pallas_call(
        _inproj_body,
        out_shape=[jax.ShapeDtypeStruct((t, n), F32) for n in widths] + [jax.ShapeDtypeStruct((16, t), F32)],
        grid=(t // tm,),
        in_specs=[
            pl.BlockSpec((tm, D_MODEL), lambda i: (i, 0)),
            mod_spec,
            _layer_spec(g, l), _layer_spec(w, l), _layer_spec(w_gate_t, l),
        ],
        out_specs=[pl.BlockSpec((tm, n), lambda i: (i, 0)) for n in widths]
        + [pl.BlockSpec((16, tm), lambda i: (0, i))],
        compiler_params=_params(("arbitrary",)),
        name="inproj",
    )(x, mod, g, w, w_gate_t)


def _outmlp_body(oa, ob, oc, od, x_ref, mod_ref, g_post_mix, g_pre_mlp, g_post_mlp,
                 wo_ref, w1_ref, w2_ref, out_ref):
    o = (_dot(oa[...], wo_ref[0]) + _dot(ob[...], wo_ref[1])
         + _dot(oc[...], wo_ref[2]) + _dot(od[...], wo_ref[3]))
    x = x_ref[...] + mod_ref[2] * _rms(o, g_post_mix[...])
    hb = (_rms(x, g_pre_mlp[...]) * (1.0 + mod_ref[4]) + mod_ref[3]).astype(BF16)
    acc = jnp.zeros(x.shape, F32)
    fc = 1024
    for c in range(D_FF // fc):
        a = _dot(hb, w1_ref[:, c * fc:(c + 1) * fc])
        a = jnp.square(jnp.maximum(a, 0.0)).astype(BF16)
        acc = acc + _dot(a, w2_ref[c * fc:(c + 1) * fc, :])
    out_ref[...] = x + mod_ref[5] * _rms(acc, g_post_mlp[...])


def _outmlp(o_parts, x, mod, mod_spec, l, g_post_mix, g_pre_mlp, g_post_mlp, wo, w1, w2):
    t = x.shape[0]
    tm = TOKEN_TILE
    part_spec = pl.BlockSpec((tm, GROUP_W), lambda i: (i, 0))
    return pl.pallas_call(
        _outmlp_body,
        out_shape=jax.ShapeDtypeStruct((t, D_MODEL), F32),
        grid=(t // tm,),
        in_specs=[part_spec] * 4 + [
            pl.BlockSpec((tm, D_MODEL), lambda i: (i, 0)),
            mod_spec,
        ] + [_layer_spec(a, l) for a in (g_post_mix, g_pre_mlp, g_post_mlp, wo, w1, w2)],
        out_specs=pl.BlockSpec((tm, D_MODEL), lambda i: (i, 0)),
        compiler_params=_params(("arbitrary",)),
        name="outproj_mlp",
    )(*o_parts, x, mod, g_post_mix, g_pre_mlp, g_post_mlp, wo, w1, w2)


def _sum_lane(h):
    return ((h + 1) % HEADS) * HEAD_DIM


def _rows(ref, j, n):
    return ref.at[pl.ds(j * n, n), :]


def _mla_body(*refs, seq, n_ctx, rope, emit_ckv, row_chunk, sub):
    it = iter(refs)
    q_ref, kv_ref = next(it), next(it)
    ctx_ref = next(it) if n_ctx else None
    g_ref, w_ref = next(it), next(it)
    tabs = tuple(next(it) for _ in range(4)) if rope else None
    o_ref = next(it)
    ckv_out = next(it) if emit_ckv else None
    k_s, v_s = next(it), next(it)
    tq = q_ref.shape[0] // sub
    for j in range(sub):
        _mla_one(_rows(q_ref, j, tq), _rows(kv_ref, j, seq), ctx_ref, g_ref, w_ref, tabs, _rows(o_ref, j, tq),
                 _rows(ckv_out, j, seq) if emit_ckv else None, k_s.at[j], v_s.at[j],
                 seq=seq, n_ctx=n_ctx, rope=rope, emit_ckv=emit_ckv, row_chunk=row_chunk)


def _mla_one(q_ref, kv_ref, ctx_ref, g_ref, w_ref, tabs, o_ref, ckv_out, k_s, v_s, *,
             seq, n_ctx, rope, emit_ckv, row_chunk):
    if rope:
        cq_ref, sq_ref, ck_ref, sk_ref = tabs

    def expand(kin, r0, n):
        kv = _dot(kin.astype(BF16), w_ref[...])
        k_s[r0:r0 + n, :] = kv[:, :HEADS * 128].astype(BF16)
        for h in range(HEADS):
            c0 = HEADS * 128 + h * GROUP_W
            v_s[h, r0:r0 + n, :] = kv[:, c0:c0 + GROUP_W].astype(BF16)

    @pl.when(pl.program_id(1) == 0)
    def _prep():
        for r0 in range(0, seq, row_chunk):
            blk = kv_ref[r0:r0 + row_chunk, :]
            cn = _rms(blk[:, :MLA_KV_RANK], g_ref[...])
            pe = blk[:, MLA_KV_RANK:]
            if rope:
                pe = _rope(pe, ck_ref[r0:r0 + row_chunk, :], sk_ref[r0:r0 + row_chunk, :])
            if emit_ckv:
                ckv_out[r0:r0 + row_chunk, :] = cn
            pe = jnp.where(lax.broadcasted_iota(jnp.int32, pe.shape, 1) == 127, 1.0, pe)
            expand(jnp.concatenate([cn, pe], axis=1), r0, row_chunk)
        if n_ctx:
            expand(ctx_ref[...], seq, n_ctx)

    scale = MLA_QK ** -0.5
    lane_head = lax.broadcasted_iota(jnp.int32, o_ref.shape, 1) // HEAD_DIM
    acc = jnp.zeros(o_ref.shape, F32)
    for h in range(HEADS):
        qh = q_ref[:, h * 128:(h + 1) * 128]
        if rope:
            qh = _rope(qh, cq_ref[...], sq_ref[...])
        s = _dot_g(qh * scale, k_s[:, h * 128:(h + 1) * 128], _NT)
        p = jnp.exp(s - jnp.max(s, axis=-1, keepdims=True))
        pv = _dot(p, v_s[h])
        inv = 1.0 / pv[:, _sum_lane(h):_sum_lane(h) + 1]
        acc = acc + jnp.where(lane_head == h, pv * inv, 0.0)
    o_ref[...] = acc.astype(BF16)


def _mla(proj, n_batch, seq, l, g, w, ctx=None, tables=None, emit_ckv=False):
    tq = min(seq, 512)
    nq = seq // tq
    sub = 4 if nq == 1 else 1
    n_ctx = 0 if ctx is None else ctx.shape[2]
    rope = tables is not None
    lk = seq + n_ctx
    in_specs = [
        pl.BlockSpec((sub * tq, HEADS * 128), lambda b, i: (b * nq + i, 0)),
        pl.BlockSpec((sub * seq, 256), lambda b, i: (b, 2)),
    ]
    args = [proj, proj]
    if n_ctx:
        in_specs.append(pl.BlockSpec((None, None, n_ctx, 256), lambda b, i: (b, l, 0, 0)))
        args.append(ctx)
    in_specs += [_layer_spec(g, l), _layer_spec(w, l)]
    args += [g, w]
    if rope:
        in_specs += [pl.BlockSpec((tq, 128), lambda b, i: (i, 0))] * 2 + [_const_spec((seq, 128))] * 2
        args += list(tables)
    out_shape = [jax.ShapeDtypeStruct((n_batch * seq, GROUP_W), BF16)]
    out_specs = [pl.BlockSpec((sub * tq, GROUP_W), lambda b, i: (b * nq + i, 0))]
    if emit_ckv:
        out_shape.append(jax.ShapeDtypeStruct((n_batch * seq, MLA_KV_RANK), F32))
        out_specs.append(pl.BlockSpec((sub * seq, MLA_KV_RANK), lambda b, i: (b, 0)))
    body = functools.partial(_mla_body, seq=seq, n_ctx=n_ctx, rope=rope, emit_ckv=emit_ckv,
                             row_chunk=min(seq, 512), sub=sub)
    return pl.pallas_call(
        body, out_shape=out_shape, grid=(n_batch // sub, nq), in_specs=in_specs, out_specs=out_specs,
        scratch_shapes=[pltpu.VMEM((sub, lk, HEADS * 128), BF16), pltpu.VMEM((sub, HEADS, lk, GROUP_W), BF16)],
        compiler_params=_params(("arbitrary", "arbitrary")),
        name="mla_latent" if rope else "mla_context",
    )(*args)


def _swa_body(*refs, seq, n_ctx, local, row_chunk, sub):
    it = iter(refs)
    q_ref, kv_ref = next(it), next(it)
    ctx_refs = (next(it), next(it)) if n_ctx else None
    consts = (next(it), next(it), next(it))
    tabs = tuple(next(it) for _ in range(4)) if local else None
    o_ref = next(it)
    k_s, v_s = next(it), next(it)
    tq = q_ref.shape[0] // sub
    for j in range(sub):
        _swa_one(_rows(q_ref, j, tq), _rows(kv_ref, j, seq), ctx_refs, consts, tabs, _rows(o_ref, j, tq),
                 k_s.at[j], v_s.at[j], seq=seq, n_ctx=n_ctx, local=local, row_chunk=row_chunk)


def _swa_one(q_ref, kv_ref, ctx_refs, consts, tabs, o_ref, k_s, v_s, *, seq, n_ctx, local, row_chunk):
    if n_ctx:
        kc_ref, vc_ref = ctx_refs
    sink_ref, pk_ref, pv_ref = consts
    if local:
        cq_ref, sq_ref, ck_ref, sk_ref = tabs
    tq = q_ref.shape[0]
    blk_id = pl.program_id(1)

    def expand(k, v, r0, n):
        k_s[r0:r0 + n, :] = _dot(k.astype(BF16), pk_ref[...]).astype(BF16)
        vb = v.astype(BF16)
        lane = lax.broadcasted_iota(jnp.int32, (n, GROUP_W), 1)
        for h in range(HEADS):
            ve = _dot(vb, pv_ref[:, h * GROUP_W:(h + 1) * GROUP_W])
            v_s[h, r0:r0 + n, :] = jnp.where(lane == _sum_lane(h), 1.0, ve).astype(BF16)

    @pl.when(blk_id == 0)
    def _prep():
        for r0 in range(0, seq, row_chunk):
            k = kv_ref[r0:r0 + row_chunk, 0:128]
            v = kv_ref[r0:r0 + row_chunk, 128:256]
            if local:
                k = _rope(k, ck_ref[r0:r0 + row_chunk, :], sk_ref[r0:r0 + row_chunk, :])
            expand(k, v, r0, row_chunk)
        if n_ctx:
            expand(kc_ref[...], vc_ref[...], seq, n_ctx)

    scale = HEAD_DIM ** -0.5
    q = q_ref[...]
    if local:
        q = _rope(q, cq_ref[...], sq_ref[...])
        win = tq + 2 * SWA_WINDOW
        start = jnp.clip(blk_id * tq - SWA_WINDOW, 0, seq - win)
        start = pl.multiple_of(start, SWA_WINDOW)
        qpos = blk_id * tq + lax.broadcasted_iota(jnp.int32, (tq, win), 0)
        kpos = start + lax.broadcasted_iota(jnp.int32, (tq, win), 1)
        valid = jnp.abs(qpos - kpos) <= SWA_WINDOW
    q = q * scale
    lane_head = lax.broadcasted_iota(jnp.int32, q.shape, 1) // HEAD_DIM
    acc = jnp.zeros(o_ref.shape, F32)
    for h in range(HEADS):
        qm = jnp.where(lane_head == h, q, 0.0).astype(BF16)
        sink = sink_ref[:, h:h + 1]
        if local:
            s_loc = jnp.where(valid, _dot_g(qm, k_s[pl.ds(start, win), :], _NT), -jnp.inf)
            s_ctx = _dot_g(qm, k_s[seq:seq + n_ctx, :], _NT)
            m = jnp.maximum(jnp.maximum(jnp.max(s_loc, axis=-1, keepdims=True),
                                        jnp.max(s_ctx, axis=-1, keepdims=True)), sink)
            o = (_dot(jnp.exp(s_loc - m), v_s[h, pl.ds(start, win), :])
                 + _dot(jnp.exp(s_ctx - m), v_s[h, seq:seq + n_ctx, :]))
        else:
            s = _dot_g(qm, k_s[...], _NT)
            m = jnp.maximum(jnp.max(s, axis=-1, keepdims=True), sink)
            o = _dot(jnp.exp(s - m), v_s[h])
        den = o[:, _sum_lane(h):_sum_lane(h) + 1] + jnp.exp(sink - m)
        acc = acc + jnp.where(lane_head == h, o * (1.0 / den), 0.0)
    o_ref[...] = acc.astype(BF16)


def _swa(proj, n_batch, seq, l, sink, pk, pv, ctx_kv=None, tables=None):
    local = tables is not None
    tq = 2 * SWA_WINDOW if local else seq
    nq = seq // tq
    sub = 4 if nq == 1 else 1
    n_ctx = 0 if ctx_kv is None else ctx_kv[0].shape[2]
    lk = seq + n_ctx
    in_specs = [
        pl.BlockSpec((sub * tq, GROUP_W), lambda b, i: (b * nq + i, 0)),
        pl.BlockSpec((sub * seq, 256), lambda b, i: (b, 1)),
    ]
    args = [proj, proj]
    if n_ctx:
        in_specs += [pl.BlockSpec((None, None, n_ctx, 128), lambda b, i: (b, l, 0, 0))] * 2
        args += list(ctx_kv)
    in_specs += [_layer_spec(sink, l), _const_spec(pk.shape), _const_spec(pv.shape)]
    args += [sink, pk, pv]
    if local:
        in_specs += [pl.BlockSpec((tq, GROUP_W), lambda b, i: (i, 0))] * 2 + [_const_spec((seq, 128))] * 2
        args += list(tables)
    body = functools.partial(_swa_body, seq=seq, n_ctx=n_ctx, local=local, row_chunk=min(seq, 512), sub=sub)
    return pl.pallas_call(
        body,
        out_shape=jax.ShapeDtypeStruct((n_batch * seq, GROUP_W), BF16),
        grid=(n_batch // sub, nq), in_specs=in_specs,
        out_specs=pl.BlockSpec((sub * tq, GROUP_W), lambda b, i: (b * nq + i, 0)),
        scratch_shapes=[pltpu.VMEM((sub, lk, GROUP_W), BF16), pltpu.VMEM((sub, HEADS, lk, GROUP_W), BF16)],
        compiler_params=_params(("arbitrary", "arbitrary")),
        name="swa_latent" if local else "swa_context",
    )(*args)


def _conv3(x, w_ref):
    n = x.shape[0]
    row = lax.broadcasted_iota(jnp.int32, x.shape, 0)
    prev = jnp.where(row == 0, 0.0, pltpu.roll(x, 1, axis=0))
    nxt = jnp.where(row == n - 1, 0.0, pltpu.roll(x, n - 1, axis=0))
    return prev * w_ref[0:1, :] + x * w_ref[1:2, :] + nxt * w_ref[2:3, :]


def _gdn_body(*refs, seq, has_state, emit_state):
    it = iter(refs)
    x_ref, gt_ref = next(it), next(it)
    s0_ref = next(it) if has_state else None
    cw_ref, gpar_ref, gpart_ref, gnorm_ref, ones_ref, tri_ref, trit_ref = (next(it) for _ in range(7))
    o_ref = next(it)
    st_ref = next(it) if emit_state else None
    q_s, k_s, v_s, of_s, ob_s, dec_s, dect_s, gate_s, s_s = (next(it) for _ in range(9))
    st_uv, st_k, st_d, st_w, st_qd, st_attn = (next(it) for _ in range(6))
    pair = 2 * GDN_CHUNK
    n_pair = seq // pair

    def conv_act(g):
        cols = slice(g * GROUP_W, (g + 1) * GROUP_W)
        return _silu(_conv3(x_ref[:, cols], cw_ref.at[:, cols]))

    def head_sums(x):
        return _dot(jnp.concatenate(_split_bf16(x, 2), axis=1), ones_ref[...])

    def head_l2(x):
        return x * lax.rsqrt(head_sums(x * x) + EPS)

    q_s[...] = head_l2(conv_act(0)) * (HEAD_DIM ** -0.5)
    k_s[...] = head_l2(conv_act(1))
    v_s[...] = conv_act(2)
    gab = x_ref[:, 4 * GROUP_W:4 * GROUP_W + 128]
    glog = -jnp.exp(gpar_ref[0:1, :]) * jax.nn.softplus(gab + gpar_ref[1:2, :])
    gate_s[...] = jax.nn.sigmoid(gab)
    lane = lax.broadcasted_iota(jnp.int32, (pair, 128), 1)
    glogt = -jnp.exp(gpart_ref[:, 0:1]) * jax.nn.softplus(gt_ref[...] + gpart_ref[:, 1:2])
    rowi = lax.broadcasted_iota(jnp.int32, (16, pair), 0)
    g3 = _split_bf16(glog, 3)
    gt3 = _split_bf16(glogt, 3)
    for p in range(n_pair):
        r = slice(p * pair, (p + 1) * pair)
        gcol = jnp.concatenate([g[r, :] for g in g3], axis=0)
        grow = jnp.concatenate([g[:, r] for g in gt3], axis=1)
        dec_s[r, :] = jnp.where(lane < HEADS, _dot(tri_ref[0], gcol), _dot(tri_ref[1], gcol))
        dect_s[:, r] = jnp.where(rowi < HEADS, _dot(grow, trit_ref[1]), _dot(grow, trit_ref[0]))

    if has_state:
        s_s[...] = s0_ref[...]
    else:
        s_s[...] = jnp.zeros(s_s.shape, F32)

    ri = lax.broadcasted_iota(jnp.int32, (pair, pair), 0)
    ci = lax.broadcasted_iota(jnp.int32, (pair, pair), 1)
    same = (ri // GDN_CHUNK) == (ci // GDN_CHUNK)
    nb = 2 * HEADS
    both = lambda f, b: jnp.concatenate([jnp.broadcast_to(f, (HEADS, pair, pair)),
                                         jnp.broadcast_to(b, (HEADS, pair, pair))], axis=0)
    incl = both(same & (ri >= ci), same & (ri <= ci))
    strict = both(same & (ri > ci), same & (ri < ci))
    eye = (ri == ci).astype(F32)
    merge = [((ri // (2 * s)) == (ci // (2 * s))) & ((ri // s) != (ci // s))
             for s in (1, 2, 4, 8, 16, 32)]
    c = GDN_CHUNK

    def first_second(x):
        return (jnp.concatenate([x[:HEADS, :c], x[HEADS:, c:]], axis=0),
                jnp.concatenate([x[:HEADS, c:], x[HEADS:, :c]], axis=0))

    def row_order(first, second):
        return jnp.concatenate([jnp.concatenate([first[:HEADS], second[:HEADS]], axis=1),
                                jnp.concatenate([second[HEADS:], first[HEADS:]], axis=1)], axis=0)

    def rows_of(i):
        fwd, bwd = i * pair, (n_pair - 1 - i) * pair
        if not isinstance(i, int):
            fwd, bwd = pl.multiple_of(fwd, pair), pl.multiple_of(bwd, pair)
        return pl.ds(fwd, pair), pl.ds(bwd, pair)

    def solve_stage(i):
        rf, rb = rows_of(i)
        slot = i % 2

        def heads(ref):
            xf, xb = ref[rf, :], ref[rb, :]
            return jnp.stack([x[:, h * HEAD_DIM:(h + 1) * HEAD_DIM] for x in (xf, xb) for h in range(HEADS)])

        def cols(ref, off):
            xf, xb = ref[rf, :], ref[rb, :]
            return jnp.stack([xf[:, off + j:off + j + 1] for j in range(HEADS)]
                             + [xb[:, off + HEADS + j:off + HEADS + j + 1] for j in range(HEADS)])

        q, k, v = heads(q_s), heads(k_s), heads(v_s)
        beta = cols(gate_s, 8)
        dcol = cols(dec_s, 0)
        tf, tb = dect_s[:, rf], dect_s[:, rb]
        drow = jnp.stack([tf[j:j + 1, :] for j in range(HEADS)]
                         + [tb[HEADS + j:HEADS + j + 1, :] for j in range(HEADS)])
        gam = jnp.where(incl, jnp.exp(jnp.where(incl, dcol - drow, 0.0)), 0.0)
        kb = k * beta
        a = jnp.where(strict, _bmm(kb, k, _B_NT) * gam, 0.0)
        attn = _bmm(q, k, _B_NT) * gam
        t = eye - jnp.where(merge[0], a, 0.0)
        for m in merge[1:]:
            t = t - _bmm(_bmm(t, jnp.where(m, a, 0.0), _B_NN), t, _B_NN)
        edec = jnp.exp(dcol)
        st_uv[slot] = _bmm(t, v * beta, _B_NN)
        st_w[slot] = _bmm(t, kb * edec, _B_NN).astype(BF16)
        st_qd[slot] = (q * edec).astype(BF16)
        st_attn[slot] = attn.astype(BF16)
        st_k[slot] = k
        st_d[slot] = dcol

    def state_stage(i):
        rf, rb = rows_of(i)
        slot = i % 2
        uv1, uv2 = first_second(st_uv[slot])
        w1, w2 = first_second(st_w[slot])
        qd1, qd2 = first_second(st_qd[slot])
        k1, k2 = first_second(st_k[slot])
        d1, d2 = first_second(st_d[slot])
        dl1 = jnp.concatenate([d1[:HEADS, c - 1:c], d1[HEADS:, 0:1]], axis=0)
        dl2 = jnp.concatenate([d2[:HEADS, c - 1:c], d2[HEADS:, 0:1]], axis=0)
        s = s_s[...].reshape(nb, HEAD_DIM, HEAD_DIM)
        u1 = uv1 - _bmm(w1, s, _B_NN)
        o1 = _bmm(qd1, s, _B_NN)
        s = s * jnp.exp(dl1) + _bmm(k1 * jnp.exp(dl1 - d1), u1, _B_TN)
        u2 = uv2 - _bmm(w2, s, _B_NN)
        o2 = _bmm(qd2, s, _B_NN)
        s = s * jnp.exp(dl2) + _bmm(k2 * jnp.exp(dl2 - d2), u2, _B_TN)
        s_s[...] = s.reshape(2, HEADS, HEAD_DIM, HEAD_DIM)
        o = row_order(o1, o2) + _bmm(st_attn[slot], row_order(u1, u2), _B_NN)
        of_s[rf, :] = jnp.concatenate([o[h] for h in range(HEADS)], axis=1)
        ob_s[rb, :] = jnp.concatenate([o[HEADS + h] for h in range(HEADS)], axis=1)

    def loop_body(i, carry):
        state_stage(i)
        solve_stage(i + 1)
        return carry

    solve_stage(0)
    lax.fori_loop(0, n_pair - 1, loop_body, 0)
    state_stage(n_pair - 1)

    o = of_s[...] + ob_s[...]
    ms = head_sums(o * o) * (1.0 / HEAD_DIM)
    o = o * lax.rsqrt(ms + EPS) * gnorm_ref[...]
    o_ref[...] = (o * _silu(x_ref[:, 3 * GROUP_W:4 * GROUP_W])).astype(BF16)
    if emit_state:
        st_ref[...] = s_s[...]


def _gdn(proj, gate_t, n_batch, seq, l, layer_params, tables, state=None, emit_state=False):
    has_state = state is not None
    in_specs = [pl.BlockSpec((seq, W_GDN), lambda b: (b, 0), pipeline_mode=pl.Buffered(1)),
                pl.BlockSpec((16, seq), lambda b: (0, b))]
    args = [proj, gate_t]
    state_block = (None, 2, HEADS, HEAD_DIM, HEAD_DIM)
    if has_state:
        in_specs.append(pl.BlockSpec((None,) + state_block, lambda b: (b, l, 0, 0, 0, 0)))
        args.append(state)
    for p in layer_params:
        in_specs.append(_layer_spec(p, l))
        args.append(p)
    for c in tables:
        in_specs.append(_const_spec(c.shape))
        args.append(c)
    out_shape = [jax.ShapeDtypeStruct((n_batch * seq, GROUP_W), BF16)]
    out_specs = [pl.BlockSpec((seq, GROUP_W), lambda b: (b, 0))]
    if emit_state:
        out_shape.append(jax.ShapeDtypeStruct((n_batch, 2, HEADS, HEAD_DIM, HEAD_DIM), F32))
        out_specs.append(pl.BlockSpec(state_block, lambda b: (b, 0, 0, 0, 0)))
    lane_dense = pltpu.VMEM((seq, GROUP_W), F32)
    scratch = [lane_dense] * 5 + [
        pltpu.VMEM((seq, 128), F32), pltpu.VMEM((16, seq), F32), pltpu.VMEM((seq, 128), F32),
        pltpu.VMEM((2, HEADS, HEAD_DIM, HEAD_DIM), F32)]
    nb, pair = 2 * HEADS, 2 * GDN_CHUNK
    scratch += [pltpu.VMEM((2, nb, pair, HEAD_DIM), F32), pltpu.VMEM((2, nb, pair, HEAD_DIM), F32),
                pltpu.VMEM((2, nb, pair, 1), F32), pltpu.VMEM((2, nb, pair, HEAD_DIM), BF16),
                pltpu.VMEM((2, nb, pair, HEAD_DIM), BF16), pltpu.VMEM((2, nb, pair, pair), BF16)]
    body = functools.partial(_gdn_body, seq=seq, has_state=has_state, emit_state=emit_state)
    return pl.pallas_call(
        body, out_shape=out_shape, grid=(n_batch,), in_specs=in_specs, out_specs=out_specs,
        scratch_shapes=scratch, compiler_params=_params(("arbitrary",)),
        name="gdn_latent" if has_state else "gdn_context",
    )(*args)


def _hy_filter_body(feat_ref, dist_ref, w1_ref, b1_ref, w2_ref, b2_ref, w3_ref, freq_ref, decay_ref,
                    cf_ref, sf_ref, hr_ref, hi_ref, nyq_ref, *, seq):
    h = jnp.sin(freq_ref[0:1, :] * (_dot(feat_ref[...], w1_ref[...], HIGHEST) + b1_ref[...]))
    h = jnp.sin(freq_ref[1:2, :] * (_dot(h, w2_ref[...], HIGHEST) + b2_ref[...]))
    h = _dot(h, w3_ref[...], HIGHEST)
    filt = h * jnp.exp(-dist_ref[...] * jnp.abs(decay_ref[...]))
    n = 2 * seq
    t = lax.broadcasted_iota(jnp.int32, (seq, 1), 0)
    alt = jnp.where((t & 1) == 0, 1.0, -1.0)
    nyq_ref[...] = jnp.sum(filt * alt, axis=0, keepdims=True) * (1.0 / n)
    fb = filt.astype(BF16)
    tk = min(seq, 512)
    for k0 in range(0, seq, tk):
        hr = _dot(cf_ref[k0:k0 + tk, :], fb)
        hs = _dot(sf_ref[k0:k0 + tk, :], fb)
        k = k0 + lax.broadcasted_iota(jnp.int32, (tk, 1), 0)
        m4 = k & 3
        c4 = jnp.where(m4 == 0, 1.0, jnp.where(m4 == 2, -1.0, 0.0))
        s4 = jnp.where(m4 == 1, 1.0, jnp.where(m4 == 3, -1.0, 0.0))
        wgt = jnp.where(k == 0, 1.0 / n, 2.0 / n)
        hr_ref[k0:k0 + tk, :] = wgt * (hr * c4 + hs * s4)
        hi_ref[k0:k0 + tk, :] = wgt * (hr * s4 - hs * c4)


def _hy_filter(seq, feats, dist, layer_params, cf, sf):
    per_layer = lambda a: pl.BlockSpec((None,) + a.shape[1:], lambda l: (l,) + (0,) * (a.ndim - 1))
    out_block = lambda rows: pl.BlockSpec((None, rows, 2 * GROUP_W), lambda l: (l, 0, 0))
    return pl.pallas_call(
        functools.partial(_hy_filter_body, seq=seq),
        out_shape=[jax.ShapeDtypeStruct((DEPTH, seq, 2 * GROUP_W), F32),
                   jax.ShapeDtypeStruct((DEPTH, seq, 2 * GROUP_W), F32),
                   jax.ShapeDtypeStruct((DEPTH, 1, 2 * GROUP_W), F32)],
        grid=(DEPTH,),
        in_specs=[_const_spec(feats.shape), _const_spec(dist.shape)] + [per_layer(a) for a in layer_params]
        + [_const_spec(cf.shape), _const_spec(sf.shape)],
        out_specs=[out_block(seq), out_block(seq), out_block(1)],
        compiler_params=_params(("arbitrary",)),
        name="hyena_filter",
    )(feats, dist, *layer_params, cf, sf)


def _hy_conv_body(x_ref, cw_ref, bias_ref, cf_ref, sf_ref, hr_ref, hi_ref, nyq_ref, o_ref, yr_s, yi_s, *, seq, sub):
    for j in range(sub):
        _hy_conv_one(_rows(x_ref, j, seq), cw_ref, bias_ref, cf_ref, sf_ref, hr_ref, hi_ref, nyq_ref,
                     _rows(o_ref, j, seq), yr_s.at[j], yi_s.at[j], seq=seq)


def _hy_conv_one(x_ref, cw_ref, bias_ref, cf_ref, sf_ref, hr_ref, hi_ref, nyq_ref, o_ref, yr_s, yi_s, *, seq):
    t = lax.broadcasted_iota(jnp.int32, (seq, 1), 0)
    alt = jnp.where((t & 1) == 0, 1.0, -1.0)
    tk = min(seq, 512)

    def long_conv(a, o):
        cols = slice(o * GROUP_W, (o + 1) * GROUP_W)
        ab = a.astype(BF16)
        for k0 in range(0, seq, tk):
            ur = _dot(cf_ref[k0:k0 + tk, :], ab)
            us = _dot(sf_ref[k0:k0 + tk, :], ab)
            hr = hr_ref[k0:k0 + tk, cols]
            hi = hi_ref[k0:k0 + tk, cols]
            yr_s[k0:k0 + tk, :] = (ur * hr + us * hi).astype(BF16)
            yi_s[k0:k0 + tk, :] = (ur * hi - us * hr).astype(BF16)
        u_nyq = jnp.sum(a * alt, axis=0, keepdims=True)
        y = _dot(cf_ref[...], yr_s[...]) - _dot(sf_ref[...], yi_s[...])
        return y + alt * (u_nyq * nyq_ref[:, cols])

    v = _conv3(x_ref[:, 0:GROUP_W], cw_ref.at[:, 0:GROUP_W])
    x1 = _conv3(x_ref[:, GROUP_W:2 * GROUP_W], cw_ref.at[:, GROUP_W:2 * GROUP_W])
    z = x1 * (long_conv(v, 0) + v * bias_ref[0:1, :])
    x2 = _conv3(x_ref[:, 2 * GROUP_W:3 * GROUP_W], cw_ref.at[:, 2 * GROUP_W:3 * GROUP_W])
    o_ref[...] = (x2 * (long_conv(z, 1) + z * bias_ref[1:2, :])).astype(BF16)


def _hy_conv(proj, n_batch, seq, l, cw, bias, cf, sf, hr, hi, nyq):
    sub = 4 if seq <= 512 else 1
    return pl.pallas_call(
        functools.partial(_hy_conv_body, seq=seq, sub=sub),
        out_shape=jax.ShapeDtypeStruct((n_batch * seq, GROUP_W), BF16),
        grid=(n_batch // sub,),
        in_specs=[pl.BlockSpec((sub * seq, W_HY), lambda b: (b, 0)),
                  _layer_spec(cw, l), _layer_spec(bias, l), _const_spec(cf.shape), _const_spec(sf.shape),
                  _layer_spec(hr, l), _layer_spec(hi, l), _layer_spec(nyq, l)],
        out_specs=pl.BlockSpec((sub * seq, GROUP_W), lambda b: (b, 0)),
        scratch_shapes=[pltpu.VMEM((sub, seq, GROUP_W), BF16)] * 2,
        compiler_params=_params(("arbitrary",)),
        name="hyena_conv",
    )(proj, cw, bias, cf, sf, hr, hi, nyq)


def _rope_tables(seq):
    rows = seq // GRID_W
    row = np.repeat(np.arange(rows), GRID_W).astype(np.float64)
    col = np.tile(np.arange(GRID_W), rows).astype(np.float64)

    def pair_tables(dim):
        n_freq = dim // 4
        inv = (ROPE_BASE ** (-np.arange(n_freq, dtype=np.float32) / n_freq)).astype(np.float64)
        ang = np.concatenate([row[:, None] * inv, col[:, None] * inv], axis=-1).astype(np.float32)
        cos = np.repeat(np.cos(ang), 2, axis=-1)
        sin = np.repeat(np.sin(ang), 2, axis=-1) * np.tile(np.array([-1.0, 1.0], np.float32), dim // 2)
        return cos, sin

    ca, sa = pair_tables(MLA_ROPE)
    one = lambda n: np.ones((seq, n), np.float32)
    zero = lambda n: np.zeros((seq, n), np.float32)
    mla = (np.concatenate([one(MLA_NOPE), ca, one(32)], 1), np.concatenate([zero(MLA_NOPE), sa, zero(32)], 1),
           np.concatenate([ca, one(96)], 1), np.concatenate([sa, zero(96)], 1))
    cb, sb = pair_tables(HEAD_DIM)
    swa = (np.tile(cb, (1, HEADS)), np.tile(sb, (1, HEADS)),
           np.tile(cb, (1, SWA_KV_HEADS)), np.tile(sb, (1, SWA_KV_HEADS)))
    as_f32 = lambda ts: tuple(jnp.asarray(t, F32) for t in ts)
    return as_f32(mla), as_f32(swa)


def _dft_tables(seq):
    k = np.arange(seq, dtype=np.int64)
    ang = ((k[:, None] * k[None, :]) % (2 * seq)).astype(np.float64) * (math.pi / seq)
    return (jnp.asarray(np.cos(ang), F32).astype(BF16), jnp.asarray(np.sin(ang), F32).astype(BF16))


def _hy_features(seq):
    t = np.arange(seq, dtype=np.float32)
    t01 = t / np.float32(max(seq - 1, 1))
    w = (np.float32(2.0 * math.pi) * t / np.float32(seq)).astype(np.float64)
    bands = np.linspace(1e-4, HY_BANDS - 1, HY_BANDS, dtype=np.float32).astype(np.float64)
    feats = np.concatenate([t01[:, None].astype(np.float64), np.cos(w[:, None] * bands), -np.sin(w[:, None] * bands)],
                           axis=-1)
    feats = np.pad(feats, ((0, 0), (0, 128 - HY_EMB)))
    dist = (np.abs(t - (seq // 2)) / np.float32(seq / 2))[:, None]
    return jnp.asarray(feats, F32), jnp.asarray(dist, F32)


def _swa_placement():
    pk = np.zeros((128, GROUP_W), np.float32)
    pv = np.zeros((128, HEADS, GROUP_W), np.float32)
    group = HEADS // SWA_KV_HEADS
    for h in range(HEADS):
        for e in range(HEAD_DIM):
            src = (h // group) * HEAD_DIM + e
            pk[src, h * HEAD_DIM + e] = 1.0
            pv[src, h, h * HEAD_DIM + e] = 1.0
    return jnp.asarray(pk, BF16), jnp.asarray(pv.reshape(128, HEADS * GROUP_W), BF16)


def _gdn_tables():
    pair = 2 * GDN_CHUNK
    i = np.arange(pair)
    same = (i[:, None] // GDN_CHUNK) == (i[None, :] // GDN_CHUNK)
    lower = (same & (i[:, None] >= i[None, :])).astype(np.float32)
    upper = (same & (i[:, None] <= i[None, :])).astype(np.float32)
    j = np.arange(GROUP_W)
    ones_bd = ((j[:, None] // HEAD_DIM) == (j[None, :] // HEAD_DIM)).astype(np.float32)
    ones2 = np.concatenate([ones_bd] * 2, axis=0)
    tri3 = np.stack([np.concatenate([m] * 3, axis=1) for m in (lower, upper)])
    trit3 = np.stack([np.concatenate([m] * 3, axis=0) for m in (lower, upper)])
    return jnp.asarray(ones2, BF16), jnp.asarray(tri3, BF16), jnp.asarray(trit3, BF16)


def _pad_last(x, n):
    return jnp.pad(x, [(0, 0)] * (x.ndim - 1) + [(0, n - x.shape[-1])])


def _layout_w_in(w):
    dep, d, _ = w.shape
    mq = _pad_last(w[..., :384].reshape(dep, d, HEADS, MLA_QK), 128).reshape(dep, d, HEADS * 128)
    mla = _pad_last(jnp.concatenate([mq, w[..., 384:544]], axis=-1), W_MLA)
    gdn = _pad_last(w[..., 1056:2096], W_GDN)
    w_all = jnp.concatenate([mla, w[..., 544:1056], gdn, w[..., 2096:2864]], axis=-1).astype(BF16)
    w_gate_t = jnp.swapaxes(w[..., 2080:2096], 1, 2).astype(BF16)
    return w_all, w_gate_t


def _layout_w_ukv(w):
    dep = w.shape[0]
    w = w.reshape(dep, MLA_KV_RANK, HEADS, 128)
    wk = _pad_last(w[..., :MLA_NOPE], 128).reshape(dep, MLA_KV_RANK, HEADS * 128)
    place = np.zeros((128, HEADS, 128), np.float32)
    for j in range(MLA_ROPE):
        place[j, :, MLA_NOPE + j] = 1.0
    place = jnp.broadcast_to(jnp.asarray(place.reshape(128, HEADS * 128)), (dep, 128, HEADS * 128))
    wk = jnp.concatenate([wk, place], axis=1)
    eye = jnp.asarray(np.eye(HEADS, dtype=np.float32))
    wv = (w[..., MLA_NOPE:][:, :, :, None, :] * eye[None, None, :, :, None]).reshape(dep, MLA_KV_RANK, HEADS * GROUP_W)
    ones_route = np.zeros((128, HEADS * GROUP_W), np.float32)
    for h in range(HEADS):
        ones_route[127, h * GROUP_W + _sum_lane(h)] = 1.0
    wv = jnp.concatenate([wv, jnp.broadcast_to(jnp.asarray(ones_route), (dep,) + ones_route.shape)], axis=1)
    return jnp.concatenate([wk, wv], axis=-1).astype(BF16)


def _layer_pass(x, n_batch, seq, l, mod, mod_spec, P, mla_ctx=None, swa_ctx=None, state=None, tables=None):
    is_ctx = tables is None
    p_mla, p_swa, p_gdn, p_hy, gate_t = _inproj(x, mod, mod_spec, l, P["g_pre_mix"], P["w_in"], P["w_gate_t"])
    mla_out = _mla(p_mla, n_batch, seq, l, P["mla_kv_norm"], P["w_ukv"], ctx=mla_ctx,
                   tables=None if is_ctx else tables[0], emit_ckv=is_ctx)
    o_a = mla_out[0]
    o_b = _swa(p_swa, n_batch, seq, l, P["swa_sink"], P["swa_pk"], P["swa_pv"], ctx_kv=swa_ctx,
               tables=None if is_ctx else tables[1])
    gdn_out = _gdn(p_gdn, gate_t, n_batch, seq, l, P["gdn_params"], P["gdn_tables"], state=state, emit_state=is_ctx)
    o_c = gdn_out[0]
    o_d = _hy_conv(p_hy, n_batch, seq, l, P["hy_conv"], P["hy_bias"], *(P["hy_ctx"] if is_ctx else P["hy_lat"]))
    y = _outmlp((o_a, o_b, o_c, o_d), x, mod, mod_spec, l, P["g_post_mix"], P["g_pre_mlp"], P["g_post_mlp"],
                P["w_out"], P["mlp_w1"], P["mlp_w2"])
    if not is_ctx:
        return y, None
    kpe0 = HEADS * 128 + MLA_KV_RANK
    new = (mla_out[1].reshape(n_batch, seq, MLA_KV_RANK),
           p_mla[:, kpe0:kpe0 + MLA_ROPE].reshape(n_batch, seq, MLA_ROPE),
           p_swa[:, 256:384].reshape(n_batch, seq, SWA_KV_HEADS, HEAD_DIM),
           p_swa[:, 384:512].reshape(n_batch, seq, SWA_KV_HEADS, HEAD_DIM),
           gdn_out[1])
    return y, new


def kernel(x_prompt, x_sample, cache_mla_ckv, cache_mla_kpe, cache_swa_k, cache_swa_v, state_gdn, c, c_ctx, w_ada, b_ada, g_pre_mix, g_post_mix, g_pre_mlp, g_post_mlp, w_in, w_out, mla_kv_norm, mla_w_ukv, swa_sink, gdn_conv, gdn_a_log, gdn_dt_bias, gdn_norm, hy_conv, hy_w1, hy_b1, hy_w2, hy_b2, hy_w3, hy_freq, hy_decay, hy_bias, mlp_w1, mlp_w2):
    n_ctx_b, seq_ctx, d = x_prompt.shape
    n_lat_b, seq_lat, _ = x_sample.shape
    past = cache_mla_ckv.shape[2]

    cond8 = jnp.concatenate([c_ctx[None, :], c, jnp.zeros((8 - 1 - n_lat_b, d), F32)], axis=0)
    mod = _modulation(cond8, w_ada, b_ada).reshape(DEPTH, 8, 6, 1, d)

    w_in_all, w_gate_t = _layout_w_in(w_in)
    gates = jnp.stack([gdn_a_log.reshape(DEPTH, 8), gdn_dt_bias.reshape(DEPTH, 8)], axis=1)
    gates_t = jnp.pad(jnp.swapaxes(gates, 1, 2), ((0, 0), (0, 8), (0, 0)))
    swa_pk, swa_pv = _swa_placement()
    P = dict(
        g_pre_mix=g_pre_mix[:, None], g_post_mix=g_post_mix[:, None],
        g_pre_mlp=g_pre_mlp[:, None], g_post_mlp=g_post_mlp[:, None],
        w_in=w_in_all, w_gate_t=w_gate_t,
        w_out=w_out.reshape(DEPTH, HEADS, GROUP_W, d).astype(BF16),
        mlp_w1=mlp_w1.astype(BF16), mlp_w2=mlp_w2.astype(BF16),
        mla_kv_norm=mla_kv_norm[:, None], w_ukv=_layout_w_ukv(mla_w_ukv),
        swa_sink=_pad_last(swa_sink, 128)[:, None], swa_pk=swa_pk, swa_pv=swa_pv,
        gdn_params=(gdn_conv, _pad_last(gates, 128), _pad_last(gates_t, 128), jnp.tile(gdn_norm, (1, HEADS))[:, None]),
        gdn_tables=_gdn_tables(),
        hy_conv=hy_conv, hy_bias=hy_bias,
    )
    hy_params = (jnp.pad(hy_w1, ((0, 0), (0, 128 - HY_EMB), (0, 0))), hy_b1[:, None], hy_w2, hy_b2[:, None], hy_w3,
                 hy_freq, hy_decay[:, None])
    for name, s in (("hy_ctx", seq_ctx), ("hy_lat", seq_lat)):
        cf, sf = _dft_tables(s)
        P[name] = (cf, sf) + tuple(_hy_filter(s, *_hy_features(s), hy_params, cf, sf))

    tables = _rope_tables(seq_lat)
    mla_ctx = jnp.concatenate([cache_mla_ckv, _pad_last(cache_mla_kpe, 127),
                               jnp.ones(cache_mla_kpe.shape[:-1] + (1,), F32)], axis=-1)
    swa_ctx = (cache_swa_k.reshape(n_lat_b, DEPTH, past, 128), cache_swa_v.reshape(n_lat_b, DEPTH, past, 128))
    lat_tiles = seq_lat // TOKEN_TILE

    xp = x_prompt.reshape(n_ctx_b * seq_ctx, d)
    xs = x_sample.reshape(n_lat_b * seq_lat, d)
    news = []
    for l in range(DEPTH):
        xp, new = _layer_pass(xp, n_ctx_b, seq_ctx, l, mod, _mod_spec(l, 0, None), P)
        news.append(new)
        xs, _ = _layer_pass(xs, n_lat_b, seq_lat, l, mod, _mod_spec(l, 1, lat_tiles), P, mla_ctx=mla_ctx,
                            swa_ctx=swa_ctx, state=state_gdn, tables=tables)

    stacked = tuple(jnp.stack([news[l][i] for l in range(DEPTH)], axis=1) for i in range(5))
    return (xp.reshape(n_ctx_b, seq_ctx, d), xs.reshape(n_lat_b, seq_lat, d)) + stacked
```

```python
import functools
import math

import jax
import jax.numpy as jnp
import numpy as np
from jax import lax
from jax.experimental import pallas as pl
from jax.experimental.pallas import tpu as pltpu

F32 = jnp.float32
BF16 = jnp.bfloat16
HIGHEST = lax.Precision.HIGHEST

D_MODEL = 1024
DEPTH = 2
GRID_W = 64
HEADS = 4
HEAD_DIM = 64
GROUP_W = 256
MLA_NOPE = 64
MLA_ROPE = 32
MLA_QK = 96
MLA_KV_RANK = 128
SWA_KV_HEADS = 2
SWA_WINDOW = 128
GDN_CHUNK = 64
HY_BANDS = 8
HY_EMB = 17
HY_FF = 64
D_FF = 4096
ROPE_BASE = 10000.0
EPS = 1e-6

W_MLA = 768
W_SWA = 512
W_GDN = 1152
W_HY = 768
W_ALL = W_MLA + W_SWA + W_GDN + W_HY

TOKEN_TILE = 512
VMEM_LIMIT = 56 * 1024 * 1024

_NT = (((1,), (1,)), ((), ()))


def _params(sem):
    return pltpu.CompilerParams(dimension_semantics=sem, vmem_limit_bytes=VMEM_LIMIT)


def _const_spec(shape):
    nd = len(shape)
    return pl.BlockSpec(shape, lambda *_: (0,) * nd, pipeline_mode=pl.Buffered(1))


def _layer_spec(arr, l):
    shape = arr.shape[1:]
    return pl.BlockSpec((None,) + shape, lambda *_: (l,) + (0,) * len(shape), pipeline_mode=pl.Buffered(1))


def _dot(a, b, precision=None):
    if precision is None:
        a, b = a.astype(BF16), b.astype(BF16)
    return jnp.dot(a, b, preferred_element_type=F32, precision=precision)


def _dot_g(a, b, dims):
    return lax.dot_general(a.astype(BF16), b.astype(BF16), dims, preferred_element_type=F32)


_B_NN = (((2,), (1,)), ((0,), (0,)))
_B_NT = (((2,), (2,)), ((0,), (0,)))
_B_TN = (((1,), (1,)), ((0,), (0,)))


def _bmm(a, b, dims):
    return lax.dot_general(a.astype(BF16), b.astype(BF16), dims, preferred_element_type=F32)


def _split_bf16(x, parts):
    out = []
    for _ in range(parts):
        p = x.astype(BF16)
        out.append(p)
        x = x - p.astype(F32)
    return out


def _rms(x, g):
    return x * lax.rsqrt(jnp.mean(x * x, axis=-1, keepdims=True) + EPS) * g


def _silu(x):
    return x * jax.nn.sigmoid(x)


def _swap_pairs(x):
    n = x.shape[-1]
    nxt = pltpu.roll(x, n - 1, axis=1)
    prv = pltpu.roll(x, 1, axis=1)
    lane = lax.broadcasted_iota(jnp.int32, x.shape, 1)
    return jnp.where((lane & 1) == 0, nxt, prv)


def _rope(x, cos, sin_signed):
    return x * cos + _swap_pairs(x) * sin_signed


def _mod_body(c_ref, w_ref, b_ref, o_ref):
    s = _silu(c_ref[...]).astype(BF16)
    o_ref[...] = _dot(s, w_ref[...].astype(BF16)) + b_ref[...]


def _modulation(cond8, w_ada, b_ada):
    n = 6 * D_MODEL
    tn = 1536
    return pl.pallas_call(
        _mod_body,
        out_shape=jax.ShapeDtypeStruct((DEPTH, 8, n), F32),
        grid=(DEPTH, n // tn),
        in_specs=[
            pl.BlockSpec((8, D_MODEL), lambda l, j: (0, 0)),
            pl.BlockSpec((None, D_MODEL, tn), lambda l, j: (l, 0, j)),
            pl.BlockSpec((None, 1, tn), lambda l, j: (l, 0, j)),
        ],
        out_specs=pl.BlockSpec((None, 8, tn), lambda l, j: (l, 0, j)),
        compiler_params=_params(("arbitrary", "arbitrary")),
        name="modulation",
    )(cond8, w_ada, b_ada.reshape(DEPTH, 1, n))


def _mod_spec(l, row0, tiles_per_row):
    blk = (None, None, 6, 1, D_MODEL)
    if tiles_per_row is None:
        return pl.BlockSpec(blk, lambda i: (l, row0, 0, 0, 0))
    return pl.BlockSpec(blk, lambda i: (l, row0 + i // tiles_per_row, 0, 0, 0))


def _inproj_body(x_ref, mod_ref, g_ref, w_ref, wgt_ref, o_mla, o_swa, o_gdn, o_hy, o_gate_t):
    h = _rms(x_ref[...], g_ref[...]) * (1.0 + mod_ref[1]) + mod_ref[0]
    hb = h.astype(BF16)
    off = 0
    for o in (o_mla, o_swa, o_gdn, o_hy):
        n = o.shape[-1]
        o[...] = _dot(hb, w_ref[:, off:off + n])
        off += n
    o_gate_t[...] = _dot_g(wgt_ref[...], hb, _NT)


def _inproj(x, mod, mod_spec, l, g, w, w_gate_t):
    t = x.shape[0]
    tm = TOKEN_TILE
    widths = (W_MLA, W_SWA, W_GDN, W_HY)
    return pl.pallas_call(
        _inproj_body,
        out_shape=[jax.ShapeDtypeStruct((t, n), F32) for n in widths] + [jax.ShapeDtypeStruct((16, t), F32)],
        grid=(t // tm,),
        in_specs=[
            pl.BlockSpec((tm, D_MODEL), lambda i: (i, 0)),
            mod_spec,
            _layer_spec(g, l), _layer_spec(w, l), _layer_spec(w_gate_t, l),
        ],
        out_specs=[pl.BlockSpec((tm, n), lambda i: (i, 0)) for n in widths]
        + [pl.BlockSpec((16, tm), lambda i: (0, i))],
        compiler_params=_params(("arbitrary",)),
        name="inproj",
    )(x, mod, g, w, w_gate_t)


def _outmlp_body(oa, ob, oc, od, x_ref, mod_ref, g_post_mix, g_pre_mlp, g_post_mlp,
                 wo_ref, w1_ref, w2_ref, out_ref):
    o = (_dot(oa[...], wo_ref[0]) + _dot(ob[...], wo_ref[1])
         + _dot(oc[...], wo_ref[2]) + _dot(od[...], wo_ref[3]))
    x = x_ref[...] + mod_ref[2] * _rms(o, g_post_mix[...])
    hb = (_rms(x, g_pre_mlp[...]) * (1.0 + mod_ref[4]) + mod_ref[3]).astype(BF16)
    acc = jnp.zeros(x.shape, F32)
    fc = 1024
    for c in range(D_FF // fc):
        a = _dot(hb, w1_ref[:, c * fc:(c + 1) * fc])
        a = jnp.square(jnp.maximum(a, 0.0)).astype(BF16)
        acc = acc + _dot(a, w2_ref[c * fc:(c + 1) * fc, :])
    out_ref[...] = x + mod_ref[5] * _rms(acc, g_post_mlp[...])


def _outmlp(o_parts, x, mod, mod_spec, l, g_post_mix, g_pre_mlp, g_post_mlp, wo, w1, w2):
    t = x.shape[0]
    tm = TOKEN_TILE
    part_spec = pl.BlockSpec((tm, GROUP_W), lambda i: (i, 0))
    return pl.pallas_call(
        _outmlp_body,
        out_shape=jax.ShapeDtypeStruct((t, D_MODEL), F32),
        grid=(t // tm,),
        in_specs=[part_spec] * 4 + [
            pl.BlockSpec((tm, D_MODEL), lambda i: (i, 0)),
            mod_spec,
        ] + [_layer_spec(a, l) for a in (g_post_mix, g_pre_mlp, g_post_mlp, wo, w1, w2)],
        out_specs=pl.BlockSpec((tm, D_MODEL), lambda i: (i, 0)),
        compiler_params=_params(("arbitrary",)),
        name="outproj_mlp",
    )(*o_parts, x, mod, g_post_mix, g_pre_mlp, g_post_mlp, wo, w1, w2)


def _sum_lane(h):
    return ((h + 1) % HEADS) * HEAD_DIM


def _rows(ref, j, n):
    return ref.at[pl.ds(j * n, n), :]


def _mla_body(*refs, seq, n_ctx, rope, emit_ckv, row_chunk, sub):
    it = iter(refs)
    q_ref, kv_ref = next(it), next(it)
    ctx_ref = next(it) if n_ctx else None
    g_ref, w_ref = next(it), next(it)
    tabs = tuple(next(it) for _ in range(4)) if rope else None
    o_ref = next(it)
    ckv_out = next(it) if emit_ckv else None
    k_s, v_s = next(it), next(it)
    tq = q_ref.shape[0] // sub
    for j in range(sub):
        _mla_one(_rows(q_ref, j, tq), _rows(kv_ref, j, seq), ctx_ref, g_ref, w_ref, tabs, _rows(o_ref, j, tq),
                 _rows(ckv_out, j, seq) if emit_ckv else None, k_s.at[j], v_s.at[j],
                 seq=seq, n_ctx=n_ctx, rope=rope, emit_ckv=emit_ckv, row_chunk=row_chunk)


def _mla_one(q_ref, kv_ref, ctx_ref, g_ref, w_ref, tabs, o_ref, ckv_out, k_s, v_s, *,
             seq, n_ctx, rope, emit_ckv, row_chunk):
    if rope:
        cq_ref, sq_ref, ck_ref, sk_ref = tabs

    def expand(kin, r0, n):
        kv = _dot(kin.astype(BF16), w_ref[...])
        k_s[r0:r0 + n, :] = kv[:, :HEADS * 128].astype(BF16)
        for h in range(HEADS):
            c0 = HEADS * 128 + h * GROUP_W
            v_s[h, r0:r0 + n, :] = kv[:, c0:c0 + GROUP_W].astype(BF16)

    @pl.when(pl.program_id(1) == 0)
    def _prep():
        for r0 in range(0, seq, row_chunk):
            blk = kv_ref[r0:r0 + row_chunk, :]
            cn = _rms(blk[:, :MLA_KV_RANK], g_ref[...])
            pe = blk[:, MLA_KV_RANK:]
            if rope:
                pe = _rope(pe, ck_ref[r0:r0 + row_chunk, :], sk_ref[r0:r0 + row_chunk, :])
            if emit_ckv:
                ckv_out[r0:r0 + row_chunk, :] = cn
            pe = jnp.where(lax.broadcasted_iota(jnp.int32, pe.shape, 1) == 127, 1.0, pe)
            expand(jnp.concatenate([cn, pe], axis=1), r0, row_chunk)
        if n_ctx:
            expand(ctx_ref[...], seq, n_ctx)

    scale = MLA_QK ** -0.5
    lane_head = lax.broadcasted_iota(jnp.int32, o_ref.shape, 1) // HEAD_DIM
    acc = jnp.zeros(o_ref.shape, F32)
    for h in range(HEADS):
        qh = q_ref[:, h * 128:(h + 1) * 128]
        if rope:
            qh = _rope(qh, cq_ref[...], sq_ref[...])
        s = _dot_g(qh * scale, k_s[:, h * 128:(h + 1) * 128], _NT)
        p = jnp.exp(s - jnp.max(s, axis=-1, keepdims=True))
        pv = _dot(p, v_s[h])
        inv = 1.0 / pv[:, _sum_lane(h):_sum_lane(h) + 1]
        acc = acc + jnp.where(lane_head == h, pv * inv, 0.0)
    o_ref[...] = acc.astype(BF16)


def _mla(proj, n_batch, seq, l, g, w, ctx=None, tables=None, emit_ckv=False):
    tq = min(seq, 1024)
    nq = seq // tq
    sub = 4 if nq == 1 else 1
    n_ctx = 0 if ctx is None else ctx.shape[2]
    rope = tables is not None
    lk = seq + n_ctx
    in_specs = [
        pl.BlockSpec((sub * tq, HEADS * 128), lambda b, i: (b * nq + i, 0)),
        pl.BlockSpec((sub * seq, 256), lambda b, i: (b, 2)),
    ]
    args = [proj, proj]
    if n_ctx:
        in_specs.append(pl.BlockSpec((None, None, n_ctx, 256), lambda b, i: (b, l, 0, 0)))
        args.append(ctx)
    in_specs += [_layer_spec(g, l), _layer_spec(w, l)]
    args += [g, w]
    if rope:
        in_specs += [pl.BlockSpec((tq, 128), lambda b, i: (i, 0))] * 2 + [_const_spec((seq, 128))] * 2
        args += list(tables)
    out_shape = [jax.ShapeDtypeStruct((n_batch * seq, GROUP_W), BF16)]
    out_specs = [pl.BlockSpec((sub * tq, GROUP_W), lambda b, i: (b * nq + i, 0))]
    if emit_ckv:
        out_shape.append(jax.ShapeDtypeStruct((n_batch * seq, MLA_KV_RANK), F32))
        out_specs.append(pl.BlockSpec((sub * seq, MLA_KV_RANK), lambda b, i: (b, 0)))
    body = functools.partial(_mla_body, seq=seq, n_ctx=n_ctx, rope=rope, emit_ckv=emit_ckv,
                             row_chunk=min(seq, 512), sub=sub)
    return pl.pallas_call(
        body, out_shape=out_shape, grid=(n_batch // sub, nq), in_specs=in_specs, out_specs=out_specs,
        scratch_shapes=[pltpu.VMEM((sub, lk, HEADS * 128), BF16), pltpu.VMEM((sub, HEADS, lk, GROUP_W), BF16)],
        compiler_params=_params(("arbitrary", "arbitrary")),
        name="mla_latent" if rope else "mla_context",
    )(*args)


def _swa_body(*refs, seq, n_ctx, local, row_chunk, sub):
    it = iter(refs)
    q_ref, kv_ref = next(it), next(it)
    ctx_refs = (next(it), next(it)) if n_ctx else None
    consts = (next(it), next(it), next(it))
    tabs = tuple(next(it) for _ in range(4)) if local else None
    o_ref = next(it)
    k_s, v_s = next(it), next(it)
    tq = q_ref.shape[0] // sub
    for j in range(sub):
        _swa_one(_rows(q_ref, j, tq), _rows(kv_ref, j, seq), ctx_refs, consts, tabs, _rows(o_ref, j, tq),
                 k_s.at[j], v_s.at[j], seq=seq, n_ctx=n_ctx, local=local, row_chunk=row_chunk)


def _swa_one(q_ref, kv_ref, ctx_refs, consts, tabs, o_ref, k_s, v_s, *, seq, n_ctx, local, row_chunk):
    if n_ctx:
        kc_ref, vc_ref = ctx_refs
    sink_ref, pk_ref, pv_ref = consts
    if local:
        cq_ref, sq_ref, ck_ref, sk_ref = tabs
    tq = q_ref.shape[0]
    blk_id = pl.program_id(1)

    def expand(k, v, r0, n):
        k_s[r0:r0 + n, :] = _dot(k.astype(BF16), pk_ref[...]).astype(BF16)
        vb = v.astype(BF16)
        lane = lax.broadcasted_iota(jnp.int32, (n, GROUP_W), 1)
        for h in range(HEADS):
            ve = _dot(vb, pv_ref[:, h * GROUP_W:(h + 1) * GROUP_W])
            v_s[h, r0:r0 + n, :] = jnp.where(lane == _sum_lane(h), 1.0, ve).astype(BF16)

    @pl.when(blk_id == 0)
    def _prep():
        for r0 in range(0, seq, row_chunk):
            k = kv_ref[r0:r0 + row_chunk, 0:128]
            v = kv_ref[r0:r0 + row_chunk, 128:256]
            if local:
                k = _rope(k, ck_ref[r0:r0 + row_chunk, :], sk_ref[r0:r0 + row_chunk, :])
            expand(k, v, r0, row_chunk)
        if n_ctx:
            expand(kc_ref[...], vc_ref[...], seq, n_ctx)

    scale = HEAD_DIM ** -0.5
    q = q_ref[...]
    if local:
        q = _rope(q, cq_ref[...], sq_ref[...])
        win = tq + 2 * SWA_WINDOW
        start = jnp.clip(blk_id * tq - SWA_WINDOW, 0, seq - win)
        start = pl.multiple_of(start, SWA_WINDOW)
        qpos = blk_id * tq + lax.broadcasted_iota(jnp.int32, (tq, win), 0)
        kpos = start + lax.broadcasted_iota(jnp.int32, (tq, win), 1)
        valid = jnp.abs(qpos - kpos) <= SWA_WINDOW
    q = q * scale
    lane_head = lax.broadcasted_iota(jnp.int32, q.shape, 1) // HEAD_DIM
    acc = jnp.zeros(o_ref.shape, F32)
    for h in range(HEADS):
        qm = jnp.where(lane_head == h, q, 0.0).astype(BF16)
        sink = sink_ref[:, h:h + 1]
        if local:
            s_loc = jnp.where(valid, _dot_g(qm, k_s[pl.ds(start, win), :], _NT), -jnp.inf)
            s_ctx = _dot_g(qm, k_s[seq:seq + n_ctx, :], _NT)
            m = jnp.maximum(jnp.maximum(jnp.max(s_loc, axis=-1, keepdims=True),
                                        jnp.max(s_ctx, axis=-1, keepdims=True)), sink)
            o = (_dot(jnp.exp(s_loc - m), v_s[h, pl.ds(start, win), :])
                 + _dot(jnp.exp(s_ctx - m), v_s[h, seq:seq + n_ctx, :]))
        else:
            s = _dot_g(qm, k_s[...], _NT)
            m = jnp.maximum(jnp.max(s, axis=-1, keepdims=True), sink)
            o = _dot(jnp.exp(s - m), v_s[h])
        den = o[:, _sum_lane(h):_sum_lane(h) + 1] + jnp.exp(sink - m)
        acc = acc + jnp.where(lane_head == h, o * (1.0 / den), 0.0)
    o_ref[...] = acc.astype(BF16)


def _swa(proj, n_batch, seq, l, sink, pk, pv, ctx_kv=None, tables=None):
    local = tables is not None
    tq = 2 * SWA_WINDOW if local else seq
    nq = seq // tq
    sub = 4 if nq == 1 else 1
    n_ctx = 0 if ctx_kv is None else ctx_kv[0].shape[2]
    lk = seq + n_ctx
    in_specs = [
        pl.BlockSpec((sub * tq, GROUP_W), lambda b, i: (b * nq + i, 0)),
        pl.BlockSpec((sub * seq, 256), lambda b, i: (b, 1)),
    ]
    args = [proj, proj]
    if n_ctx:
        in_specs += [pl.BlockSpec((None, None, n_ctx, 128), lambda b, i: (b, l, 0, 0))] * 2
        args += list(ctx_kv)
    in_specs += [_layer_spec(sink, l), _const_spec(pk.shape), _const_spec(pv.shape)]
    args += [sink, pk, pv]
    if local:
        in_specs += [pl.BlockSpec((tq, GROUP_W), lambda b, i: (i, 0))] * 2 + [_const_spec((seq, 128))] * 2
        args += list(tables)
    body = functools.partial(_swa_body, seq=seq, n_ctx=n_ctx, local=local, row_chunk=min(seq, 512), sub=sub)
    return pl.pallas_call(
        body,
        out_shape=jax.ShapeDtypeStruct((n_batch * seq, GROUP_W), BF16),
        grid=(n_batch // sub, nq), in_specs=in_specs,
        out_specs=pl.BlockSpec((sub * tq, GROUP_W), lambda b, i: (b * nq + i, 0)),
        scratch_shapes=[pltpu.VMEM((sub, lk, GROUP_W), BF16), pltpu.VMEM((sub, HEADS, lk, GROUP_W), BF16)],
        compiler_params=_params(("arbitrary", "arbitrary")),
        name="swa_latent" if local else "swa_context",
    )(*args)


def _conv3(x, w_ref):
    n = x.shape[0]
    row = lax.broadcasted_iota(jnp.int32, x.shape, 0)
    prev = jnp.where(row == 0, 0.0, pltpu.roll(x, 1, axis=0))
    nxt = jnp.where(row == n - 1, 0.0, pltpu.roll(x, n - 1, axis=0))
    return prev * w_ref[0:1, :] + x * w_ref[1:2, :] + nxt * w_ref[2:3, :]


def _gdn_body(*refs, seq, has_state, emit_state, sub):
    it = iter(refs)
    x_ref, gt_ref = next(it), next(it)
    s0_ref = next(it) if has_state else None
    consts = tuple(next(it) for _ in range(7))
    o_ref = next(it)
    st_ref = next(it) if emit_state else None
    scratch = tuple(it)
    for j in range(sub):
        _gdn_one(_rows(x_ref, j, seq), gt_ref.at[:, pl.ds(j * seq, seq)], s0_ref, consts, _rows(o_ref, j, seq),
                 st_ref.at[j] if emit_state else None, tuple(s.at[j] for s in scratch),
                 seq=seq, has_state=has_state, emit_state=emit_state)


def _gdn_one(x_ref, gt_ref, s0_ref, consts, o_ref, st_ref, scratch, *, seq, has_state, emit_state):
    cw_ref, gpar_ref, gpart_ref, gnorm_ref, ones_ref, tri_ref, trit_ref = consts
    q_s, k_s, v_s, of_s, ob_s, dec_s, dect_s, gate_s, s_s = scratch[:9]
    st_uv, st_k, st_d, st_w, st_qd, st_attn = scratch[9:]
    pair = 2 * GDN_CHUNK
    n_pair = seq // pair

    def conv_act(g):
        cols = slice(g * GROUP_W, (g + 1) * GROUP_W)
        return _silu(_conv3(x_ref[:, cols], cw_ref.at[:, cols]))

    def head_sums(x):
        return _dot(jnp.concatenate(_split_bf16(x, 2), axis=1), ones_ref[...])

    def head_l2(x):
        return x * lax.rsqrt(head_sums(x * x) + EPS)

    def to_heads(dst, x):
        for h in range(HEADS):
            dst[h] = x[:, h * HEAD_DIM:(h + 1) * HEAD_DIM]

    to_heads(q_s, head_l2(conv_act(0)) * (HEAD_DIM ** -0.5))
    to_heads(k_s, head_l2(conv_act(1)))
    to_heads(v_s, conv_act(2))
    gab = x_ref[:, 4 * GROUP_W:4 * GROUP_W + 128]
    glog = -jnp.exp(gpar_ref[0:1, :]) * jax.nn.softplus(gab + gpar_ref[1:2, :])
    gate_s[...] = jax.nn.sigmoid(gab)
    lane = lax.broadcasted_iota(jnp.int32, (pair, 128), 1)
    glogt = -jnp.exp(gpart_ref[:, 0:1]) * jax.nn.softplus(gt_ref[...] + gpart_ref[:, 1:2])
    rowi = lax.broadcasted_iota(jnp.int32, (16, pair), 0)
    g3 = _split_bf16(glog, 3)
    gt3 = _split_bf16(glogt, 3)
    for p in range(n_pair):
        r = slice(p * pair, (p + 1) * pair)
        gcol = jnp.concatenate([g[r, :] for g in g3], axis=0)
        grow = jnp.concatenate([g[:, r] for g in gt3], axis=1)
        dec_s[r, :] = jnp.where(lane < HEADS, _dot(tri_ref[0], gcol), _dot(tri_ref[1], gcol))
        dect_s[:, r] = jnp.where(rowi < HEADS, _dot(grow, trit_ref[1]), _dot(grow, trit_ref[0]))

    if has_state:
        s_s[...] = s0_ref[...]
    else:
        s_s[...] = jnp.zeros(s_s.shape, F32)

    ri = lax.broadcasted_iota(jnp.int32, (pair, pair), 0)
    ci = lax.broadcasted_iota(jnp.int32, (pair, pair), 1)
    same = (ri // GDN_CHUNK) == (ci // GDN_CHUNK)
    nb = 2 * HEADS
    incl = (same & (ri >= ci), same & (ri <= ci))
    strict = (same & (ri > ci), same & (ri < ci))
    eye = (ri == ci).astype(F32)
    merge = [((ri // (2 * s)) == (ci // (2 * s))) & ((ri // s) != (ci // s))
             for s in (1, 2, 4, 8, 16, 32)]
    c = GDN_CHUNK

    def first_second(x):
        return (jnp.concatenate([x[:HEADS, :c], x[HEADS:, c:]], axis=0),
                jnp.concatenate([x[:HEADS, c:], x[HEADS:, :c]], axis=0))

    def row_order(first, second):
        return jnp.concatenate([jnp.concatenate([first[:HEADS], second[:HEADS]], axis=1),
                                jnp.concatenate([second[HEADS:], first[HEADS:]], axis=1)], axis=0)

    def rows_of(i):
        fwd, bwd = i * pair, (n_pair - 1 - i) * pair
        if not isinstance(i, int):
            fwd, bwd = pl.multiple_of(fwd, pair), pl.multiple_of(bwd, pair)
        return pl.ds(fwd, pair), pl.ds(bwd, pair)

    def solve_stage(i, filler=iter(())):
        rows = rows_of(i)
        slot = i % 2
        ab, t, rhs_v, rhs_w = [], [], [], []
        for d in range(2):
            r = rows[d]
            gd_, dd_, td_ = gate_s[r, :], dec_s[r, :], dect_s[:, r]
            for h in range(HEADS):
                p, col = d * HEADS + h, d * HEADS + h
                q, k, v = q_s[h, r, :], k_s[h, r, :], v_s[h, r, :]
                beta = gd_[:, 8 + col:9 + col]
                dcol = dd_[:, col:col + 1]
                gam = jnp.where(incl[d], jnp.exp(jnp.where(incl[d], dcol - td_[col:col + 1, :], 0.0)), 0.0)
                kb = k * beta
                kk_qk = _dot_g(jnp.concatenate([kb, q], axis=0), k, _NT)
                a = jnp.where(strict[d], kk_qk[:pair] * gam, 0.0)
                edec = jnp.exp(dcol)
                st_attn[slot, p] = (kk_qk[pair:] * gam).astype(BF16)
                st_qd[slot, p] = (q * edec).astype(BF16)
                st_k[slot, p] = k
                st_d[slot, p] = dcol
                ab.append(a.astype(BF16))
                t.append((eye - jnp.where(merge[0], a, 0.0)).astype(BF16))
                rhs_v.append(v * beta)
                rhs_w.append(kb * edec)
        for m in merge[1:]:
            next(filler, None)
            x = [_dot(t[p], jnp.where(m, ab[p], 0.0)) for p in range(nb)]
            x = [_dot(x[p], t[p]) for p in range(nb)]
            t = [t[p] - x[p].astype(BF16) for p in range(nb)]
        for p in range(nb):
            st_uv[slot, p] = _dot(t[p], rhs_v[p])
            st_w[slot, p] = _dot(t[p], rhs_w[p]).astype(BF16)
        for _ in filler:
            pass

    def state_steps(i):
        rf, rb = rows_of(i)
        slot = i % 2
        uv1, uv2 = first_second(st_uv[slot])
        w1, w2 = first_second(st_w[slot])
        qd1, qd2 = first_second(st_qd[slot])
        k1, k2 = first_second(st_k[slot])
        d1, d2 = first_second(st_d[slot])
        dl1 = jnp.concatenate([d1[:HEADS, c - 1:c], d1[HEADS:, 0:1]], axis=0)
        dl2 = jnp.concatenate([d2[:HEADS, c - 1:c], d2[HEADS:, 0:1]], axis=0)
        s = s_s[...].reshape(nb, HEAD_DIM, HEAD_DIM)
        u1 = uv1 - _bmm(w1, s, _B_NN)
        o1 = _bmm(qd1, s, _B_NN)
        yield
        s = s * jnp.exp(dl1) + _bmm(k1 * jnp.exp(dl1 - d1), u1, _B_TN)
        yield
        u2 = uv2 - _bmm(w2, s, _B_NN)
        o2 = _bmm(qd2, s, _B_NN)
        yield
        s = s * jnp.exp(dl2) + _bmm(k2 * jnp.exp(dl2 - d2), u2, _B_TN)
        s_s[...] = s.reshape(2, HEADS, HEAD_DIM, HEAD_DIM)
        yield
        o = row_order(o1, o2) + _bmm(st_attn[slot], row_order(u1, u2), _B_NN)
        of_s[:, rf, :] = o[:HEADS]
        ob_s[:, rb, :] = o[HEADS:]

    def loop_body(i, carry):
        solve_stage(i + 1, state_steps(i))
        return carry

    solve_stage(0)
    if n_pair <= 2:
        for i in range(n_pair - 1):
            loop_body(i, 0)
    else:
        lax.fori_loop(0, n_pair - 1, loop_body, 0)
    for _ in state_steps(n_pair - 1):
        pass

    o = jnp.concatenate([of_s[h] + ob_s[h] for h in range(HEADS)], axis=1)
    ms = head_sums(o * o) * (1.0 / HEAD_DIM)
    o = o * lax.rsqrt(ms + EPS) * gnorm_ref[...]
    o_ref[...] = (o * _silu(x_ref[:, 3 * GROUP_W:4 * GROUP_W])).astype(BF16)
    if emit_state:
        st_ref[...] = s_s[...]


def _gdn(proj, gate_t, n_batch, seq, l, layer_params, tables, state=None, emit_state=False):
    has_state = state is not None
    sub = 1 if has_state else 2
    in_specs = [pl.BlockSpec((sub * seq, W_GDN), lambda b: (b, 0), pipeline_mode=pl.Buffered(1)),
                pl.BlockSpec((16, sub * seq), lambda b: (0, b))]
    args = [proj, gate_t]
    state_block = (sub, 2, HEADS, HEAD_DIM, HEAD_DIM)
    if has_state:
        in_specs.append(pl.BlockSpec((None, None) + state_block[1:], lambda b: (b, l, 0, 0, 0, 0)))
        args.append(state)
    for p in layer_params:
        in_specs.append(_layer_spec(p, l))
        args.append(p)
    for c in tables:
        in_specs.append(_const_spec(c.shape))
        args.append(c)
    out_shape = [jax.ShapeDtypeStruct((n_batch * seq, GROUP_W), BF16)]
    out_specs = [pl.BlockSpec((sub * seq, GROUP_W), lambda b: (b, 0))]
    if emit_state:
        out_shape.append(jax.ShapeDtypeStruct((n_batch, 2, HEADS, HEAD_DIM, HEAD_DIM), F32))
        out_specs.append(pl.BlockSpec(state_block, lambda b: (b, 0, 0, 0, 0)))
    vmem = lambda shape, dtype: pltpu.VMEM((sub,) + shape, dtype)
    scratch = [vmem((HEADS, seq, HEAD_DIM), F32)] * 5 + [
        vmem((seq, 128), F32), vmem((16, seq), F32), vmem((seq, 128), F32),
        vmem((2, HEADS, HEAD_DIM, HEAD_DIM), F32)]
    nb, pair = 2 * HEADS, 2 * GDN_CHUNK
    scratch += [vmem((2, nb, pair, HEAD_DIM), F32), vmem((2, nb, pair, HEAD_DIM), F32),
                vmem((2, nb, pair, 1), F32), vmem((2, nb, pair, HEAD_DIM), BF16),
                vmem((2, nb, pair, HEAD_DIM), BF16), vmem((2, nb, pair, pair), BF16)]
    body = functools.partial(_gdn_body, seq=seq, has_state=has_state, emit_state=emit_state, sub=sub)
    return pl.pallas_call(
        body, out_shape=out_shape, grid=(n_batch // sub,), in_specs=in_specs, out_specs=out_specs,
        scratch_shapes=scratch, compiler_params=_params(("arbitrary",)),
        name="gdn_latent" if has_state else "gdn_context",
    )(*args)


def _hy_filter_body(feat_ref, dist_ref, w1_ref, b1_ref, w2_ref, b2_ref, w3_ref, freq_ref, decay_ref,
                    cf_ref, sf_ref, hr_ref, hi_ref, nyq_ref, *, seq):
    h = jnp.sin(freq_ref[0:1, :] * (_dot(feat_ref[...], w1_ref[...], HIGHEST) + b1_ref[...]))
    h = jnp.sin(freq_ref[1:2, :] * (_dot(h, w2_ref[...], HIGHEST) + b2_ref[...]))
    h = _dot(h, w3_ref[...], HIGHEST)
    filt = h * jnp.exp(-dist_ref[...] * jnp.abs(decay_ref[...]))
    n = 2 * seq
    t = lax.broadcasted_iota(jnp.int32, (seq, 1), 0)
    alt = jnp.where((t & 1) == 0, 1.0, -1.0)
    nyq_ref[...] = jnp.sum(filt * alt, axis=0, keepdims=True) * (1.0 / n)
    fb = filt.astype(BF16)
    tk = min(seq, 512)
    for k0 in range(0, seq, tk):
        hr = _dot(cf_ref[k0:k0 + tk, :], fb)
        hs = _dot(sf_ref[k0:k0 + tk, :], fb)
        k = k0 + lax.broadcasted_iota(jnp.int32, (tk, 1), 0)
        m4 = k & 3
        c4 = jnp.where(m4 == 0, 1.0, jnp.where(m4 == 2, -1.0, 0.0))
        s4 = jnp.where(m4 == 1, 1.0, jnp.where(m4 == 3, -1.0, 0.0))
        wgt = jnp.where(k == 0, 1.0 / n, 2.0 / n)
        hr_ref[k0:k0 + tk, :] = wgt * (hr * c4 + hs * s4)
        hi_ref[k0:k0 + tk, :] = wgt * (hr * s4 - hs * c4)


def _hy_filter(seq, feats, dist, layer_params, cf, sf):
    per_layer = lambda a: pl.BlockSpec((None,) + a.shape[1:], lambda l: (l,) + (0,) * (a.ndim - 1))
    out_block = lambda rows: pl.BlockSpec((None, rows, 2 * GROUP_W), lambda l: (l, 0, 0))
    return pl.pallas_call(
        functools.partial(_hy_filter_body, seq=seq),
        out_shape=[jax.ShapeDtypeStruct((DEPTH, seq, 2 * GROUP_W), F32),
                   jax.ShapeDtypeStruct((DEPTH, seq, 2 * GROUP_W), F32),
                   jax.ShapeDtypeStruct((DEPTH, 1, 2 * GROUP_W), F32)],
        grid=(DEPTH,),
        in_specs=[_const_spec(feats.shape), _const_spec(dist.shape)] + [per_layer(a) for a in layer_params]
        + [_const_spec(cf.shape), _const_spec(sf.shape)],
        out_specs=[out_block(seq), out_block(seq), out_block(1)],
        compiler_params=_params(("arbitrary",)),
        name="hyena_filter",
    )(feats, dist, *layer_params, cf, sf)


def _hy_conv_body(x_ref, cw_ref, bias_ref, cf_ref, sf_ref, hr_ref, hi_ref, nyq_ref, o_ref, yr_s, yi_s, *, seq, sub):
    for j in range(sub):
        _hy_conv_one(_rows(x_ref, j, seq), cw_ref, bias_ref, cf_ref, sf_ref, hr_ref, hi_ref, nyq_ref,
                     _rows(o_ref, j, seq), yr_s.at[j], yi_s.at[j], seq=seq)


def _hy_conv_one(x_ref, cw_ref, bias_ref, cf_ref, sf_ref, hr_ref, hi_ref, nyq_ref, o_ref, yr_s, yi_s, *, seq):
    t = lax.broadcasted_iota(jnp.int32, (seq, 1), 0)
    alt = jnp.where((t & 1) == 0, 1.0, -1.0)
    tk = min(seq, 512)

    def long_conv(a, o):
        cols = slice(o * GROUP_W, (o + 1) * GROUP_W)
        ab = a.astype(BF16)
        for k0 in range(0, seq, tk):
            ur = _dot(cf_ref[k0:k0 + tk, :], ab)
            us = _dot(sf_ref[k0:k0 + tk, :], ab)
            hr = hr_ref[k0:k0 + tk, cols]
            hi = hi_ref[k0:k0 + tk, cols]
            yr_s[k0:k0 + tk, :] = (ur * hr + us * hi).astype(BF16)
            yi_s[k0:k0 + tk, :] = (ur * hi - us * hr).astype(BF16)
        u_nyq = jnp.sum(a * alt, axis=0, keepdims=True)
        y = _dot(cf_ref[...], yr_s[...]) - _dot(sf_ref[...], yi_s[...])
        return y + alt * (u_nyq * nyq_ref[:, cols])

    v = _conv3(x_ref[:, 0:GROUP_W], cw_ref.at[:, 0:GROUP_W])
    x1 = _conv3(x_ref[:, GROUP_W:2 * GROUP_W], cw_ref.at[:, GROUP_W:2 * GROUP_W])
    z = x1 * (long_conv(v, 0) + v * bias_ref[0:1, :])
    x2 = _conv3(x_ref[:, 2 * GROUP_W:3 * GROUP_W], cw_ref.at[:, 2 * GROUP_W:3 * GROUP_W])
    o_ref[...] = (x2 * (long_conv(z, 1) + z * bias_ref[1:2, :])).astype(BF16)


def _hy_conv(proj, n_batch, seq, l, cw, bias, cf, sf, hr, hi, nyq):
    sub = 4 if seq <= 512 else 1
    return pl.pallas_call(
        functools.partial(_hy_conv_body, seq=seq, sub=sub),
        out_shape=jax.ShapeDtypeStruct((n_batch * seq, GROUP_W), BF16),
        grid=(n_batch // sub,),
        in_specs=[pl.BlockSpec((sub * seq, W_HY), lambda b: (b, 0)),
                  _layer_spec(cw, l), _layer_spec(bias, l), _const_spec(cf.shape), _const_spec(sf.shape),
                  _layer_spec(hr, l), _layer_spec(hi, l), _layer_spec(nyq, l)],
        out_specs=pl.BlockSpec((sub * seq, GROUP_W), lambda b: (b, 0)),
        scratch_shapes=[pltpu.VMEM((sub, seq, GROUP_W), BF16)] * 2,
        compiler_params=_params(("arbitrary",)),
        name="hyena_conv",
    )(proj, cw, bias, cf, sf, hr, hi, nyq)


def _rope_tables(seq):
    rows = seq // GRID_W
    row = np.repeat(np.arange(rows), GRID_W).astype(np.float64)
    col = np.tile(np.arange(GRID_W), rows).astype(np.float64)

    def pair_tables(dim):
        n_freq = dim // 4
        inv = (ROPE_BASE ** (-np.arange(n_freq, dtype=np.float32) / n_freq)).astype(np.float64)
        ang = np.concatenate([row[:, None] * inv, col[:, None] * inv], axis=-1).astype(np.float32)
        cos = np.repeat(np.cos(ang), 2, axis=-1)
        sin = np.repeat(np.sin(ang), 2, axis=-1) * np.tile(np.array([-1.0, 1.0], np.float32), dim // 2)
        return cos, sin

    ca, sa = pair_tables(MLA_ROPE)
    one = lambda n: np.ones((seq, n), np.float32)
    zero = lambda n: np.zeros((seq, n), np.float32)
    mla = (np.concatenate([one(MLA_NOPE), ca, one(32)], 1), np.concatenate([zero(MLA_NOPE), sa, zero(32)], 1),
           np.concatenate([ca, one(96)], 1), np.concatenate([sa, zero(96)], 1))
    cb, sb = pair_tables(HEAD_DIM)
    swa = (np.tile(cb, (1, HEADS)), np.tile(sb, (1, HEADS)),
           np.tile(cb, (1, SWA_KV_HEADS)), np.tile(sb, (1, SWA_KV_HEADS)))
    as_f32 = lambda ts: tuple(jnp.asarray(t, F32) for t in ts)
    return as_f32(mla), as_f32(swa)


def _dft_tables(seq):
    k = np.arange(seq, dtype=np.int64)
    ang = ((k[:, None] * k[None, :]) % (2 * seq)).astype(np.float64) * (math.pi / seq)
    return (jnp.asarray(np.cos(ang), F32).astype(BF16), jnp.asarray(np.sin(ang), F32).astype(BF16))


def _hy_features(seq):
    t = np.arange(seq, dtype=np.float32)
    t01 = t / np.float32(max(seq - 1, 1))
    w = (np.float32(2.0 * math.pi) * t / np.float32(seq)).astype(np.float64)
    bands = np.linspace(1e-4, HY_BANDS - 1, HY_BANDS, dtype=np.float32).astype(np.float64)
    feats = np.concatenate([t01[:, None].astype(np.float64), np.cos(w[:, None] * bands), -np.sin(w[:, None] * bands)],
                           axis=-1)
    feats = np.pad(feats, ((0, 0), (0, 128 - HY_EMB)))
    dist = (np.abs(t - (seq // 2)) / np.float32(seq / 2))[:, None]
    return jnp.asarray(feats, F32), jnp.asarray(dist, F32)


def _swa_placement():
    pk = np.zeros((128, GROUP_W), np.float32)
    pv = np.zeros((128, HEADS, GROUP_W), np.float32)
    group = HEADS // SWA_KV_HEADS
    for h in range(HEADS):
        for e in range(HEAD_DIM):
            src = (h // group) * HEAD_DIM + e
            pk[src, h * HEAD_DIM + e] = 1.0
            pv[src, h, h * HEAD_DIM + e] = 1.0
    return jnp.asarray(pk, BF16), jnp.asarray(pv.reshape(128, HEADS * GROUP_W), BF16)


def _gdn_tables():
    pair = 2 * GDN_CHUNK
    i = np.arange(pair)
    same = (i[:, None] // GDN_CHUNK) == (i[None, :] // GDN_CHUNK)
    lower = (same & (i[:, None] >= i[None, :])).astype(np.float32)
    upper = (same & (i[:, None] <= i[None, :])).astype(np.float32)
    j = np.arange(GROUP_W)
    ones_bd = ((j[:, None] // HEAD_DIM) == (j[None, :] // HEAD_DIM)).astype(np.float32)
    ones2 = np.concatenate([ones_bd] * 2, axis=0)
    tri3 = np.stack([np.concatenate([m] * 3, axis=1) for m in (lower, upper)])
    trit3 = np.stack([np.concatenate([m] * 3, axis=0) for m in (lower, upper)])
    return jnp.asarray(ones2, BF16), jnp.asarray(tri3, BF16), jnp.asarray(trit3, BF16)


def _pad_last(x, n):
    return jnp.pad(x, [(0, 0)] * (x.ndim - 1) + [(0, n - x.shape[-1])])


def _layout_w_in(w):
    dep, d, _ = w.shape
    mq = _pad_last(w[..., :384].reshape(dep, d, HEADS, MLA_QK), 128).reshape(dep, d, HEADS * 128)
    mla = _pad_last(jnp.concatenate([mq, w[..., 384:544]], axis=-1), W_MLA)
    gdn = _pad_last(w[..., 1056:2096], W_GDN)
    w_all = jnp.concatenate([mla, w[..., 544:1056], gdn, w[..., 2096:2864]], axis=-1).astype(BF16)
    w_gate_t = jnp.swapaxes(w[..., 2080:2096], 1, 2).astype(BF16)
    return w_all, w_gate_t


def _layout_w_ukv(w):
    dep = w.shape[0]
    w = w.reshape(dep, MLA_KV_RANK, HEADS, 128)
    wk = _pad_last(w[..., :MLA_NOPE], 128).reshape(dep, MLA_KV_RANK, HEADS * 128)
    place = np.zeros((128, HEADS, 128), np.float32)
    for j in range(MLA_ROPE):
        place[j, :, MLA_NOPE + j] = 1.0
    place = jnp.broadcast_to(jnp.asarray(place.reshape(128, HEADS * 128)), (dep, 128, HEADS * 128))
    wk = jnp.concatenate([wk, place], axis=1)
    eye = jnp.asarray(np.eye(HEADS, dtype=np.float32))
    wv = (w[..., MLA_NOPE:][:, :, :, None, :] * eye[None, None, :, :, None]).reshape(dep, MLA_KV_RANK, HEADS * GROUP_W)
    ones_route = np.zeros((128, HEADS * GROUP_W), np.float32)
    for h in range(HEADS):
        ones_route[127, h * GROUP_W + _sum_lane(h)] = 1.0
    wv = jnp.concatenate([wv, jnp.broadcast_to(jnp.asarray(ones_route), (dep,) + ones_route.shape)], axis=1)
    return jnp.concatenate([wk, wv], axis=-1).astype(BF16)


def _layer_pass(x, n_batch, seq, l, mod, mod_spec, P, mla_ctx=None, swa_ctx=None, state=None, tables=None):
    is_ctx = tables is None
    p_mla, p_swa, p_gdn, p_hy, gate_t = _inproj(x, mod, mod_spec, l, P["g_pre_mix"], P["w_in"], P["w_gate_t"])
    mla_out = _mla(p_mla, n_batch, seq, l, P["mla_kv_norm"], P["w_ukv"], ctx=mla_ctx,
                   tables=None if is_ctx else tables[0], emit_ckv=is_ctx)
    o_a = mla_out[0]
    o_b = _swa(p_swa, n_batch, seq, l, P["swa_sink"], P["swa_pk"], P["swa_pv"], ctx_kv=swa_ctx,
               tables=None if is_ctx else tables[1])
    gdn_out = _gdn(p_gdn, gate_t, n_batch, seq, l, P["gdn_params"], P["gdn_tables"], state=state, emit_state=is_ctx)
    o_c = gdn_out[0]
    o_d = _hy_conv(p_hy, n_batch, seq, l, P["hy_conv"], P["hy_bias"], *(P["hy_ctx"] if is_ctx else P["hy_lat"]))
    y = _outmlp((o_a, o_b, o_c, o_d), x, mod, mod_spec, l, P["g_post_mix"], P["g_pre_mlp"], P["g_post_mlp"],
                P["w_out"], P["mlp_w1"], P["mlp_w2"])
    if not is_ctx:
        return y, None
    kpe0 = HEADS * 128 + MLA_KV_RANK
    new = (mla_out[1].reshape(n_batch, seq, MLA_KV_RANK),
           p_mla[:, kpe0:kpe0 + MLA_ROPE].reshape(n_batch, seq, MLA_ROPE),
           p_swa[:, 256:384].reshape(n_batch, seq, SWA_KV_HEADS, HEAD_DIM),
           p_swa[:, 384:512].reshape(n_batch, seq, SWA_KV_HEADS, HEAD_DIM),
           gdn_out[1])
    return y, new


def kernel(x_prompt, x_sample, cache_mla_ckv, cache_mla_kpe, cache_swa_k, cache_swa_v, state_gdn, c, c_ctx, w_ada, b_ada, g_pre_mix, g_post_mix, g_pre_mlp, g_post_mlp, w_in, w_out, mla_kv_norm, mla_w_ukv, swa_sink, gdn_conv, gdn_a_log, gdn_dt_bias, gdn_norm, hy_conv, hy_w1, hy_b1, hy_w2, hy_b2, hy_w3, hy_freq, hy_decay, hy_bias, mlp_w1, mlp_w2):
    n_ctx_b, seq_ctx, d = x_prompt.shape
    n_lat_b, seq_lat, _ = x_sample.shape
    past = cache_mla_ckv.shape[2]

    cond8 = jnp.concatenate([c_ctx[None, :], c, jnp.zeros((8 - 1 - n_lat_b, d), F32)], axis=0)
    mod = _modulation(cond8, w_ada, b_ada).reshape(DEPTH, 8, 6, 1, d)

    w_in_all, w_gate_t = _layout_w_in(w_in)
    gates = jnp.stack([gdn_a_log.reshape(DEPTH, 8), gdn_dt_bias.reshape(DEPTH, 8)], axis=1)
    gates_t = jnp.pad(jnp.swapaxes(gates, 1, 2), ((0, 0), (0, 8), (0, 0)))
    swa_pk, swa_pv = _swa_placement()
    P = dict(
        g_pre_mix=g_pre_mix[:, None], g_post_mix=g_post_mix[:, None],
        g_pre_mlp=g_pre_mlp[:, None], g_post_mlp=g_post_mlp[:, None],
        w_in=w_in_all, w_gate_t=w_gate_t,
        w_out=w_out.reshape(DEPTH, HEADS, GROUP_W, d).astype(BF16),
        mlp_w1=mlp_w1.astype(BF16), mlp_w2=mlp_w2.astype(BF16),
        mla_kv_norm=mla_kv_norm[:, None], w_ukv=_layout_w_ukv(mla_w_ukv),
        swa_sink=_pad_last(swa_sink, 128)[:, None], swa_pk=swa_pk, swa_pv=swa_pv,
        gdn_params=(gdn_conv, _pad_last(gates, 128), _pad_last(gates_t, 128), jnp.tile(gdn_norm, (1, HEADS))[:, None]),
        gdn_tables=_gdn_tables(),
        hy_conv=hy_conv, hy_bias=hy_bias,
    )
    hy_params = (jnp.pad(hy_w1, ((0, 0), (0, 128 - HY_EMB), (0, 0))), hy_b1[:, None], hy_w2, hy_b2[:, None], hy_w3,
                 hy_freq, hy_decay[:, None])
    for name, s in (("hy_ctx", seq_ctx), ("hy_lat", seq_lat)):
        cf, sf = _dft_tables(s)
        P[name] = (cf, sf) + tuple(_hy_filter(s, *_hy_features(s), hy_params, cf, sf))

    tables = _rope_tables(seq_lat)
    mla_ctx = jnp.concatenate([cache_mla_ckv, _pad_last(cache_mla_kpe, 127),
                               jnp.ones(cache_mla_kpe.shape[:-1] + (1,), F32)], axis=-1)
    swa_ctx = (cache_swa_k.reshape(n_lat_b, DEPTH, past, 128), cache_swa_v.reshape(n_lat_b, DEPTH, past, 128))
    lat_tiles = seq_lat // TOKEN_TILE

    xp = x_prompt.reshape(n_ctx_b * seq_ctx, d)
    xs = x_sample.reshape(n_lat_b * seq_lat, d)
    news = []
    for l in range(DEPTH):
        xp, new = _layer_pass(xp, n_ctx_b, seq_ctx, l, mod, _mod_spec(l, 0, None), P)
        news.append(new)
        xs, _ = _layer_pass(xs, n_lat_b, seq_lat, l, mod, _mod_spec(l, 1, lat_tiles), P, mla_ctx=mla_ctx,
                            swa_ctx=swa_ctx, state=state_gdn, tables=tables)

    stacked = tuple(jnp.stack([news[l][i] for l in range(DEPTH)], axis=1) for i in range(5))
    return (xp.reshape(n_ctx_b, seq_ctx, d), xs.reshape(n_lat_b, seq_lat, d)) + stacked
```

```python
import functools
import math

import jax
import jax.numpy as jnp
import numpy as np
from jax import lax
from jax.experimental import pallas as pl
from jax.experimental.pallas import tpu as pltpu

F32 = jnp.float32
BF16 = jnp.bfloat16
HIGHEST = lax.Precision.HIGHEST

D_MODEL = 1024
DEPTH = 2
GRID_W = 64
HEADS = 4
HEAD_DIM = 64
GROUP_W = 256
MLA_NOPE = 64
MLA_ROPE = 32
MLA_QK = 96
MLA_KV_RANK = 128
SWA_KV_HEADS = 2
SWA_WINDOW = 128
GDN_CHUNK = 64
HY_BANDS = 8
HY_EMB = 17
HY_FF = 64
D_FF = 4096
ROPE_BASE = 10000.0
EPS = 1e-6

W_MLA = 768
W_SWA = 512
W_GDN = 1152
W_HY = 768
W_ALL = W_MLA + W_SWA + W_GDN + W_HY

TOKEN_TILE = 1024
VMEM_LIMIT = 56 * 1024 * 1024

_NT = (((1,), (1,)), ((), ()))


def _params(sem):
    return pltpu.CompilerParams(dimension_semantics=sem, vmem_limit_bytes=VMEM_LIMIT)


def _const_spec(shape):
    nd = len(shape)
    return pl.BlockSpec(shape, lambda *_: (0,) * nd, pipeline_mode=pl.Buffered(1))


def _layer_spec(arr, l):
    shape = arr.shape[1:]
    return pl.BlockSpec((None,) + shape, lambda *_: (l,) + (0,) * len(shape), pipeline_mode=pl.Buffered(1))


def _dot(a, b, precision=None):
    if precision is None:
        a, b = a.astype(BF16), b.astype(BF16)
    return jnp.dot(a, b, preferred_element_type=F32, precision=precision)


def _dot_g(a, b, dims):
    return lax.dot_general(a.astype(BF16), b.astype(BF16), dims, preferred_element_type=F32)


_B_NN = (((2,), (1,)), ((0,), (0,)))
_B_NT = (((2,), (2,)), ((0,), (0,)))
_B_TN = (((1,), (1,)), ((0,), (0,)))


def _bmm(a, b, dims):
    return lax.dot_general(a.astype(BF16), b.astype(BF16), dims, preferred_element_type=F32)


def _split_bf16(x, parts):
    out = []
    for _ in range(parts):
        p = x.astype(BF16)
        out.append(p)
        x = x - p.astype(F32)
    return out


def _rms(x, g):
    return x * lax.rsqrt(jnp.mean(x * x, axis=-1, keepdims=True) + EPS) * g


def _silu(x):
    return x * jax.nn.sigmoid(x)


def _swap_pairs(x):
    n = x.shape[-1]
    nxt = pltpu.roll(x, n - 1, axis=1)
    prv = pltpu.roll(x, 1, axis=1)
    lane = lax.broadcasted_iota(jnp.int32, x.shape, 1)
    return jnp.where((lane & 1) == 0, nxt, prv)


def _rope(x, cos, sin_signed):
    return x * cos + _swap_pairs(x) * sin_signed


def _mod_body(c_ref, w_ref, b_ref, o_ref):
    s = _silu(c_ref[...]).astype(BF16)
    o_ref[...] = _dot(s, w_ref[...].astype(BF16)) + b_ref[...]


def _modulation(cond8, w_ada, b_ada):
    n = 6 * D_MODEL
    tn = 1536
    return pl.pallas_call(
        _mod_body,
        out_shape=jax.ShapeDtypeStruct((DEPTH, 8, n), F32),
        grid=(DEPTH, n // tn),
        in_specs=[
            pl.BlockSpec((8, D_MODEL), lambda l, j: (0, 0)),
            pl.BlockSpec((None, D_MODEL, tn), lambda l, j: (l, 0, j)),
            pl.BlockSpec((None, 1, tn), lambda l, j: (l, 0, j)),
        ],
        out_specs=pl.BlockSpec((None, 8, tn), lambda l, j: (l, 0, j)),
        compiler_params=_params(("arbitrary", "arbitrary")),
        name="modulation",
    )(cond8, w_ada, b_ada.reshape(DEPTH, 1, n))


def _mod_spec(l, row0, tiles_per_row):
    blk = (None, None, 6, 1, D_MODEL)
    if tiles_per_row is None:
        return pl.BlockSpec(blk, lambda i: (l, row0, 0, 0, 0))
    return pl.BlockSpec(blk, lambda i: (l, row0 + i // tiles_per_row, 0, 0, 0))


def _inproj_body(x_ref, mod_ref, g_ref, w_ref, wgt_ref, o_mla, o_swa, o_gdn, o_hy, o_gate_t):
    h = _rms(x_ref[...], g_ref[...]) * (1.0 + mod_ref[1]) + mod_ref[0]
    hb = h.astype(BF16)
    off = 0
    for o in (o_mla, o_swa, o_gdn, o_hy):
        n = o.shape[-1]
        o[...] = _dot(hb, w_ref[:, off:off + n])
        off += n
    o_gate_t[...] = _dot_g(wgt_ref[...], hb, _NT)


def _inproj(x, mod, mod_spec, l, g, w, w_gate_t):
    t = x.shape[0]
    tm = TOKEN_TILE
    widths = (W_MLA, W_SWA, W_GDN, W_HY)
    return pl.pallas_call(
        _inproj_body,
        out_shape=[jax.ShapeDtypeStruct((t, n), F32) for n in widths] + [jax.ShapeDtypeStruct((16, t), F32)],
        grid=(t // tm,),
        in_specs=[
            pl.BlockSpec((tm, D_MODEL), lambda i: (i, 0)),
            mod_spec,
            _layer_spec(g, l), _layer_spec(w, l), _layer_spec(w_gate_t, l),
        ],
        out_specs=[pl.BlockSpec((tm, n), lambda i: (i, 0)) for n in widths]
        + [pl.BlockSpec((16, tm), lambda i: (0, i))],
        compiler_params=_params(("arbitrary",)),
        name="inproj",
    )(x, mod, g, w, w_gate_t)


def _outmlp_body(oa, ob, oc, od, x_ref, mod_ref, g_post_mix, g_pre_mlp, g_post_mlp,
                 wo_ref, w1_ref, w2_ref, out_ref):
    o = (_dot(oa[...], wo_ref[0]) + _dot(ob[...], wo_ref[1])
         + _dot(oc[...], wo_ref[2]) + _dot(od[...], wo_ref[3]))
    x = x_ref[...] + mod_ref[2] * _rms(o, g_post_mix[...])
    hb = (_rms(x, g_pre_mlp[...]) * (1.0 + mod_ref[4]) + mod_ref[3]).astype(BF16)
    acc = jnp.zeros(x.shape, F32)
    fc = 1024
    for c in range(D_FF // fc):
        a = _dot(hb, w1_ref[:, c * fc:(c + 1) * fc])
        a = jnp.square(jnp.maximum(a, 0.0)).astype(BF16)
        acc = acc + _dot(a, w2_ref[c * fc:(c + 1) * fc, :])
    out_ref[...] = x + mod_ref[5] * _rms(acc, g_post_mlp[...])


def _outmlp(o_parts, x, mod, mod_spec, l, g_post_mix, g_pre_mlp, g_post_mlp, wo, w1, w2):
    t = x.shape[0]
    tm = TOKEN_TILE
    part_spec = pl.BlockSpec((tm, GROUP_W), lambda i: (i, 0))
    return pl.pallas_call(
        _outmlp_body,
        out_shape=jax.ShapeDtypeStruct((t, D_MODEL), F32),
        grid=(t // tm,),
        in_specs=[part_spec] * 4 + [
            pl.BlockSpec((tm, D_MODEL), lambda i: (i, 0)),
            mod_spec,
        ] + [_layer_spec(a, l) for a in (g_post_mix, g_pre_mlp, g_post_mlp, wo, w1, w2)],
        out_specs=pl.BlockSpec((tm, D_MODEL), lambda i: (i, 0)),
        compiler_params=_params(("arbitrary",)),
        name="outproj_mlp",
    )(*o_parts, x, mod, g_post_mix, g_pre_mlp, g_post_mlp, wo, w1, w2)


def _sum_lane(h):
    return ((h + 1) % HEADS) * HEAD_DIM


def _rows(ref, j, n):
    return ref.at[pl.ds(j * n, n), :]


def _mla_body(*refs, seq, n_ctx, rope, emit_ckv, row_chunk, sub):
    it = iter(refs)
    q_ref, kv_ref = next(it), next(it)
    ctx_ref = next(it) if n_ctx else None
    g_ref, w_ref = next(it), next(it)
    tabs = tuple(next(it) for _ in range(4)) if rope else None
    o_ref = next(it)
    ckv_out = next(it) if emit_ckv else None
    k_s, v_s = next(it), next(it)
    tq = q_ref.shape[0] // sub
    for j in range(sub):
        _mla_one(_rows(q_ref, j, tq), _rows(kv_ref, j, seq), ctx_ref, g_ref, w_ref, tabs, _rows(o_ref, j, tq),
                 _rows(ckv_out, j, seq) if emit_ckv else None, k_s.at[j], v_s.at[j],
                 seq=seq, n_ctx=n_ctx, rope=rope, emit_ckv=emit_ckv, row_chunk=row_chunk)


def _mla_one(q_ref, kv_ref, ctx_ref, g_ref, w_ref, tabs, o_ref, ckv_out, k_s, v_s, *,
             seq, n_ctx, rope, emit_ckv, row_chunk):
    if rope:
        cq_ref, sq_ref, ck_ref, sk_ref = tabs

    def expand(kin, r0, n):
        kv = _dot(kin.astype(BF16), w_ref[...])
        k_s[r0:r0 + n, :] = kv[:, :HEADS * 128].astype(BF16)
        for h in range(HEADS):
            c0 = HEADS * 128 + h * GROUP_W
            v_s[h, r0:r0 + n, :] = kv[:, c0:c0 + GROUP_W].astype(BF16)

    @pl.when(pl.program_id(1) == 0)
    def _prep():
        for r0 in range(0, seq, row_chunk):
            blk = kv_ref[r0:r0 + row_chunk, :]
            cn = _rms(blk[:, :MLA_KV_RANK], g_ref[...])
            pe = blk[:, MLA_KV_RANK:]
            if rope:
                pe = _rope(pe, ck_ref[r0:r0 + row_chunk, :], sk_ref[r0:r0 + row_chunk, :])
            if emit_ckv:
                ckv_out[r0:r0 + row_chunk, :] = cn
            pe = jnp.where(lax.broadcasted_iota(jnp.int32, pe.shape, 1) == 127, 1.0, pe)
            expand(jnp.concatenate([cn, pe], axis=1), r0, row_chunk)
        if n_ctx:
            expand(ctx_ref[...], seq, n_ctx)

    scale = MLA_QK ** -0.5
    lane_head = lax.broadcasted_iota(jnp.int32, o_ref.shape, 1) // HEAD_DIM
    acc = jnp.zeros(o_ref.shape, F32)
    for h in range(HEADS):
        qh = q_ref[:, h * 128:(h + 1) * 128]
        if rope:
            qh = _rope(qh, cq_ref[...], sq_ref[...])
        s = _dot_g(qh * scale, k_s[:, h * 128:(h + 1) * 128], _NT)
        p = jnp.exp(s - jnp.max(s, axis=-1, keepdims=True))
        pv = _dot(p, v_s[h])
        inv = 1.0 / pv[:, _sum_lane(h):_sum_lane(h) + 1]
        acc = acc + jnp.where(lane_head == h, pv * inv, 0.0)
    o_ref[...] = acc.astype(BF16)


def _mla(proj, n_batch, seq, l, g, w, ctx=None, tables=None, emit_ckv=False):
    tq = min(seq, 1024)
    nq = seq // tq
    sub = 4 if nq == 1 else 1
    n_ctx = 0 if ctx is None else ctx.shape[2]
    rope = tables is not None
    lk = seq + n_ctx
    in_specs = [
        pl.BlockSpec((sub * tq, HEADS * 128), lambda b, i: (b * nq + i, 0)),
        pl.BlockSpec((sub * seq, 256), lambda b, i: (b, 2)),
    ]
    args = [proj, proj]
    if n_ctx:
        in_specs.append(pl.BlockSpec((None, None, n_ctx, 256), lambda b, i: (b, l, 0, 0)))
        args.append(ctx)
    in_specs += [_layer_spec(g, l), _layer_spec(w, l)]
    args += [g, w]
    if rope:
        in_specs += [pl.BlockSpec((tq, 128), lambda b, i: (i, 0))] * 2 + [_const_spec((seq, 128))] * 2
        args += list(tables)
    out_shape = [jax.ShapeDtypeStruct((n_batch * seq, GROUP_W), BF16)]
    out_specs = [pl.BlockSpec((sub * tq, GROUP_W), lambda b, i: (b * nq + i, 0))]
    if emit_ckv:
        out_shape.append(jax.ShapeDtypeStruct((n_batch * seq, MLA_KV_RANK), F32))
        out_specs.append(pl.BlockSpec((sub * seq, MLA_KV_RANK), lambda b, i: (b, 0)))
    body = functools.partial(_mla_body, seq=seq, n_ctx=n_ctx, rope=rope, emit_ckv=emit_ckv,
                             row_chunk=min(seq, 512), sub=sub)
    return pl.pallas_call(
        body, out_shape=out_shape, grid=(n_batch // sub, nq), in_specs=in_specs, out_specs=out_specs,
        scratch_shapes=[pltpu.VMEM((sub, lk, HEADS * 128), BF16), pltpu.VMEM((sub, HEADS, lk, GROUP_W), BF16)],
        compiler_params=_params(("arbitrary", "arbitrary")),
        name="mla_latent" if rope else "mla_context",
    )(*args)


def _swa_body(*refs, seq, n_ctx, local, row_chunk, sub):
    it = iter(refs)
    q_ref, kv_ref = next(it), next(it)
    ctx_refs = (next(it), next(it)) if n_ctx else None
    consts = (next(it), next(it), next(it))
    tabs = tuple(next(it) for _ in range(4)) if local else None
    o_ref = next(it)
    k_s, v_s = next(it), next(it)
    tq = q_ref.shape[0] // sub
    for j in range(sub):
        _swa_one(_rows(q_ref, j, tq), _rows(kv_ref, j, seq), ctx_refs, consts, tabs, _rows(o_ref, j, tq),
                 k_s.at[j], v_s.at[j], seq=seq, n_ctx=n_ctx, local=local, row_chunk=row_chunk)


def _swa_one(q_ref, kv_ref, ctx_refs, consts, tabs, o_ref, k_s, v_s, *, seq, n_ctx, local, row_chunk):
    if n_ctx:
        kc_ref, vc_ref = ctx_refs
    sink_ref, pk_ref, pv_ref = consts
    if local:
        cq_ref, sq_ref, ck_ref, sk_ref = tabs
    tq = q_ref.shape[0]
    blk_id = pl.program_id(1)

    def expand(k, v, r0, n):
        k_s[r0:r0 + n, :] = _dot(k.astype(BF16), pk_ref[...]).astype(BF16)
        vb = v.astype(BF16)
        lane = lax.broadcasted_iota(jnp.int32, (n, GROUP_W), 1)
        for h in range(HEADS):
            ve = _dot(vb, pv_ref[:, h * GROUP_W:(h + 1) * GROUP_W])
            v_s[h, r0:r0 + n, :] = jnp.where(lane == _sum_lane(h), 1.0, ve).astype(BF16)

    @pl.when(blk_id == 0)
    def _prep():
        for r0 in range(0, seq, row_chunk):
            k = kv_ref[r0:r0 + row_chunk, 0:128]
            v = kv_ref[r0:r0 + row_chunk, 128:256]
            if local:
                k = _rope(k, ck_ref[r0:r0 + row_chunk, :], sk_ref[r0:r0 + row_chunk, :])
            expand(k, v, r0, row_chunk)
        if n_ctx:
            expand(kc_ref[...], vc_ref[...], seq, n_ctx)

    scale = HEAD_DIM ** -0.5
    q = q_ref[...]
    if local:
        q = _rope(q, cq_ref[...], sq_ref[...])
        win = tq + 2 * SWA_WINDOW
        start = jnp.clip(blk_id * tq - SWA_WINDOW, 0, seq - win)
        start = pl.multiple_of(start, SWA_WINDOW)
        qpos = blk_id * tq + lax.broadcasted_iota(jnp.int32, (tq, win), 0)
        kpos = start + lax.broadcasted_iota(jnp.int32, (tq, win), 1)
        valid = jnp.abs(qpos - kpos) <= SWA_WINDOW
    q = q * scale
    lane_head = lax.broadcasted_iota(jnp.int32, q.shape, 1) // HEAD_DIM
    acc = jnp.zeros(o_ref.shape, F32)
    for h in range(HEADS):
        qm = jnp.where(lane_head == h, q, 0.0).astype(BF16)
        sink = sink_ref[:, h:h + 1]
        if local:
            s_loc = jnp.where(valid, _dot_g(qm, k_s[pl.ds(start, win), :], _NT), -jnp.inf)
            s_ctx = _dot_g(qm, k_s[seq:seq + n_ctx, :], _NT)
            m = jnp.maximum(jnp.maximum(jnp.max(s_loc, axis=-1, keepdims=True),
                                        jnp.max(s_ctx, axis=-1, keepdims=True)), sink)
            o = (_dot(jnp.exp(s_loc - m), v_s[h, pl.ds(start, win), :])
                 + _dot(jnp.exp(s_ctx - m), v_s[h, seq:seq + n_ctx, :]))
        else:
            s = _dot_g(qm, k_s[...], _NT)
            m = jnp.maximum(jnp.max(s, axis=-1, keepdims=True), sink)
            o = _dot(jnp.exp(s - m), v_s[h])
        den = o[:, _sum_lane(h):_sum_lane(h) + 1] + jnp.exp(sink - m)
        acc = acc + jnp.where(lane_head == h, o * (1.0 / den), 0.0)
    o_ref[...] = acc.astype(BF16)


def _swa(proj, n_batch, seq, l, sink, pk, pv, ctx_kv=None, tables=None):
    local = tables is not None
    tq = 2 * SWA_WINDOW if local else seq
    nq = seq // tq
    sub = 4 if nq == 1 else 1
    n_ctx = 0 if ctx_kv is None else ctx_kv[0].shape[2]
    lk = seq + n_ctx
    in_specs = [
        pl.BlockSpec((sub * tq, GROUP_W), lambda b, i: (b * nq + i, 0)),
        pl.BlockSpec((sub * seq, 256), lambda b, i: (b, 1)),
    ]
    args = [proj, proj]
    if n_ctx:
        in_specs += [pl.BlockSpec((None, None, n_ctx, 128), lambda b, i: (b, l, 0, 0))] * 2
        args += list(ctx_kv)
    in_specs += [_layer_spec(sink, l), _const_spec(pk.shape), _const_spec(pv.shape)]
    args += [sink, pk, pv]
    if local:
        in_specs += [pl.BlockSpec((tq, GROUP_W), lambda b, i: (i, 0))] * 2 + [_const_spec((seq, 128))] * 2
        args += list(tables)
    body = functools.partial(_swa_body, seq=seq, n_ctx=n_ctx, local=local, row_chunk=min(seq, 512), sub=sub)
    return pl.pallas_call(
        body,
        out_shape=jax.ShapeDtypeStruct((n_batch * seq, GROUP_W), BF16),
        grid=(n_batch // sub, nq), in_specs=in_specs,
        out_specs=pl.BlockSpec((sub * tq, GROUP_W), lambda b, i: (b * nq + i, 0)),
        scratch_shapes=[pltpu.VMEM((sub, lk, GROUP_W), BF16), pltpu.VMEM((sub, HEADS, lk, GROUP_W), BF16)],
        compiler_params=_params(("arbitrary", "arbitrary")),
        name="swa_latent" if local else "swa_context",
    )(*args)


def _conv3(x, w_ref):
    n = x.shape[0]
    row = lax.broadcasted_iota(jnp.int32, x.shape, 0)
    prev = jnp.where(row == 0, 0.0, pltpu.roll(x, 1, axis=0))
    nxt = jnp.where(row == n - 1, 0.0, pltpu.roll(x, n - 1, axis=0))
    return prev * w_ref[0:1, :] + x * w_ref[1:2, :] + nxt * w_ref[2:3, :]


def _gdn_body(*refs, seq, has_state, emit_state, sub):
    it = iter(refs)
    x_ref, gt_ref = next(it), next(it)
    s0_ref = next(it) if has_state else None
    consts = tuple(next(it) for _ in range(7))
    o_ref = next(it)
    st_ref = next(it) if emit_state else None
    scratch = tuple(it)
    for j in range(sub):
        _gdn_one(_rows(x_ref, j, seq), gt_ref.at[:, pl.ds(j * seq, seq)], s0_ref, consts, _rows(o_ref, j, seq),
                 st_ref.at[j] if emit_state else None, tuple(s.at[j] for s in scratch),
                 seq=seq, has_state=has_state, emit_state=emit_state)


def _gdn_one(x_ref, gt_ref, s0_ref, consts, o_ref, st_ref, scratch, *, seq, has_state, emit_state):
    cw_ref, gpar_ref, gpart_ref, gnorm_ref, ones_ref, tri_ref, trit_ref = consts
    q_s, k_s, v_s, of_s, ob_s, dec_s, dect_s, gate_s, s_s = scratch[:9]
    st_uv, st_k, st_d, st_w, st_qd, st_attn, st_a, st_t, st_rv, st_rw = scratch[9:]
    pair = 2 * GDN_CHUNK
    n_pair = seq // pair

    def conv_act(g):
        cols = slice(g * GROUP_W, (g + 1) * GROUP_W)
        return _silu(_conv3(x_ref[:, cols], cw_ref.at[:, cols]))

    def head_sums(x):
        return _dot(jnp.concatenate(_split_bf16(x, 2), axis=1), ones_ref[...])

    def head_l2(x):
        return x * lax.rsqrt(head_sums(x * x) + EPS)

    def to_heads(dst, x):
        for h in range(HEADS):
            dst[h] = x[:, h * HEAD_DIM:(h + 1) * HEAD_DIM]

    to_heads(q_s, head_l2(conv_act(0)) * (HEAD_DIM ** -0.5))
    to_heads(k_s, head_l2(conv_act(1)))
    to_heads(v_s, conv_act(2))
    gab = x_ref[:, 4 * GROUP_W:4 * GROUP_W + 128]
    glog = -jnp.exp(gpar_ref[0:1, :]) * jax.nn.softplus(gab + gpar_ref[1:2, :])
    gate_s[...] = jax.nn.sigmoid(gab)
    lane = lax.broadcasted_iota(jnp.int32, (pair, 128), 1)
    glogt = -jnp.exp(gpart_ref[:, 0:1]) * jax.nn.softplus(gt_ref[...] + gpart_ref[:, 1:2])
    rowi = lax.broadcasted_iota(jnp.int32, (16, pair), 0)
    g3 = _split_bf16(glog, 3)
    gt3 = _split_bf16(glogt, 3)
    for p in range(n_pair):
        r = slice(p * pair, (p + 1) * pair)
        gcol = jnp.concatenate([g[r, :] for g in g3], axis=0)
        grow = jnp.concatenate([g[:, r] for g in gt3], axis=1)
        dec_s[r, :] = jnp.where(lane < HEADS, _dot(tri_ref[0], gcol), _dot(tri_ref[1], gcol))
        dect_s[:, r] = jnp.where(rowi < HEADS, _dot(grow, trit_ref[1]), _dot(grow, trit_ref[0]))

    if has_state:
        s_s[...] = s0_ref[...]
    else:
        s_s[...] = jnp.zeros(s_s.shape, F32)

    ri = lax.broadcasted_iota(jnp.int32, (pair, pair), 0)
    ci = lax.broadcasted_iota(jnp.int32, (pair, pair), 1)
    same = (ri // GDN_CHUNK) == (ci // GDN_CHUNK)
    nb = 2 * HEADS
    incl = (same & (ri >= ci), same & (ri <= ci))
    strict = (same & (ri > ci), same & (ri < ci))
    eye = (ri == ci).astype(F32)
    merge = [((ri // (2 * s)) == (ci // (2 * s))) & ((ri // s) != (ci // s))
             for s in (1, 2, 4, 8, 16, 32)]
    c = GDN_CHUNK

    def first_second(x):
        return (jnp.concatenate([x[:HEADS, :c], x[HEADS:, c:]], axis=0),
                jnp.concatenate([x[:HEADS, c:], x[HEADS:, :c]], axis=0))

    def row_order(first, second):
        return jnp.concatenate([jnp.concatenate([first[:HEADS], second[:HEADS]], axis=1),
                                jnp.concatenate([second[HEADS:], first[HEADS:]], axis=1)], axis=0)

    def rows_of(i):
        fwd, bwd = i * pair, (n_pair - 1 - i) * pair
        if not isinstance(i, int):
            fwd, bwd = pl.multiple_of(fwd, pair), pl.multiple_of(bwd, pair)
        return pl.ds(fwd, pair), pl.ds(bwd, pair)


    def prep_stage(i):
        rows = rows_of(i)
        slot, slot3 = i % 2, i % 3
        for d in range(2):
            r = rows[d]
            gd_, dd_, td_ = gate_s[r, :], dec_s[r, :], dect_s[:, r]
            for h in range(HEADS):
                p, col = d * HEADS + h, d * HEADS + h
                q, k, v = q_s[h, r, :], k_s[h, r, :], v_s[h, r, :]
                beta = gd_[:, 8 + col:9 + col]
                dcol = dd_[:, col:col + 1]
                gam = jnp.where(incl[d], jnp.exp(jnp.where(incl[d], dcol - td_[col:col + 1, :], 0.0)), 0.0)
                kb = k * beta
                kk_qk = _dot_g(jnp.concatenate([kb, q], axis=0), k, _NT)
                a = jnp.where(strict[d], kk_qk[:pair] * gam, 0.0)
                edec = jnp.exp(dcol)
                st_attn[slot3, p] = (kk_qk[pair:] * gam).astype(BF16)
                st_qd[slot3, p] = (q * edec).astype(BF16)
                st_k[slot3, p] = k
                st_d[slot3, p] = dcol
                st_a[slot, p] = a.astype(BF16)
                st_t[slot, p] = (eye - jnp.where(merge[0], a, 0.0)).astype(BF16)
                st_rv[slot, p] = (v * beta).astype(BF16)
                st_rw[slot, p] = (kb * edec).astype(BF16)

    def invert_stage(i, filler=iter(())):
        slot = i % 2
        ab = [st_a[slot, p] for p in range(nb)]
        t = [st_t[slot, p] for p in range(nb)]
        for m in merge[1:]:
            next(filler, None)
            x = [_dot(t[p], jnp.where(m, ab[p], 0.0)) for p in range(nb)]
            x = [_dot(x[p], t[p]) for p in range(nb)]
            t = [t[p] - x[p].astype(BF16) for p in range(nb)]
        for p in range(nb):
            st_uv[slot, p] = _dot(t[p], st_rv[slot, p])
            st_w[slot, p] = _dot(t[p], st_rw[slot, p]).astype(BF16)
        for _ in filler:
            pass

    def state_steps(i):
        rf, rb = rows_of(i)
        slot, slot3 = i % 2, i % 3
        uv1, uv2 = first_second(st_uv[slot])
        w1, w2 = first_second(st_w[slot])
        qd1, qd2 = first_second(st_qd[slot3])
        k1, k2 = first_second(st_k[slot3])
        d1, d2 = first_second(st_d[slot3])
        dl1 = jnp.concatenate([d1[:HEADS, c - 1:c], d1[HEADS:, 0:1]], axis=0)
        dl2 = jnp.concatenate([d2[:HEADS, c - 1:c], d2[HEADS:, 0:1]], axis=0)
        s = s_s[...].reshape(nb, HEAD_DIM, HEAD_DIM)
        u1 = uv1 - _bmm(w1, s, _B_NN)
        o1 = _bmm(qd1, s, _B_NN)
        yield
        s = s * jnp.exp(dl1) + _bmm(k1 * jnp.exp(dl1 - d1), u1, _B_TN)
        yield
        u2 = uv2 - _bmm(w2, s, _B_NN)
        o2 = _bmm(qd2, s, _B_NN)
        yield
        s = s * jnp.exp(dl2) + _bmm(k2 * jnp.exp(dl2 - d2), u2, _B_TN)
        s_s[...] = s.reshape(2, HEADS, HEAD_DIM, HEAD_DIM)
        yield
        o = row_order(o1, o2) + _bmm(st_attn[slot3], row_order(u1, u2), _B_NN)
        for j in range(HEADS // 2):
            of_s[j, rf, :] = jnp.concatenate([o[2 * j], o[2 * j + 1]], axis=1)
            ob_s[j, rb, :] = jnp.concatenate([o[HEADS + 2 * j], o[HEADS + 2 * j + 1]], axis=1)

    def loop_body(i, carry):
        invert_stage(i + 1, state_steps(i))
        prep_stage(i + 2)
        return carry

    prep_stage(0)
    prep_stage(1)
    invert_stage(0)
    if n_pair == 2:
        pass
    else:
        lax.fori_loop(0, n_pair - 2, loop_body, 0)
    invert_stage(n_pair - 1, state_steps(n_pair - 2))
    for _ in state_steps(n_pair - 1):
        pass

    o = jnp.concatenate([of_s[j] + ob_s[j] for j in range(HEADS // 2)], axis=1)
    ms = head_sums(o * o) * (1.0 / HEAD_DIM)
    o = o * lax.rsqrt(ms + EPS) * gnorm_ref[...]
    o_ref[...] = (o * _silu(x_ref[:, 3 * GROUP_W:4 * GROUP_W])).astype(BF16)
    if emit_state:
        st_ref[...] = s_s[...]


def _gdn(proj, gate_t, n_batch, seq, l, layer_params, tables, state=None, emit_state=False):
    has_state = state is not None
    sub = 1 if has_state else 2
    in_specs = [pl.BlockSpec((sub * seq, W_GDN), lambda b: (b, 0), pipeline_mode=pl.Buffered(1)),
                pl.BlockSpec((16, sub * seq), lambda b: (0, b))]
    args = [proj, gate_t]
    state_block = (sub, 2, HEADS, HEAD_DIM, HEAD_DIM)
    if has_state:
        in_specs.append(pl.BlockSpec((None, None) + state_block[1:], lambda b: (b, l, 0, 0, 0, 0)))
        args.append(state)
    for p in layer_params:
        in_specs.append(_layer_spec(p, l))
        args.append(p)
    for c in tables:
        in_specs.append(_const_spec(c.shape))
        args.append(c)
    out_shape = [jax.ShapeDtypeStruct((n_batch * seq, GROUP_W), BF16)]
    out_specs = [pl.BlockSpec((sub * seq, GROUP_W), lambda b: (b, 0))]
    if emit_state:
        out_shape.append(jax.ShapeDtypeStruct((n_batch, 2, HEADS, HEAD_DIM, HEAD_DIM), F32))
        out_specs.append(pl.BlockSpec(state_block, lambda b: (b, 0, 0, 0, 0)))
    vmem = lambda shape, dtype: pltpu.VMEM((sub,) + shape, dtype)
    scratch = [vmem((HEADS, seq, HEAD_DIM), F32)] * 3 + [vmem((HEADS // 2, seq, 128), F32)] * 2 + [
        vmem((seq, 128), F32), vmem((16, seq), F32), vmem((seq, 128), F32),
        vmem((2, HEADS, HEAD_DIM, HEAD_DIM), F32)]
    nb, pair = 2 * HEADS, 2 * GDN_CHUNK
    scratch += [vmem((2, nb, pair, HEAD_DIM), F32), vmem((3, nb, pair, HEAD_DIM), F32),
                vmem((3, nb, pair, 1), F32), vmem((2, nb, pair, HEAD_DIM), BF16),
                vmem((3, nb, pair, HEAD_DIM), BF16), vmem((3, nb, pair, pair), BF16),
                vmem((2, nb, pair, pair), BF16), vmem((2, nb, pair, pair), BF16),
                vmem((2, nb, pair, HEAD_DIM), BF16), vmem((2, nb, pair, HEAD_DIM), BF16)]
    body = functools.partial(_gdn_body, seq=seq, has_state=has_state, emit_state=emit_state, sub=sub)
    return pl.pallas_call(
        body, out_shape=out_shape, grid=(n_batch // sub,), in_specs=in_specs, out_specs=out_specs,
        scratch_shapes=scratch, compiler_params=_params(("arbitrary",)),
        name="gdn_latent" if has_state else "gdn_context",
    )(*args)


def _hy_filter_body(feat_ref, dist_ref, w1_ref, b1_ref, w2_ref, b2_ref, w3_ref, freq_ref, decay_ref,
                    cf_ref, sf_ref, rot_ref, hr_ref, hi_ref, nyq_ref, *, seq):
    h = jnp.sin(freq_ref[0:1, :] * (_dot(feat_ref[...], w1_ref[...], HIGHEST) + b1_ref[...]))
    h = jnp.sin(freq_ref[1:2, :] * (_dot(h, w2_ref[...], HIGHEST) + b2_ref[...]))
    h = _dot(h, w3_ref[...], HIGHEST)
    filt = h * jnp.exp(-dist_ref[...] * jnp.abs(decay_ref[...]))
    kf = cf_ref.shape[0]
    t = lax.broadcasted_iota(jnp.int32, (seq, 1), 0)
    alt = jnp.where((t & 1) == 0, 1.0, -1.0)
    nyq_ref[...] = jnp.sum(filt * alt, axis=0, keepdims=True) * (1.0 / (2 * kf))
    fb = filt.astype(BF16)
    tk = min(kf, 512)
    for k0 in range(0, kf, tk):
        hr = _dot(cf_ref[k0:k0 + tk, :], fb)
        hs = _dot(sf_ref[k0:k0 + tk, :], fb)
        wc, ws = rot_ref[k0:k0 + tk, 0:1], rot_ref[k0:k0 + tk, 1:2]
        hr_ref[k0:k0 + tk, :] = hr * wc + hs * ws
        hi_ref[k0:k0 + tk, :] = hr * ws - hs * wc


def _hy_filter(seq, feats, dist, layer_params, dft):
    cf, sf, _, _, rot = dft
    kf = cf.shape[0]
    per_layer = lambda a: pl.BlockSpec((None,) + a.shape[1:], lambda l: (l,) + (0,) * (a.ndim - 1))
    out_block = lambda rows: pl.BlockSpec((None, rows, 2 * GROUP_W), lambda l: (l, 0, 0))
    return pl.pallas_call(
        functools.partial(_hy_filter_body, seq=seq),
        out_shape=[jax.ShapeDtypeStruct((DEPTH, kf, 2 * GROUP_W), F32),
                   jax.ShapeDtypeStruct((DEPTH, kf, 2 * GROUP_W), F32),
                   jax.ShapeDtypeStruct((DEPTH, 1, 2 * GROUP_W), F32)],
        grid=(DEPTH,),
        in_specs=[_const_spec(feats.shape), _const_spec(dist.shape)] + [per_layer(a) for a in layer_params]
        + [_const_spec(cf.shape), _const_spec(sf.shape), _const_spec(rot.shape)],
        out_specs=[out_block(kf), out_block(kf), out_block(1)],
        compiler_params=_params(("arbitrary",)),
        name="hyena_filter",
    )(feats, dist, *layer_params, cf, sf, rot)


def _hy_conv_body(x_ref, cw_ref, bias_ref, cf_ref, sf_ref, cft_ref, sft_ref, hr_ref, hi_ref, nyq_ref, o_ref,
                  yr_s, yi_s, *, seq, sub):
    for j in range(sub):
        _hy_conv_one(_rows(x_ref, j, seq), cw_ref, bias_ref, cf_ref, sf_ref, cft_ref, sft_ref, hr_ref, hi_ref,
                     nyq_ref, _rows(o_ref, j, seq), yr_s.at[j], yi_s.at[j], seq=seq)


def _hy_conv_one(x_ref, cw_ref, bias_ref, cf_ref, sf_ref, cft_ref, sft_ref, hr_ref, hi_ref, nyq_ref, o_ref,
                 yr_s, yi_s, *, seq):
    t = lax.broadcasted_iota(jnp.int32, (seq, 1), 0)
    alt = jnp.where((t & 1) == 0, 1.0, -1.0)
    kf = cf_ref.shape[0]
    tk = min(kf, 512)

    def long_conv(a, o):
        cols = slice(o * GROUP_W, (o + 1) * GROUP_W)
        ab = a.astype(BF16)
        for k0 in range(0, kf, tk):
            ur = _dot(cf_ref[k0:k0 + tk, :], ab)
            us = _dot(sf_ref[k0:k0 + tk, :], ab)
            hr = hr_ref[k0:k0 + tk, cols]
            hi = hi_ref[k0:k0 + tk, cols]
            yr_s[k0:k0 + tk, :] = (ur * hr + us * hi).astype(BF16)
            yi_s[k0:k0 + tk, :] = (ur * hi - us * hr).astype(BF16)
        u_nyq = jnp.sum(a * alt, axis=0, keepdims=True)
        y = _dot(cft_ref[...], yr_s[...]) - _dot(sft_ref[...], yi_s[...])
        return y + alt * (u_nyq * nyq_ref[:, cols])

    v = _conv3(x_ref[:, 0:GROUP_W], cw_ref.at[:, 0:GROUP_W])
    x1 = _conv3(x_ref[:, GROUP_W:2 * GROUP_W], cw_ref.at[:, GROUP_W:2 * GROUP_W])
    z = x1 * (long_conv(v, 0) + v * bias_ref[0:1, :])
    x2 = _conv3(x_ref[:, 2 * GROUP_W:3 * GROUP_W], cw_ref.at[:, 2 * GROUP_W:3 * GROUP_W])
    o_ref[...] = (x2 * (long_conv(z, 1) + z * bias_ref[1:2, :])).astype(BF16)


def _hy_conv(proj, n_batch, seq, l, cw, bias, dft, hr, hi, nyq):
    cf, sf, cft, sft, _ = dft
    kf = cf.shape[0]
    sub = 4 if seq <= 512 else 1
    return pl.pallas_call(
        functools.partial(_hy_conv_body, seq=seq, sub=sub),
        out_shape=jax.ShapeDtypeStruct((n_batch * seq, GROUP_W), BF16),
        grid=(n_batch // sub,),
        in_specs=[pl.BlockSpec((sub * seq, W_HY), lambda b: (b, 0), pipeline_mode=pl.Buffered(1)),
                  _layer_spec(cw, l), _layer_spec(bias, l)] + [_const_spec(a.shape) for a in (cf, sf, cft, sft)]
        + [_layer_spec(hr, l), _layer_spec(hi, l), _layer_spec(nyq, l)],
        out_specs=pl.BlockSpec((sub * seq, GROUP_W), lambda b: (b, 0)),
        scratch_shapes=[pltpu.VMEM((sub, kf, GROUP_W), BF16)] * 2,
        compiler_params=_params(("arbitrary",)),
        name="hyena_conv",
    )(proj, cw, bias, cf, sf, cft, sft, hr, hi, nyq)


def _rope_tables(seq):
    rows = seq // GRID_W
    row = np.repeat(np.arange(rows), GRID_W).astype(np.float64)
    col = np.tile(np.arange(GRID_W), rows).astype(np.float64)

    def pair_tables(dim):
        n_freq = dim // 4
        inv = (ROPE_BASE ** (-np.arange(n_freq, dtype=np.float32) / n_freq)).astype(np.float64)
        ang = np.concatenate([row[:, None] * inv, col[:, None] * inv], axis=-1).astype(np.float32)
        cos = np.repeat(np.cos(ang), 2, axis=-1)
        sin = np.repeat(np.sin(ang), 2, axis=-1) * np.tile(np.array([-1.0, 1.0], np.float32), dim // 2)
        return cos, sin

    ca, sa = pair_tables(MLA_ROPE)
    one = lambda n: np.ones((seq, n), np.float32)
    zero = lambda n: np.zeros((seq, n), np.float32)
    mla = (np.concatenate([one(MLA_NOPE), ca, one(32)], 1), np.concatenate([zero(MLA_NOPE), sa, zero(32)], 1),
           np.concatenate([ca, one(96)], 1), np.concatenate([sa, zero(96)], 1))
    cb, sb = pair_tables(HEAD_DIM)
    swa = (np.tile(cb, (1, HEADS)), np.tile(sb, (1, HEADS)),
           np.tile(cb, (1, SWA_KV_HEADS)), np.tile(sb, (1, SWA_KV_HEADS)))
    as_f32 = lambda ts: tuple(jnp.asarray(t, F32) for t in ts)
    return as_f32(mla), as_f32(swa)


def _dft_tables(seq):
    n = 3 * seq // 2
    kf = n // 2
    k = np.arange(kf, dtype=np.int64)
    s = np.arange(seq, dtype=np.int64)
    ang = ((k[:, None] * s[None, :]) % n).astype(np.float64) * (2.0 * math.pi / n)
    cos, sin = np.cos(ang), np.sin(ang)
    theta = ((k * (seq // 2)) % n).astype(np.float64) * (2.0 * math.pi / n)
    wgt = np.where(k == 0, 1.0 / n, 2.0 / n)
    rot = np.zeros((kf, 128), np.float64)
    rot[:, 0], rot[:, 1] = wgt * np.cos(theta), wgt * np.sin(theta)
    bf = lambda a: jnp.asarray(a, F32).astype(BF16)
    return bf(cos), bf(sin), bf(cos.T), bf(sin.T), jnp.asarray(rot, F32)


def _hy_features(seq):
    t = np.arange(seq, dtype=np.float32)
    t01 = t / np.float32(max(seq - 1, 1))
    w = (np.float32(2.0 * math.pi) * t / np.float32(seq)).astype(np.float64)
    bands = np.linspace(1e-4, HY_BANDS - 1, HY_BANDS, dtype=np.float32).astype(np.float64)
    feats = np.concatenate([t01[:, None].astype(np.float64), np.cos(w[:, None] * bands), -np.sin(w[:, None] * bands)],
                           axis=-1)
    feats = np.pad(feats, ((0, 0), (0, 128 - HY_EMB)))
    dist = (np.abs(t - (seq // 2)) / np.float32(seq / 2))[:, None]
    return jnp.asarray(feats, F32), jnp.asarray(dist, F32)


def _swa_placement():
    pk = np.zeros((128, GROUP_W), np.float32)
    pv = np.zeros((128, HEADS, GROUP_W), np.float32)
    group = HEADS // SWA_KV_HEADS
    for h in range(HEADS):
        for e in range(HEAD_DIM):
            src = (h // group) * HEAD_DIM + e
            pk[src, h * HEAD_DIM + e] = 1.0
            pv[src, h, h * HEAD_DIM + e] = 1.0
    return jnp.asarray(pk, BF16), jnp.asarray(pv.reshape(128, HEADS * GROUP_W), BF16)


def _gdn_tables():
    pair = 2 * GDN_CHUNK
    i = np.arange(pair)
    same = (i[:, None] // GDN_CHUNK) == (i[None, :] // GDN_CHUNK)
    lower = (same & (i[:, None] >= i[None, :])).astype(np.float32)
    upper = (same & (i[:, None] <= i[None, :])).astype(np.float32)
    j = np.arange(GROUP_W)
    ones_bd = ((j[:, None] // HEAD_DIM) == (j[None, :] // HEAD_DIM)).astype(np.float32)
    ones2 = np.concatenate([ones_bd] * 2, axis=0)
    tri3 = np.stack([np.concatenate([m] * 3, axis=1) for m in (lower, upper)])
    trit3 = np.stack([np.concatenate([m] * 3, axis=0) for m in (lower, upper)])
    return jnp.asarray(ones2, BF16), jnp.asarray(tri3, BF16), jnp.asarray(trit3, BF16)


def _pad_last(x, n):
    return jnp.pad(x, [(0, 0)] * (x.ndim - 1) + [(0, n - x.shape[-1])])


def _layout_w_in(w):
    dep, d, _ = w.shape
    mq = _pad_last(w[..., :384].reshape(dep, d, HEADS, MLA_QK), 128).reshape(dep, d, HEADS * 128)
    mla = _pad_last(jnp.concatenate([mq, w[..., 384:544]], axis=-1), W_MLA)
    gdn = _pad_last(w[..., 1056:2096], W_GDN)
    w_all = jnp.concatenate([mla, w[..., 544:1056], gdn, w[..., 2096:2864]], axis=-1).astype(BF16)
    w_gate_t = jnp.swapaxes(w[..., 2080:2096], 1, 2).astype(BF16)
    return w_all, w_gate_t


def _layout_w_ukv(w):
    dep = w.shape[0]
    w = w.reshape(dep, MLA_KV_RANK, HEADS, 128)
    wk = _pad_last(w[..., :MLA_NOPE], 128).reshape(dep, MLA_KV_RANK, HEADS * 128)
    place = np.zeros((128, HEADS, 128), np.float32)
    for j in range(MLA_ROPE):
        place[j, :, MLA_NOPE + j] = 1.0
    place = jnp.broadcast_to(jnp.asarray(place.reshape(128, HEADS * 128)), (dep, 128, HEADS * 128))
    wk = jnp.concatenate([wk, place], axis=1)
    eye = jnp.asarray(np.eye(HEADS, dtype=np.float32))
    wv = (w[..., MLA_NOPE:][:, :, :, None, :] * eye[None, None, :, :, None]).reshape(dep, MLA_KV_RANK, HEADS * GROUP_W)
    ones_route = np.zeros((128, HEADS * GROUP_W), np.float32)
    for h in range(HEADS):
        ones_route[127, h * GROUP_W + _sum_lane(h)] = 1.0
    wv = jnp.concatenate([wv, jnp.broadcast_to(jnp.asarray(ones_route), (dep,) + ones_route.shape)], axis=1)
    return jnp.concatenate([wk, wv], axis=-1).astype(BF16)


def _layer_pass(x, n_batch, seq, l, mod, mod_spec, P, mla_ctx=None, swa_ctx=None, state=None, tables=None):
    is_ctx = tables is None
    p_mla, p_swa, p_gdn, p_hy, gate_t = _inproj(x, mod, mod_spec, l, P["g_pre_mix"], P["w_in"], P["w_gate_t"])
    mla_out = _mla(p_mla, n_batch, seq, l, P["mla_kv_norm"], P["w_ukv"], ctx=mla_ctx,
                   tables=None if is_ctx else tables[0], emit_ckv=is_ctx)
    o_a = mla_out[0]
    o_b = _swa(p_swa, n_batch, seq, l, P["swa_sink"], P["swa_pk"], P["swa_pv"], ctx_kv=swa_ctx,
               tables=None if is_ctx else tables[1])
    gdn_out = _gdn(p_gdn, gate_t, n_batch, seq, l, P["gdn_params"], P["gdn_tables"], state=state, emit_state=is_ctx)
    o_c = gdn_out[0]
    o_d = _hy_conv(p_hy, n_batch, seq, l, P["hy_conv"], P["hy_bias"], *(P["hy_ctx"] if is_ctx else P["hy_lat"]))
    y = _outmlp((o_a, o_b, o_c, o_d), x, mod, mod_spec, l, P["g_post_mix"], P["g_pre_mlp"], P["g_post_mlp"],
                P["w_out"], P["mlp_w1"], P["mlp_w2"])
    if not is_ctx:
        return y, None
    kpe0 = HEADS * 128 + MLA_KV_RANK
    new = (mla_out[1].reshape(n_batch, seq, MLA_KV_RANK),
           p_mla[:, kpe0:kpe0 + MLA_ROPE].reshape(n_batch, seq, MLA_ROPE),
           p_swa[:, 256:384].reshape(n_batch, seq, SWA_KV_HEADS, HEAD_DIM),
           p_swa[:, 384:512].reshape(n_batch, seq, SWA_KV_HEADS, HEAD_DIM),
           gdn_out[1])
    return y, new


def kernel(x_prompt, x_sample, cache_mla_ckv, cache_mla_kpe, cache_swa_k, cache_swa_v, state_gdn, c, c_ctx, w_ada, b_ada, g_pre_mix, g_post_mix, g_pre_mlp, g_post_mlp, w_in, w_out, mla_kv_norm, mla_w_ukv, swa_sink, gdn_conv, gdn_a_log, gdn_dt_bias, gdn_norm, hy_conv, hy_w1, hy_b1, hy_w2, hy_b2, hy_w3, hy_freq, hy_decay, hy_bias, mlp_w1, mlp_w2):
    n_ctx_b, seq_ctx, d = x_prompt.shape
    n_lat_b, seq_lat, _ = x_sample.shape
    past = cache_mla_ckv.shape[2]

    cond8 = jnp.concatenate([c_ctx[None, :], c, jnp.zeros((8 - 1 - n_lat_b, d), F32)], axis=0)
    mod = _modulation(cond8, w_ada, b_ada).reshape(DEPTH, 8, 6, 1, d)

    w_in_all, w_gate_t = _layout_w_in(w_in)
    gates = jnp.stack([gdn_a_log.reshape(DEPTH, 8), gdn_dt_bias.reshape(DEPTH, 8)], axis=1)
    gates_t = jnp.pad(jnp.swapaxes(gates, 1, 2), ((0, 0), (0, 8), (0, 0)))
    swa_pk, swa_pv = _swa_placement()
    P = dict(
        g_pre_mix=g_pre_mix[:, None], g_post_mix=g_post_mix[:, None],
        g_pre_mlp=g_pre_mlp[:, None], g_post_mlp=g_post_mlp[:, None],
        w_in=w_in_all, w_gate_t=w_gate_t,
        w_out=w_out.reshape(DEPTH, HEADS, GROUP_W, d).astype(BF16),
        mlp_w1=mlp_w1.astype(BF16), mlp_w2=mlp_w2.astype(BF16),
        mla_kv_norm=mla_kv_norm[:, None], w_ukv=_layout_w_ukv(mla_w_ukv),
        swa_sink=_pad_last(swa_sink, 128)[:, None], swa_pk=swa_pk, swa_pv=swa_pv,
        gdn_params=(gdn_conv, _pad_last(gates, 128), _pad_last(gates_t, 128), jnp.tile(gdn_norm, (1, HEADS))[:, None]),
        gdn_tables=_gdn_tables(),
        hy_conv=hy_conv, hy_bias=hy_bias,
    )
    hy_params = (jnp.pad(hy_w1, ((0, 0), (0, 128 - HY_EMB), (0, 0))), hy_b1[:, None], hy_w2, hy_b2[:, None], hy_w3,
                 hy_freq, hy_decay[:, None])
    for name, s in (("hy_ctx", seq_ctx), ("hy_lat", seq_lat)):
        dft = _dft_tables(s)
        P[name] = (dft,) + tuple(_hy_filter(s, *_hy_features(s), hy_params, dft))

    tables = _rope_tables(seq_lat)
    mla_ctx = jnp.concatenate([cache_mla_ckv, _pad_last(cache_mla_kpe, 127),
                               jnp.ones(cache_mla_kpe.shape[:-1] + (1,), F32)], axis=-1)
    swa_ctx = (cache_swa_k.reshape(n_lat_b, DEPTH, past, 128), cache_swa_v.reshape(n_lat_b, DEPTH, past, 128))
    lat_tiles = seq_lat // TOKEN_TILE

    xp = x_prompt.reshape(n_ctx_b * seq_ctx, d)
    xs = x_sample.reshape(n_lat_b * seq_lat, d)
    news = []
    for l in range(DEPTH):
        xp, new = _layer_pass(xp, n_ctx_b, seq_ctx, l, mod, _mod_spec(l, 0, None), P)
        news.append(new)
        xs, _ = _layer_pass(xs, n_lat_b, seq_lat, l, mod, _mod_spec(l, 1, lat_tiles), P, mla_ctx=mla_ctx,
                            swa_ctx=swa_ctx, state=state_gdn, tables=tables)

    stacked = tuple(jnp.stack([news[l][i] for l in range(DEPTH)], axis=1) for i in range(5))
    return (xp.reshape(n_ctx_b, seq_ctx, d), xs.reshape(n_lat_b, seq_lat, d)) + stacked
```

```python
import functools
import math

import jax
import jax.numpy as jnp
import numpy as np
from jax import lax
from jax.experimental import pallas as pl
from jax.experimental.pallas import tpu as pltpu

F32 = jnp.float32
BF16 = jnp.bfloat16
HIGHEST = lax.Precision.HIGHEST

D_MODEL = 1024
DEPTH = 2
GRID_W = 64
HEADS = 4
HEAD_DIM = 64
GROUP_W = 256
MLA_NOPE = 64
MLA_ROPE = 32
MLA_QK = 96
MLA_KV_RANK = 128
SWA_KV_HEADS = 2
SWA_WINDOW = 128
GDN_CHUNK = 64
HY_BANDS = 8
HY_EMB = 17
HY_FF = 64
D_FF = 4096
ROPE_BASE = 10000.0
EPS = 1e-6

W_MLA = 768
W_SWA = 512
W_GDN = 1152
W_HY = 768
W_ALL = W_MLA + W_SWA + W_GDN + W_HY

TOKEN_TILE = 1024
VMEM_LIMIT = 56 * 1024 * 1024

_NT = (((1,), (1,)), ((), ()))


def _params(sem):
    return pltpu.CompilerParams(dimension_semantics=sem, vmem_limit_bytes=VMEM_LIMIT)


def _const_spec(shape):
    nd = len(shape)
    return pl.BlockSpec(shape, lambda *_: (0,) * nd, pipeline_mode=pl.Buffered(1))


def _layer_spec(arr, l):
    shape = arr.shape[1:]
    return pl.BlockSpec((None,) + shape, lambda *_: (l,) + (0,) * len(shape), pipeline_mode=pl.Buffered(1))


def _dot(a, b, precision=None):
    if precision is None:
        a, b = a.astype(BF16), b.astype(BF16)
    return jnp.dot(a, b, preferred_element_type=F32, precision=precision)


def _dot_g(a, b, dims):
    return lax.dot_general(a.astype(BF16), b.astype(BF16), dims, preferred_element_type=F32)


_B_NN = (((2,), (1,)), ((0,), (0,)))
_B_NT = (((2,), (2,)), ((0,), (0,)))
_B_TN = (((1,), (1,)), ((0,), (0,)))


def _bmm(a, b, dims):
    return lax.dot_general(a.astype(BF16), b.astype(BF16), dims, preferred_element_type=F32)


def _split_bf16(x, parts):
    out = []
    for _ in range(parts):
        p = x.astype(BF16)
        out.append(p)
        x = x - p.astype(F32)
    return out


def _rms(x, g):
    return x * lax.rsqrt(jnp.mean(x * x, axis=-1, keepdims=True) + EPS) * g


def _silu(x):
    return x * jax.nn.sigmoid(x)


def _swap_pairs(x):
    n = x.shape[-1]
    nxt = pltpu.roll(x, n - 1, axis=1)
    prv = pltpu.roll(x, 1, axis=1)
    lane = lax.broadcasted_iota(jnp.int32, x.shape, 1)
    return jnp.where((lane & 1) == 0, nxt, prv)


def _rope(x, cos, sin_signed):
    return x * cos + _swap_pairs(x) * sin_signed


def _mod_body(c_ref, w_ref, b_ref, o_ref):
    s = _silu(c_ref[...]).astype(BF16)
    o_ref[...] = _dot(s, w_ref[...].astype(BF16)) + b_ref[...]


def _modulation(cond8, w_ada, b_ada):
    n = 6 * D_MODEL
    tn = 1536
    return pl.pallas_call(
        _mod_body,
        out_shape=jax.ShapeDtypeStruct((DEPTH, 8, n), F32),
        grid=(DEPTH, n // tn),
        in_specs=[
            pl.BlockSpec((8, D_MODEL), lambda l, j: (0, 0)),
            pl.BlockSpec((None, D_MODEL, tn), lambda l, j: (l, 0, j)),
            pl.BlockSpec((None, 1, tn), lambda l, j: (l, 0, j)),
        ],
        out_specs=pl.BlockSpec((None, 8, tn), lambda l, j: (l, 0, j)),
        compiler_params=_params(("arbitrary", "arbitrary")),
        name="modulation",
    )(cond8, w_ada, b_ada.reshape(DEPTH, 1, n))


def _mod_spec(l, row0, tiles_per_row):
    blk = (None, None, 6, 1, D_MODEL)
    if tiles_per_row is None:
        return pl.BlockSpec(blk, lambda i: (l, row0, 0, 0, 0))
    return pl.BlockSpec(blk, lambda i: (l, row0 + i // tiles_per_row, 0, 0, 0))


def _inproj_body(x_ref, mod_ref, g_ref, w_ref, wgt_ref, o_mla, o_swa, o_gdn, o_hy, o_gate_t):
    h = _rms(x_ref[...], g_ref[...]) * (1.0 + mod_ref[1]) + mod_ref[0]
    hb = h.astype(BF16)
    off = 0
    for o in (o_mla, o_swa, o_gdn, o_hy):
        n = o.shape[-1]
        o[...] = _dot(hb, w_ref[:, off:off + n])
        off += n
    o_gate_t[...] = _dot_g(wgt_ref[...], hb, _NT)


def _inproj(x, mod, mod_spec, l, g, w, w_gate_t):
    t = x.shape[0]
    tm = TOKEN_TILE
    widths = (W_MLA, W_SWA, W_GDN, W_HY)
    return pl.pallas_call(
        _inproj_body,
        out_shape=[jax.ShapeDtypeStruct((t, n), F32) for n in widths] + [jax.ShapeDtypeStruct((16, t), F32)],
        grid=(t // tm,),
        in_specs=[
            pl.BlockSpec((tm, D_MODEL), lambda i: (i, 0)),
            mod_spec,
            _layer_spec(g, l), _layer_spec(w, l), _layer_spec(w_gate_t, l),
        ],
        out_specs=[pl.BlockSpec((tm, n), lambda i: (i, 0)) for n in widths]
        + [pl.BlockSpec((16, tm), lambda i: (0, i))],
        compiler_params=_params(("arbitrary",)),
        name="inproj",
    )(x, mod, g, w, w_gate_t)


def _outmlp_body(oa, ob, oc, od, x_ref, mod_ref, g_post_mix, g_pre_mlp, g_post_mlp,
                 wo_ref, w1_ref, w2_ref, out_ref):
    o = (_dot(oa[...], wo_ref[0]) + _dot(ob[...], wo_ref[1])
         + _dot(oc[...], wo_ref[2]) + _dot(od[...], wo_ref[3]))
    x = x_ref[...] + mod_ref[2] * _rms(o, g_post_mix[...])
    hb = (_rms(x, g_pre_mlp[...]) * (1.0 + mod_ref[4]) + mod_ref[3]).astype(BF16)
    acc = jnp.zeros(x.shape, F32)
    fc = 1024
    for c in range(D_FF // fc):
        a = _dot(hb, w1_ref[:, c * fc:(c + 1) * fc])
        a = jnp.square(jnp.maximum(a, 0.0)).astype(BF16)
        acc = acc + _dot(a, w2_ref[c * fc:(c + 1) * fc, :])
    out_ref[...] = x + mod_ref[5] * _rms(acc, g_post_mlp[...])


def _outmlp(o_parts, x, mod, mod_spec, l, g_post_mix, g_pre_mlp, g_post_mlp, wo, w1, w2):
    t = x.shape[0]
    tm = TOKEN_TILE
    part_spec = pl.BlockSpec((tm, GROUP_W), lambda i: (i, 0))
    return pl.pallas_call(
        _outmlp_body,
        out_shape=jax.ShapeDtypeStruct((t, D_MODEL), F32),
        grid=(t // tm,),
        in_specs=[part_spec] * 4 + [
            pl.BlockSpec((tm, D_MODEL), lambda i: (i, 0)),
            mod_spec,
        ] + [_layer_spec(a, l) for a in (g_post_mix, g_pre_mlp, g_post_mlp, wo, w1, w2)],
        out_specs=pl.BlockSpec((tm, D_MODEL), lambda i: (i, 0)),
        compiler_params=_params(("arbitrary",)),
        name="outproj_mlp",
    )(*o_parts, x, mod, g_post_mix, g_pre_mlp, g_post_mlp, wo, w1, w2)


def _sum_lane(h):
    return ((h + 1) % HEADS) * HEAD_DIM


def _rows(ref, j, n):
    return ref.at[pl.ds(j * n, n), :]


def _mla_body(*refs, seq, n_ctx, rope, emit_ckv, row_chunk, sub):
    it = iter(refs)
    q_ref, kv_ref = next(it), next(it)
    ctx_ref = next(it) if n_ctx else None
    g_ref, w_ref = next(it), next(it)
    tabs = tuple(next(it) for _ in range(4)) if rope else None
    o_ref = next(it)
    ckv_out = next(it) if emit_ckv else None
    k_s, v_s = next(it), next(it)
    tq = q_ref.shape[0] // sub
    for j in range(sub):
        _mla_one(_rows(q_ref, j, tq), _rows(kv_ref, j, seq), ctx_ref, g_ref, w_ref, tabs, _rows(o_ref, j, tq),
                 _rows(ckv_out, j, seq) if emit_ckv else None, k_s.at[j], v_s.at[j],
                 seq=seq, n_ctx=n_ctx, rope=rope, emit_ckv=emit_ckv, row_chunk=row_chunk)


def _mla_one(q_ref, kv_ref, ctx_ref, g_ref, w_ref, tabs, o_ref, ckv_out, k_s, v_s, *,
             seq, n_ctx, rope, emit_ckv, row_chunk):
    if rope:
        cq_ref, sq_ref, ck_ref, sk_ref = tabs

    def expand(kin, r0, n):
        kv = _dot(kin.astype(BF16), w_ref[...])
        k_s[r0:r0 + n, :] = kv[:, :HEADS * 128].astype(BF16)
        for h in range(HEADS):
            c0 = HEADS * 128 + h * GROUP_W
            v_s[h, r0:r0 + n, :] = kv[:, c0:c0 + GROUP_W].astype(BF16)

    @pl.when(pl.program_id(1) == 0)
    def _prep():
        for r0 in range(0, seq, row_chunk):
            blk = kv_ref[r0:r0 + row_chunk, :]
            cn = _rms(blk[:, :MLA_KV_RANK], g_ref[...])
            pe = blk[:, MLA_KV_RANK:]
            if rope:
                pe = _rope(pe, ck_ref[r0:r0 + row_chunk, :], sk_ref[r0:r0 + row_chunk, :])
            if emit_ckv:
                ckv_out[r0:r0 + row_chunk, :] = cn
            pe = jnp.where(lax.broadcasted_iota(jnp.int32, pe.shape, 1) == 127, 1.0, pe)
            expand(jnp.concatenate([cn, pe], axis=1), r0, row_chunk)
        if n_ctx:
            expand(ctx_ref[...], seq, n_ctx)

    scale = MLA_QK ** -0.5
    lane_head = lax.broadcasted_iota(jnp.int32, o_ref.shape, 1) // HEAD_DIM
    acc = jnp.zeros(o_ref.shape, F32)
    for h in range(HEADS):
        qh = q_ref[:, h * 128:(h + 1) * 128]
        if rope:
            qh = _rope(qh, cq_ref[...], sq_ref[...])
        s = _dot_g(qh * scale, k_s[:, h * 128:(h + 1) * 128], _NT)
        p = jnp.exp(s - jnp.max(s, axis=-1, keepdims=True))
        pv = _dot(p, v_s[h])
        inv = 1.0 / pv[:, _sum_lane(h):_sum_lane(h) + 1]
        acc = acc + jnp.where(lane_head == h, pv * inv, 0.0)
    o_ref[...] = acc.astype(BF16)


def _mla(proj, n_batch, seq, l, g, w, ctx=None, tables=None, emit_ckv=False):
    tq = min(seq, 1024)
    nq = seq // tq
    sub = 4 if nq == 1 else 1
    n_ctx = 0 if ctx is None else ctx.shape[2]
    rope = tables is not None
    lk = seq + n_ctx
    in_specs = [
        pl.BlockSpec((sub * tq, HEADS * 128), lambda b, i: (b * nq + i, 0)),
        pl.BlockSpec((sub * seq, 256), lambda b, i: (b, 2)),
    ]
    args = [proj, proj]
    if n_ctx:
        in_specs.append(pl.BlockSpec((None, None, n_ctx, 256), lambda b, i: (b, l, 0, 0)))
        args.append(ctx)
    in_specs += [_layer_spec(g, l), _layer_spec(w, l)]
    args += [g, w]
    if rope:
        in_specs += [pl.BlockSpec((tq, 128), lambda b, i: (i, 0))] * 2 + [_const_spec((seq, 128))] * 2
        args += list(tables)
    out_shape = [jax.ShapeDtypeStruct((n_batch * seq, GROUP_W), BF16)]
    out_specs = [pl.BlockSpec((sub * tq, GROUP_W), lambda b, i: (b * nq + i, 0))]
    if emit_ckv:
        out_shape.append(jax.ShapeDtypeStruct((n_batch * seq, MLA_KV_RANK), F32))
        out_specs.append(pl.BlockSpec((sub * seq, MLA_KV_RANK), lambda b, i: (b, 0)))
    body = functools.partial(_mla_body, seq=seq, n_ctx=n_ctx, rope=rope, emit_ckv=emit_ckv,
                             row_chunk=min(seq, 512), sub=sub)
    return pl.pallas_call(
        body, out_shape=out_shape, grid=(n_batch // sub, nq), in_specs=in_specs, out_specs=out_specs,
        scratch_shapes=[pltpu.VMEM((sub, lk, HEADS * 128), BF16), pltpu.VMEM((sub, HEADS, lk, GROUP_W), BF16)],
        compiler_params=_params(("arbitrary", "arbitrary")),
        name="mla_latent" if rope else "mla_context",
    )(*args)


def _swa_body(*refs, seq, n_ctx, local, row_chunk, sub):
    it = iter(refs)
    q_ref, kv_ref = next(it), next(it)
    ctx_refs = (next(it), next(it)) if n_ctx else None
    consts = (next(it), next(it), next(it))
    tabs = tuple(next(it) for _ in range(4)) if local else None
    o_ref = next(it)
    k_s, v_s = next(it), next(it)
    tq = q_ref.shape[0] // sub
    for j in range(sub):
        _swa_one(_rows(q_ref, j, tq), _rows(kv_ref, j, seq), ctx_refs, consts, tabs, _rows(o_ref, j, tq),
                 k_s.at[j], v_s.at[j], seq=seq, n_ctx=n_ctx, local=local, row_chunk=row_chunk)


def _swa_one(q_ref, kv_ref, ctx_refs, consts, tabs, o_ref, k_s, v_s, *, seq, n_ctx, local, row_chunk):
    if n_ctx:
        kc_ref, vc_ref = ctx_refs
    sink_ref, pk_ref, pv_ref = consts
    if local:
        cq_ref, sq_ref, ck_ref, sk_ref = tabs
    tq = q_ref.shape[0]
    blk_id = pl.program_id(1)

    def expand(k, v, r0, n):
        k_s[r0:r0 + n, :] = _dot(k.astype(BF16), pk_ref[...]).astype(BF16)
        vb = v.astype(BF16)
        lane = lax.broadcasted_iota(jnp.int32, (n, GROUP_W), 1)
        for h in range(HEADS):
            ve = _dot(vb, pv_ref[:, h * GROUP_W:(h + 1) * GROUP_W])
            v_s[h, r0:r0 + n, :] = jnp.where(lane == _sum_lane(h), 1.0, ve).astype(BF16)

    @pl.when(blk_id == 0)
    def _prep():
        for r0 in range(0, seq, row_chunk):
            k = kv_ref[r0:r0 + row_chunk, 0:128]
            v = kv_ref[r0:r0 + row_chunk, 128:256]
            if local:
                k = _rope(k, ck_ref[r0:r0 + row_chunk, :], sk_ref[r0:r0 + row_chunk, :])
            expand(k, v, r0, row_chunk)
        if n_ctx:
            expand(kc_ref[...], vc_ref[...], seq, n_ctx)

    scale = HEAD_DIM ** -0.5
    q = q_ref[...]
    if local:
        q = _rope(q, cq_ref[...], sq_ref[...])
        win = tq + 2 * SWA_WINDOW
        start = jnp.clip(blk_id * tq - SWA_WINDOW, 0, seq - win)
        start = pl.multiple_of(start, SWA_WINDOW)
        qpos = blk_id * tq + lax.broadcasted_iota(jnp.int32, (tq, win), 0)
        kpos = start + lax.broadcasted_iota(jnp.int32, (tq, win), 1)
        valid = jnp.abs(qpos - kpos) <= SWA_WINDOW
    q = q * scale
    lane_head = lax.broadcasted_iota(jnp.int32, q.shape, 1) // HEAD_DIM
    acc = jnp.zeros(o_ref.shape, F32)
    for h in range(HEADS):
        qm = jnp.where(lane_head == h, q, 0.0).astype(BF16)
        sink = sink_ref[:, h:h + 1]
        if local:
            s_loc = jnp.where(valid, _dot_g(qm, k_s[pl.ds(start, win), :], _NT), -jnp.inf)
            s_ctx = _dot_g(qm, k_s[seq:seq + n_ctx, :], _NT)
            m = jnp.maximum(jnp.maximum(jnp.max(s_loc, axis=-1, keepdims=True),
                                        jnp.max(s_ctx, axis=-1, keepdims=True)), sink)
            o = (_dot(jnp.exp(s_loc - m), v_s[h, pl.ds(start, win), :])
                 + _dot(jnp.exp(s_ctx - m), v_s[h, seq:seq + n_ctx, :]))
        else:
            s = _dot_g(qm, k_s[...], _NT)
            m = jnp.maximum(jnp.max(s, axis=-1, keepdims=True), sink)
            o = _dot(jnp.exp(s - m), v_s[h])
        den = o[:, _sum_lane(h):_sum_lane(h) + 1] + jnp.exp(sink - m)
        acc = acc + jnp.where(lane_head == h, o * (1.0 / den), 0.0)
    o_ref[...] = acc.astype(BF16)


def _swa(proj, n_batch, seq, l, sink, pk, pv, ctx_kv=None, tables=None):
    local = tables is not None
    tq = 2 * SWA_WINDOW if local else seq
    nq = seq // tq
    sub = 4 if nq == 1 else 1
    n_ctx = 0 if ctx_kv is None else ctx_kv[0].shape[2]
    lk = seq + n_ctx
    in_specs = [
        pl.BlockSpec((sub * tq, GROUP_W), lambda b, i: (b * nq + i, 0)),
        pl.BlockSpec((sub * seq, 256), lambda b, i: (b, 1)),
    ]
    args = [proj, proj]
    if n_ctx:
        in_specs += [pl.BlockSpec((None, None, n_ctx, 128), lambda b, i: (b, l, 0, 0))] * 2
        args += list(ctx_kv)
    in_specs += [_layer_spec(sink, l), _const_spec(pk.shape), _const_spec(pv.shape)]
    args += [sink, pk, pv]
    if local:
        in_specs += [pl.BlockSpec((tq, GROUP_W), lambda b, i: (i, 0))] * 2 + [_const_spec((seq, 128))] * 2
        args += list(tables)
    body = functools.partial(_swa_body, seq=seq, n_ctx=n_ctx, local=local, row_chunk=min(seq, 512), sub=sub)
    return pl.pallas_call(
        body,
        out_shape=jax.ShapeDtypeStruct((n_batch * seq, GROUP_W), BF16),
        grid=(n_batch // sub, nq), in_specs=in_specs,
        out_specs=pl.BlockSpec((sub * tq, GROUP_W), lambda b, i: (b * nq + i, 0)),
        scratch_shapes=[pltpu.VMEM((sub, lk, GROUP_W), BF16), pltpu.VMEM((sub, HEADS, lk, GROUP_W), BF16)],
        compiler_params=_params(("arbitrary", "arbitrary")),
        name="swa_latent" if local else "swa_context",
    )(*args)


def _conv3(x, w_ref):
    n = x.shape[0]
    row = lax.broadcasted_iota(jnp.int32, x.shape, 0)
    prev = jnp.where(row == 0, 0.0, pltpu.roll(x, 1, axis=0))
    nxt = jnp.where(row == n - 1, 0.0, pltpu.roll(x, n - 1, axis=0))
    return prev * w_ref[0:1, :] + x * w_ref[1:2, :] + nxt * w_ref[2:3, :]


def _gdn_body(*refs, seq, has_state, emit_state, sub):
    it = iter(refs)
    x_ref, gt_ref = next(it), next(it)
    s0_ref = next(it) if has_state else None
    consts = tuple(next(it) for _ in range(7))
    o_ref = next(it)
    st_ref = next(it) if emit_state else None
    scratch = tuple(it)
    for j in range(sub):
        _gdn_one(_rows(x_ref, j, seq), gt_ref.at[:, pl.ds(j * seq, seq)], s0_ref, consts, _rows(o_ref, j, seq),
                 st_ref.at[j] if emit_state else None, tuple(s.at[j] for s in scratch),
                 seq=seq, has_state=has_state, emit_state=emit_state)


def _gdn_one(x_ref, gt_ref, s0_ref, consts, o_ref, st_ref, scratch, *, seq, has_state, emit_state):
    cw_ref, gpar_ref, gpart_ref, gnorm_ref, ones_ref, tri_ref, trit_ref = consts
    q_s, k_s, v_s, of_s, ob_s, dec_s, dect_s, gate_s, s_s = scratch[:9]
    st_uv, st_k, st_d, st_w, st_qd, st_attn = scratch[9:]
    pair = 2 * GDN_CHUNK
    n_pair = seq // pair

    def conv_act(g):
        cols = slice(g * GROUP_W, (g + 1) * GROUP_W)
        return _silu(_conv3(x_ref[:, cols], cw_ref.at[:, cols]))

    def head_sums(x):
        return _dot(jnp.concatenate(_split_bf16(x, 2), axis=1), ones_ref[...])

    def head_l2(x):
        return x * lax.rsqrt(head_sums(x * x) + EPS)

    def to_heads(dst, x):
        for h in range(HEADS):
            dst[h] = x[:, h * HEAD_DIM:(h + 1) * HEAD_DIM]

    to_heads(q_s, head_l2(conv_act(0)) * (HEAD_DIM ** -0.5))
    to_heads(k_s, head_l2(conv_act(1)))
    to_heads(v_s, conv_act(2))
    gab = x_ref[:, 4 * GROUP_W:4 * GROUP_W + 128]
    glog = -jnp.exp(gpar_ref[0:1, :]) * jax.nn.softplus(gab + gpar_ref[1:2, :])
    gate_s[...] = jax.nn.sigmoid(gab)
    lane = lax.broadcasted_iota(jnp.int32, (pair, 128), 1)
    glogt = -jnp.exp(gpart_ref[:, 0:1]) * jax.nn.softplus(gt_ref[...] + gpart_ref[:, 1:2])
    rowi = lax.broadcasted_iota(jnp.int32, (16, pair), 0)
    g3 = _split_bf16(glog, 3)
    gt3 = _split_bf16(glogt, 3)
    for p in range(n_pair):
        r = slice(p * pair, (p + 1) * pair)
        gcol = jnp.concatenate([g[r, :] for g in g3], axis=0)
        grow = jnp.concatenate([g[:, r] for g in gt3], axis=1)
        dec_s[r, :] = jnp.where(lane < HEADS, _dot(tri_ref[0], gcol), _dot(tri_ref[1], gcol))
        dect_s[:, r] = jnp.where(rowi < HEADS, _dot(grow, trit_ref[1]), _dot(grow, trit_ref[0]))

    if has_state:
        s_s[...] = s0_ref[...]
    else:
        s_s[...] = jnp.zeros(s_s.shape, F32)

    ri = lax.broadcasted_iota(jnp.int32, (pair, pair), 0)
    ci = lax.broadcasted_iota(jnp.int32, (pair, pair), 1)
    same = (ri // GDN_CHUNK) == (ci // GDN_CHUNK)
    nb = 2 * HEADS
    incl = (same & (ri >= ci), same & (ri <= ci))
    strict = (same & (ri > ci), same & (ri < ci))
    eye = (ri == ci).astype(F32)
    merge = [((ri // (2 * s)) == (ci // (2 * s))) & ((ri // s) != (ci // s))
             for s in (1, 2, 4, 8, 16, 32)]
    c = GDN_CHUNK

    def first_second(x):
        return (jnp.concatenate([x[:HEADS, :c], x[HEADS:, c:]], axis=0),
                jnp.concatenate([x[:HEADS, c:], x[HEADS:, :c]], axis=0))

    def row_order(first, second):
        return jnp.concatenate([jnp.concatenate([first[:HEADS], second[:HEADS]], axis=1),
                                jnp.concatenate([second[HEADS:], first[HEADS:]], axis=1)], axis=0)

    def rows_of(i):
        fwd, bwd = i * pair, (n_pair - 1 - i) * pair
        if not isinstance(i, int):
            fwd, bwd = pl.multiple_of(fwd, pair), pl.multiple_of(bwd, pair)
        return pl.ds(fwd, pair), pl.ds(bwd, pair)

    def solve_stage(steps, filler=iter(())):
        ab, t, rhs_v, rhs_w, dst = [], [], [], [], []
        for i, d in [(i, d) for i in steps for d in range(2)]:
            slot = i % 2
            r = rows_of(i)[d]
            gd_, dd_, td_ = gate_s[r, :], dec_s[r, :], dect_s[:, r]
            for h in range(HEADS):
                p, col = d * HEADS + h, d * HEADS + h
                dst.append((slot, p))
                q, k, v = q_s[h, r, :], k_s[h, r, :], v_s[h, r, :]
                beta = gd_[:, 8 + col:9 + col]
                dcol = dd_[:, col:col + 1]
                gam = jnp.where(incl[d], jnp.exp(jnp.where(incl[d], dcol - td_[col:col + 1, :], 0.0)), 0.0)
                kb = k * beta
                kk_qk = _dot_g(jnp.concatenate([kb, q], axis=0), k, _NT)
                a = jnp.where(strict[d], kk_qk[:pair] * gam, 0.0)
                edec = jnp.exp(dcol)
                st_attn[slot, p] = (kk_qk[pair:] * gam).astype(BF16)
                st_qd[slot, p] = (q * edec).astype(BF16)
                st_k[slot, p] = k
                st_d[slot, p] = dcol
                ab.append(a.astype(BF16))
                t.append((eye - jnp.where(merge[0], a, 0.0)).astype(BF16))
                rhs_v.append(v * beta)
                rhs_w.append(kb * edec)
        n_prob = len(dst)
        for m in merge[1:]:
            next(filler, None)
            x = [_dot(t[j], jnp.where(m, ab[j], 0.0)) for j in range(n_prob)]
            x = [_dot(x[j], t[j]) for j in range(n_prob)]
            t = [t[j] - x[j].astype(BF16) for j in range(n_prob)]
        for j, (slot, p) in enumerate(dst):
            st_uv[slot, p] = _dot(t[j], rhs_v[j])
            st_w[slot, p] = _dot(t[j], rhs_w[j]).astype(BF16)
        for _ in filler:
            pass

    def state_steps(i):
        rf, rb = rows_of(i)
        slot = i % 2
        uv1, uv2 = first_second(st_uv[slot])
        w1, w2 = first_second(st_w[slot])
        qd1, qd2 = first_second(st_qd[slot])
        k1, k2 = first_second(st_k[slot])
        d1, d2 = first_second(st_d[slot])
        dl1 = jnp.concatenate([d1[:HEADS, c - 1:c], d1[HEADS:, 0:1]], axis=0)
        dl2 = jnp.concatenate([d2[:HEADS, c - 1:c], d2[HEADS:, 0:1]], axis=0)
        s = s_s[...].reshape(nb, HEAD_DIM, HEAD_DIM)
        u1 = uv1 - _bmm(w1, s, _B_NN)
        o1 = _bmm(qd1, s, _B_NN)
        yield
        s = s * jnp.exp(dl1) + _bmm(k1 * jnp.exp(dl1 - d1), u1, _B_TN)
        yield
        u2 = uv2 - _bmm(w2, s, _B_NN)
        o2 = _bmm(qd2, s, _B_NN)
        yield
        s = s * jnp.exp(dl2) + _bmm(k2 * jnp.exp(dl2 - d2), u2, _B_TN)
        s_s[...] = s.reshape(2, HEADS, HEAD_DIM, HEAD_DIM)
        yield
        o = row_order(o1, o2) + _bmm(st_attn[slot], row_order(u1, u2), _B_NN)
        for j in range(HEADS // 2):
            of_s[j, rf, :] = jnp.concatenate([o[2 * j], o[2 * j + 1]], axis=1)
            ob_s[j, rb, :] = jnp.concatenate([o[HEADS + 2 * j], o[HEADS + 2 * j + 1]], axis=1)

    def loop_body(i, carry):
        solve_stage((i + 1,), state_steps(i))
        return carry

    if n_pair == 2:
        solve_stage((0, 1))
        for _ in state_steps(0):
            pass
    else:
        solve_stage((0,))
        lax.fori_loop(0, n_pair - 1, loop_body, 0)
    for _ in state_steps(n_pair - 1):
        pass

    o = jnp.concatenate([of_s[j] + ob_s[j] for j in range(HEADS // 2)], axis=1)
    ms = head_sums(o * o) * (1.0 / HEAD_DIM)
    o = o * lax.rsqrt(ms + EPS) * gnorm_ref[...]
    o_ref[...] = (o * _silu(x_ref[:, 3 * GROUP_W:4 * GROUP_W])).astype(BF16)
    if emit_state:
        st_ref[...] = s_s[...]


def _gdn(proj, gate_t, n_batch, seq, l, layer_params, tables, state=None, emit_state=False):
    has_state = state is not None
    sub = 1 if has_state else 2
    in_specs = [pl.BlockSpec((sub * seq, W_GDN), lambda b: (b, 0), pipeline_mode=pl.Buffered(1)),
                pl.BlockSpec((16, sub * seq), lambda b: (0, b))]
    args = [proj, gate_t]
    state_block = (sub, 2, HEADS, HEAD_DIM, HEAD_DIM)
    if has_state:
        in_specs.append(pl.BlockSpec((None, None) + state_block[1:], lambda b: (b, l, 0, 0, 0, 0)))
        args.append(state)
    for p in layer_params:
        in_specs.append(_layer_spec(p, l))
        args.append(p)
    for c in tables:
        in_specs.append(_const_spec(c.shape))
        args.append(c)
    out_shape = [jax.ShapeDtypeStruct((n_batch * seq, GROUP_W), BF16)]
    out_specs = [pl.BlockSpec((sub * seq, GROUP_W), lambda b: (b, 0))]
    if emit_state:
        out_shape.append(jax.ShapeDtypeStruct((n_batch, 2, HEADS, HEAD_DIM, HEAD_DIM), F32))
        out_specs.append(pl.BlockSpec(state_block, lambda b: (b, 0, 0, 0, 0)))
    vmem = lambda shape, dtype: pltpu.VMEM((sub,) + shape, dtype)
    scratch = [vmem((HEADS, seq, HEAD_DIM), F32)] * 3 + [vmem((HEADS // 2, seq, 128), F32)] * 2 + [
        vmem((seq, 128), F32), vmem((16, seq), F32), vmem((seq, 128), F32),
        vmem((2, HEADS, HEAD_DIM, HEAD_DIM), F32)]
    nb, pair = 2 * HEADS, 2 * GDN_CHUNK
    scratch += [vmem((2, nb, pair, HEAD_DIM), F32), vmem((2, nb, pair, HEAD_DIM), F32),
                vmem((2, nb, pair, 1), F32), vmem((2, nb, pair, HEAD_DIM), BF16),
                vmem((2, nb, pair, HEAD_DIM), BF16), vmem((2, nb, pair, pair), BF16)]
    body = functools.partial(_gdn_body, seq=seq, has_state=has_state, emit_state=emit_state, sub=sub)
    return pl.pallas_call(
        body, out_shape=out_shape, grid=(n_batch // sub,), in_specs=in_specs, out_specs=out_specs,
        scratch_shapes=scratch, compiler_params=_params(("arbitrary",)),
        name="gdn_latent" if has_state else "gdn_context",
    )(*args)


def _hy_filter_body(feat_ref, dist_ref, w1_ref, b1_ref, w2_ref, b2_ref, w3_ref, freq_ref, decay_ref,
                    cf_ref, sf_ref, rot_ref, hr_ref, hi_ref, nyq_ref, *, seq):
    h = jnp.sin(freq_ref[0:1, :] * (_dot(feat_ref[...], w1_ref[...], HIGHEST) + b1_ref[...]))
    h = jnp.sin(freq_ref[1:2, :] * (_dot(h, w2_ref[...], HIGHEST) + b2_ref[...]))
    h = _dot(h, w3_ref[...], HIGHEST)
    filt = h * jnp.exp(-dist_ref[...] * jnp.abs(decay_ref[...]))
    kf = cf_ref.shape[0]
    t = lax.broadcasted_iota(jnp.int32, (seq, 1), 0)
    alt = jnp.where((t & 1) == 0, 1.0, -1.0)
    nyq_ref[...] = jnp.sum(filt * alt, axis=0, keepdims=True) * (1.0 / (2 * kf))
    fb = filt.astype(BF16)
    tk = min(kf, 512)
    for k0 in range(0, kf, tk):
        hr = _dot(cf_ref[k0:k0 + tk, :], fb)
        hs = _dot(sf_ref[k0:k0 + tk, :], fb)
        wc, ws = rot_ref[k0:k0 + tk, 0:1], rot_ref[k0:k0 + tk, 1:2]
        hr_ref[k0:k0 + tk, :] = hr * wc + hs * ws
        hi_ref[k0:k0 + tk, :] = hr * ws - hs * wc


def _hy_filter(seq, feats, dist, layer_params, dft):
    cf, sf, _, _, rot = dft
    kf = cf.shape[0]
    per_layer = lambda a: pl.BlockSpec((None,) + a.shape[1:], lambda l: (l,) + (0,) * (a.ndim - 1))
    out_block = lambda rows: pl.BlockSpec((None, rows, 2 * GROUP_W), lambda l: (l, 0, 0))
    return pl.pallas_call(
        functools.partial(_hy_filter_body, seq=seq),
        out_shape=[jax.ShapeDtypeStruct((DEPTH, kf, 2 * GROUP_W), F32),
                   jax.ShapeDtypeStruct((DEPTH, kf, 2 * GROUP_W), F32),
                   jax.ShapeDtypeStruct((DEPTH, 1, 2 * GROUP_W), F32)],
        grid=(DEPTH,),
        in_specs=[_const_spec(feats.shape), _const_spec(dist.shape)] + [per_layer(a) for a in layer_params]
        + [_const_spec(cf.shape), _const_spec(sf.shape), _const_spec(rot.shape)],
        out_specs=[out_block(kf), out_block(kf), out_block(1)],
        compiler_params=_params(("arbitrary",)),
        name="hyena_filter",
    )(feats, dist, *layer_params, cf, sf, rot)


def _hy_conv_body(x_ref, cw_ref, bias_ref, cf_ref, sf_ref, cft_ref, sft_ref, hr_ref, hi_ref, nyq_ref, o_ref,
                  yr_s, yi_s, *, seq, sub):
    for j in range(sub):
        _hy_conv_one(_rows(x_ref, j, seq), cw_ref, bias_ref, cf_ref, sf_ref, cft_ref, sft_ref, hr_ref, hi_ref,
                     nyq_ref, _rows(o_ref, j, seq), yr_s.at[j], yi_s.at[j], seq=seq)


def _hy_conv_one(x_ref, cw_ref, bias_ref, cf_ref, sf_ref, cft_ref, sft_ref, hr_ref, hi_ref, nyq_ref, o_ref,
                 yr_s, yi_s, *, seq):
    t = lax.broadcasted_iota(jnp.int32, (seq, 1), 0)
    alt = jnp.where((t & 1) == 0, 1.0, -1.0)
    kf = cf_ref.shape[0]
    tk = min(kf, 512)

    def long_conv(a, o):
        cols = slice(o * GROUP_W, (o + 1) * GROUP_W)
        ab = a.astype(BF16)
        for k0 in range(0, kf, tk):
            ur = _dot(cf_ref[k0:k0 + tk, :], ab)
            us = _dot(sf_ref[k0:k0 + tk, :], ab)
            hr = hr_ref[k0:k0 + tk, cols]
            hi = hi_ref[k0:k0 + tk, cols]
            yr_s[k0:k0 + tk, :] = (ur * hr + us * hi).astype(BF16)
            yi_s[k0:k0 + tk, :] = (ur * hi - us * hr).astype(BF16)
        u_nyq = jnp.sum(a * alt, axis=0, keepdims=True)
        y = _dot(cft_ref[...], yr_s[...]) - _dot(sft_ref[...], yi_s[...])
        return y + alt * (u_nyq * nyq_ref[:, cols])

    v = _conv3(x_ref[:, 0:GROUP_W], cw_ref.at[:, 0:GROUP_W])
    x1 = _conv3(x_ref[:, GROUP_W:2 * GROUP_W], cw_ref.at[:, GROUP_W:2 * GROUP_W])
    z = x1 * (long_conv(v, 0) + v * bias_ref[0:1, :])
    x2 = _conv3(x_ref[:, 2 * GROUP_W:3 * GROUP_W], cw_ref.at[:, 2 * GROUP_W:3 * GROUP_W])
    o_ref[...] = (x2 * (long_conv(z, 1) + z * bias_ref[1:2, :])).astype(BF16)


def _hy_conv(proj, n_batch, seq, l, cw, bias, dft, hr, hi, nyq):
    cf, sf, cft, sft, _ = dft
    kf = cf.shape[0]
    sub = 4 if seq <= 512 else 1
    return pl.pallas_call(
        functools.partial(_hy_conv_body, seq=seq, sub=sub),
        out_shape=jax.ShapeDtypeStruct((n_batch * seq, GROUP_W), BF16),
        grid=(n_batch // sub,),
        in_specs=[pl.BlockSpec((sub * seq, W_HY), lambda b: (b, 0), pipeline_mode=pl.Buffered(1)),
                  _layer_spec(cw, l), _layer_spec(bias, l)] + [_const_spec(a.shape) for a in (cf, sf, cft, sft)]
        + [_layer_spec(hr, l), _layer_spec(hi, l), _layer_spec(nyq, l)],
        out_specs=pl.BlockSpec((sub * seq, GROUP_W), lambda b: (b, 0)),
        scratch_shapes=[pltpu.VMEM((sub, kf, GROUP_W), BF16)] * 2,
        compiler_params=_params(("arbitrary",)),
        name="hyena_conv",
    )(proj, cw, bias, cf, sf, cft, sft, hr, hi, nyq)


def _rope_tables(seq):
    rows = seq // GRID_W
    row = np.repeat(np.arange(rows), GRID_W).astype(np.float64)
    col = np.tile(np.arange(GRID_W), rows).astype(np.float64)

    def pair_tables(dim):
        n_freq = dim // 4
        inv = (ROPE_BASE ** (-np.arange(n_freq, dtype=np.float32) / n_freq)).astype(np.float64)
        ang = np.concatenate([row[:, None] * inv, col[:, None] * inv], axis=-1).astype(np.float32)
        cos = np.repeat(np.cos(ang), 2, axis=-1)
        sin = np.repeat(np.sin(ang), 2, axis=-1) * np.tile(np.array([-1.0, 1.0], np.float32), dim // 2)
        return cos, sin

    ca, sa = pair_tables(MLA_ROPE)
    one = lambda n: np.ones((seq, n), np.float32)
    zero = lambda n: np.zeros((seq, n), np.float32)
    mla = (np.concatenate([one(MLA_NOPE), ca, one(32)], 1), np.concatenate([zero(MLA_NOPE), sa, zero(32)], 1),
           np.concatenate([ca, one(96)], 1), np.concatenate([sa, zero(96)], 1))
    cb, sb = pair_tables(HEAD_DIM)
    swa = (np.tile(cb, (1, HEADS)), np.tile(sb, (1, HEADS)),
           np.tile(cb, (1, SWA_KV_HEADS)), np.tile(sb, (1, SWA_KV_HEADS)))
    as_f32 = lambda ts: tuple(jnp.asarray(t, F32) for t in ts)
    return as_f32(mla), as_f32(swa)


def _dft_tables(seq):
    n = 3 * seq // 2 if (3 * seq // 4) % 256 == 0 else 2 * seq
    kf = n // 2
    k = np.arange(kf, dtype=np.int64)
    s = np.arange(seq, dtype=np.int64)
    ang = ((k[:, None] * s[None, :]) % n).astype(np.float64) * (2.0 * math.pi / n)
    cos, sin = np.cos(ang), np.sin(ang)
    theta = ((k * (seq // 2)) % n).astype(np.float64) * (2.0 * math.pi / n)
    wgt = np.where(k == 0, 1.0 / n, 2.0 / n)
    rot = np.zeros((kf, 128), np.float64)
    rot[:, 0], rot[:, 1] = wgt * np.cos(theta), wgt * np.sin(theta)
    bf = lambda a: jnp.asarray(a, F32).astype(BF16)
    return bf(cos), bf(sin), bf(cos.T), bf(sin.T), jnp.asarray(rot, F32)


def _hy_features(seq):
    t = np.arange(seq, dtype=np.float32)
    t01 = t / np.float32(max(seq - 1, 1))
    w = (np.float32(2.0 * math.pi) * t / np.float32(seq)).astype(np.float64)
    bands = np.linspace(1e-4, HY_BANDS - 1, HY_BANDS, dtype=np.float32).astype(np.float64)
    feats = np.concatenate([t01[:, None].astype(np.float64), np.cos(w[:, None] * bands), -np.sin(w[:, None] * bands)],
                           axis=-1)
    feats = np.pad(feats, ((0, 0), (0, 128 - HY_EMB)))
    dist = (np.abs(t - (seq // 2)) / np.float32(seq / 2))[:, None]
    return jnp.asarray(feats, F32), jnp.asarray(dist, F32)


def _swa_placement():
    pk = np.zeros((128, GROUP_W), np.float32)
    pv = np.zeros((128, HEADS, GROUP_W), np.float32)
    group = HEADS // SWA_KV_HEADS
    for h in range(HEADS):
        for e in range(HEAD_DIM):
            src = (h // group) * HEAD_DIM + e
            pk[src, h * HEAD_DIM + e] = 1.0
            pv[src, h, h * HEAD_DIM + e] = 1.0
    return jnp.asarray(pk, BF16), jnp.asarray(pv.reshape(128, HEADS * GROUP_W), BF16)


def _gdn_tables():
    pair = 2 * GDN_CHUNK
    i = np.arange(pair)
    same = (i[:, None] // GDN_CHUNK) == (i[None, :] // GDN_CHUNK)
    lower = (same & (i[:, None] >= i[None, :])).astype(np.float32)
    upper = (same & (i[:, None] <= i[None, :])).astype(np.float32)
    j = np.arange(GROUP_W)
    ones_bd = ((j[:, None] // HEAD_DIM) == (j[None, :] // HEAD_DIM)).astype(np.float32)
    ones2 = np.concatenate([ones_bd] * 2, axis=0)
    tri3 = np.stack([np.concatenate([m] * 3, axis=1) for m in (lower, upper)])
    trit3 = np.stack([np.concatenate([m] * 3, axis=0) for m in (lower, upper)])
    return jnp.asarray(ones2, BF16), jnp.asarray(tri3, BF16), jnp.asarray(trit3, BF16)


def _pad_last(x, n):
    return jnp.pad(x, [(0, 0)] * (x.ndim - 1) + [(0, n - x.shape[-1])])


def _layout_w_in(w):
    dep, d, _ = w.shape
    mq = _pad_last(w[..., :384].reshape(dep, d, HEADS, MLA_QK), 128).reshape(dep, d, HEADS * 128)
    mla = _pad_last(jnp.concatenate([mq, w[..., 384:544]], axis=-1), W_MLA)
    gdn = _pad_last(w[..., 1056:2096], W_GDN)
    w_all = jnp.concatenate([mla, w[..., 544:1056], gdn, w[..., 2096:2864]], axis=-1).astype(BF16)
    w_gate_t = jnp.swapaxes(w[..., 2080:2096], 1, 2).astype(BF16)
    return w_all, w_gate_t


def _layout_w_ukv(w):
    dep = w.shape[0]
    w = w.reshape(dep, MLA_KV_RANK, HEADS, 128)
    wk = _pad_last(w[..., :MLA_NOPE], 128).reshape(dep, MLA_KV_RANK, HEADS * 128)
    place = np.zeros((128, HEADS, 128), np.float32)
    for j in range(MLA_ROPE):
        place[j, :, MLA_NOPE + j] = 1.0
    place = jnp.broadcast_to(jnp.asarray(place.reshape(128, HEADS * 128)), (dep, 128, HEADS * 128))
    wk = jnp.concatenate([wk, place], axis=1)
    eye = jnp.asarray(np.eye(HEADS, dtype=np.float32))
    wv = (w[..., MLA_NOPE:][:, :, :, None, :] * eye[None, None, :, :, None]).reshape(dep, MLA_KV_RANK, HEADS * GROUP_W)
    ones_route = np.zeros((128, HEADS * GROUP_W), np.float32)
    for h in range(HEADS):
        ones_route[127, h * GROUP_W + _sum_lane(h)] = 1.0
    wv = jnp.concatenate([wv, jnp.broadcast_to(jnp.asarray(ones_route), (dep,) + ones_route.shape)], axis=1)
    return jnp.concatenate([wk, wv], axis=-1).astype(BF16)


def _layer_pass(x, n_batch, seq, l, mod, mod_spec, P, mla_ctx=None, swa_ctx=None, state=None, tables=None):
    is_ctx = tables is None
    p_mla, p_swa, p_gdn, p_hy, gate_t = _inproj(x, mod, mod_spec, l, P["g_pre_mix"], P["w_in"], P["w_gate_t"])
    mla_out = _mla(p_mla, n_batch, seq, l, P["mla_kv_norm"], P["w_ukv"], ctx=mla_ctx,
                   tables=None if is_ctx else tables[0], emit_ckv=is_ctx)
    o_a = mla_out[0]
    o_b = _swa(p_swa, n_batch, seq, l, P["swa_sink"], P["swa_pk"], P["swa_pv"], ctx_kv=swa_ctx,
               tables=None if is_ctx else tables[1])
    gdn_out = _gdn(p_gdn, gate_t, n_batch, seq, l, P["gdn_params"], P["gdn_tables"], state=state, emit_state=is_ctx)
    o_c = gdn_out[0]
    o_d = _hy_conv(p_hy, n_batch, seq, l, P["hy_conv"], P["hy_bias"], *(P["hy_ctx"] if is_ctx else P["hy_lat"]))
    y = _outmlp((o_a, o_b, o_c, o_d), x, mod, mod_spec, l, P["g_post_mix"], P["g_pre_mlp"], P["g_post_mlp"],
                P["w_out"], P["mlp_w1"], P["mlp_w2"])
    if not is_ctx:
        return y, None
    kpe0 = HEADS * 128 + MLA_KV_RANK
    new = (mla_out[1].reshape(n_batch, seq, MLA_KV_RANK),
           p_mla[:, kpe0:kpe0 + MLA_ROPE].reshape(n_batch, seq, MLA_ROPE),
           p_swa[:, 256:384].reshape(n_batch, seq, SWA_KV_HEADS, HEAD_DIM),
           p_swa[:, 384:512].reshape(n_batch, seq, SWA_KV_HEADS, HEAD_DIM),
           gdn_out[1])
    return y, new


def kernel(x_prompt, x_sample, cache_mla_ckv, cache_mla_kpe, cache_swa_k, cache_swa_v, state_gdn, c, c_ctx, w_ada, b_ada, g_pre_mix, g_post_mix, g_pre_mlp, g_post_mlp, w_in, w_out, mla_kv_norm, mla_w_ukv, swa_sink, gdn_conv, gdn_a_log, gdn_dt_bias, gdn_norm, hy_conv, hy_w1, hy_b1, hy_w2, hy_b2, hy_w3, hy_freq, hy_decay, hy_bias, mlp_w1, mlp_w2):
    n_ctx_b, seq_ctx, d = x_prompt.shape
    n_lat_b, seq_lat, _ = x_sample.shape
    past = cache_mla_ckv.shape[2]

    cond8 = jnp.concatenate([c_ctx[None, :], c, jnp.zeros((8 - 1 - n_lat_b, d), F32)], axis=0)
    mod = _modulation(cond8, w_ada, b_ada).reshape(DEPTH, 8, 6, 1, d)

    w_in_all, w_gate_t = _layout_w_in(w_in)
    gates = jnp.stack([gdn_a_log.reshape(DEPTH, 8), gdn_dt_bias.reshape(DEPTH, 8)], axis=1)
    gates_t = jnp.pad(jnp.swapaxes(gates, 1, 2), ((0, 0), (0, 8), (0, 0)))
    swa_pk, swa_pv = _swa_placement()
    P = dict(
        g_pre_mix=g_pre_mix[:, None], g_post_mix=g_post_mix[:, None],
        g_pre_mlp=g_pre_mlp[:, None], g_post_mlp=g_post_mlp[:, None],
        w_in=w_in_all, w_gate_t=w_gate_t,
        w_out=w_out.reshape(DEPTH, HEADS, GROUP_W, d).astype(BF16),
        mlp_w1=mlp_w1.astype(BF16), mlp_w2=mlp_w2.astype(BF16),
        mla_kv_norm=mla_kv_norm[:, None], w_ukv=_layout_w_ukv(mla_w_ukv),
        swa_sink=_pad_last(swa_sink, 128)[:, None], swa_pk=swa_pk, swa_pv=swa_pv,
        gdn_params=(gdn_conv, _pad_last(gates, 128), _pad_last(gates_t, 128), jnp.tile(gdn_norm, (1, HEADS))[:, None]),
        gdn_tables=_gdn_tables(),
        hy_conv=hy_conv, hy_bias=hy_bias,
    )
    hy_params = (jnp.pad(hy_w1, ((0, 0), (0, 128 - HY_EMB), (0, 0))), hy_b1[:, None], hy_w2, hy_b2[:, None], hy_w3,
                 hy_freq, hy_decay[:, None])
    for name, s in (("hy_ctx", seq_ctx), ("hy_lat", seq_lat)):
        dft = _dft_tables(s)
        P[name] = (dft,) + tuple(_hy_filter(s, *_hy_features(s), hy_params, dft))

    tables = _rope_tables(seq_lat)
    mla_ctx = jnp.concatenate([cache_mla_ckv, _pad_last(cache_mla_kpe, 127),
                               jnp.ones(cache_mla_kpe.shape[:-1] + (1,), F32)], axis=-1)
    swa_ctx = (cache_swa_k.reshape(n_lat_b, DEPTH, past, 128), cache_swa_v.reshape(n_lat_b, DEPTH, past, 128))
    lat_tiles = seq_lat // TOKEN_TILE

    xp = x_prompt.reshape(n_ctx_b * seq_ctx, d)
    xs = x_sample.reshape(n_lat_b * seq_lat, d)
    news = []
    for l in range(DEPTH):
        xp, new = _layer_pass(xp, n_ctx_b, seq_ctx, l, mod, _mod_spec(l, 0, None), P)
        news.append(new)
        xs, _ = _layer_pass(xs, n_lat_b, seq_lat, l, mod, _mod_spec(l, 1, lat_tiles), P, mla_ctx=mla_ctx,
                            swa_ctx=swa_ctx, state=state_gdn, tables=tables)

    stacked = tuple(jnp.stack([news[l][i] for l in range(DEPTH)], axis=1) for i in range(5))
    return (xp.reshape(n_ctx_b, seq_ctx, d), xs.reshape(n_lat_b, seq_lat, d)) + stacked
```

```python
import functools
import math

import jax
import jax.numpy as jnp
import numpy as np
from jax import lax
from jax.experimental import pallas as pl
from jax.experimental.pallas import tpu as pltpu

F32 = jnp.float32
BF16 = jnp.bfloat16
HIGHEST = lax.Precision.HIGHEST

D_MODEL = 1024
DEPTH = 2
GRID_W = 64
HEADS = 4
HEAD_DIM = 64
GROUP_W = 256
MLA_NOPE = 64
MLA_ROPE = 32
MLA_QK = 96
MLA_KV_RANK = 128
SWA_KV_HEADS = 2
SWA_WINDOW = 128
GDN_CHUNK = 64
GDN_GROUP = 2
HY_BANDS = 8
HY_EMB = 17
HY_FF = 64
D_FF = 4096
ROPE_BASE = 10000.0
EPS = 1e-6

W_MLA = 768
W_SWA = 512
W_GDN = 1152
W_HY = 768
W_ALL = W_MLA + W_SWA + W_GDN + W_HY

TOKEN_TILE = 1024
VMEM_LIMIT = 56 * 1024 * 1024

_NT = (((1,), (1,)), ((), ()))


def _params(sem):
    return pltpu.CompilerParams(dimension_semantics=sem, vmem_limit_bytes=VMEM_LIMIT)


def _const_spec(shape):
    nd = len(shape)
    return pl.BlockSpec(shape, lambda *_: (0,) * nd, pipeline_mode=pl.Buffered(1))


def _layer_spec(arr, l):
    shape = arr.shape[1:]
    return pl.BlockSpec((None,) + shape, lambda *_: (l,) + (0,) * len(shape), pipeline_mode=pl.Buffered(1))


def _dot(a, b, precision=None):
    if precision is None:
        a, b = a.astype(BF16), b.astype(BF16)
    return jnp.dot(a, b, preferred_element_type=F32, precision=precision)


def _dot_g(a, b, dims):
    return lax.dot_general(a.astype(BF16), b.astype(BF16), dims, preferred_element_type=F32)


_B_NN = (((2,), (1,)), ((0,), (0,)))
_B_NT = (((2,), (2,)), ((0,), (0,)))
_B_TN = (((1,), (1,)), ((0,), (0,)))


def _bmm(a, b, dims):
    return lax.dot_general(a.astype(BF16), b.astype(BF16), dims, preferred_element_type=F32)


def _split_bf16(x, parts):
    out = []
    for _ in range(parts):
        p = x.astype(BF16)
        out.append(p)
        x = x - p.astype(F32)
    return out


def _rms(x, g):
    return x * lax.rsqrt(jnp.mean(x * x, axis=-1, keepdims=True) + EPS) * g


def _silu(x):
    return x * jax.nn.sigmoid(x)


def _swap_pairs(x):
    n = x.shape[-1]
    nxt = pltpu.roll(x, n - 1, axis=1)
    prv = pltpu.roll(x, 1, axis=1)
    lane = lax.broadcasted_iota(jnp.int32, x.shape, 1)
    return jnp.where((lane & 1) == 0, nxt, prv)


def _rope(x, cos, sin_signed):
    return x * cos + _swap_pairs(x) * sin_signed


def _mod_body(c_ref, w_ref, b_ref, o_ref):
    s = _silu(c_ref[...]).astype(BF16)
    o_ref[...] = _dot(s, w_ref[...].astype(BF16)) + b_ref[...]


def _modulation(cond8, w_ada, b_ada):
    n = 6 * D_MODEL
    tn = 1536
    return pl.pallas_call(
        _mod_body,
        out_shape=jax.ShapeDtypeStruct((DEPTH, 8, n), F32),
        grid=(DEPTH, n // tn),
        in_specs=[
            pl.BlockSpec((8, D_MODEL), lambda l, j: (0, 0)),
            pl.BlockSpec((None, D_MODEL, tn), lambda l, j: (l, 0, j)),
            pl.BlockSpec((None, 1, tn), lambda l, j: (l, 0, j)),
        ],
        out_specs=pl.BlockSpec((None, 8, tn), lambda l, j: (l, 0, j)),
        compiler_params=_params(("arbitrary", "arbitrary")),
        name="modulation",
    )(cond8, w_ada, b_ada.reshape(DEPTH, 1, n))


def _mod_spec(l, row0, tiles_per_row):
    blk = (None, None, 6, 1, D_MODEL)
    if tiles_per_row is None:
        return pl.BlockSpec(blk, lambda i: (l, row0, 0, 0, 0))
    return pl.BlockSpec(blk, lambda i: (l, row0 + i // tiles_per_row, 0, 0, 0))


def _inproj_body(x_ref, mod_ref, g_ref, w_ref, wgt_ref, o_mla, o_swa, o_gdn, o_hy, o_gate_t):
    h = _rms(x_ref[...], g_ref[...]) * (1.0 + mod_ref[1]) + mod_ref[0]
    hb = h.astype(BF16)
    off = 0
    for o in (o_mla, o_swa, o_gdn, o_hy):
        n = o.shape[-1]
        o[...] = _dot(hb, w_ref[:, off:off + n])
        off += n
    o_gate_t[...] = _dot_g(wgt_ref[...], hb, _NT)


def _inproj(x, mod, mod_spec, l, g, w, w_gate_t):
    t = x.shape[0]
    tm = TOKEN_TILE
    widths = (W_MLA, W_SWA, W_GDN, W_HY)
    return pl.pallas_call(
        _inproj_body,
        out_shape=[jax.ShapeDtypeStruct((t, n), F32) for n in widths] + [jax.ShapeDtypeStruct((16, t), F32)],
        grid=(t // tm,),
        in_specs=[
            pl.BlockSpec((tm, D_MODEL), lambda i: (i, 0)),
            mod_spec,
            _layer_spec(g, l), _layer_spec(w, l), _layer_spec(w_gate_t, l),
        ],
        out_specs=[pl.BlockSpec((tm, n), lambda i: (i, 0)) for n in widths]
        + [pl.BlockSpec((16, tm), lambda i: (0, i))],
        compiler_params=_params(("arbitrary",)),
        name="inproj",
    )(x, mod, g, w, w_gate_t)


def _outmlp_body(oa, ob, oc, od, x_ref, mod_ref, g_post_mix, g_pre_mlp, g_post_mlp,
                 wo_ref, w1_ref, w2_ref, out_ref):
    o = (_dot(oa[...], wo_ref[0]) + _dot(ob[...], wo_ref[1])
         + _dot(oc[...], wo_ref[2]) + _dot(od[...], wo_ref[3]))
    x = x_ref[...] + mod_ref[2] * _rms(o, g_post_mix[...])
    hb = (_rms(x, g_pre_mlp[...]) * (1.0 + mod_ref[4]) + mod_ref[3]).astype(BF16)
    acc = jnp.zeros(x.shape, F32)
    fc = 1024
    for c in range(D_FF // fc):
        a = _dot(hb, w1_ref[:, c * fc:(c + 1) * fc])
        a = jnp.square(jnp.maximum(a, 0.0)).astype(BF16)
        acc = acc + _dot(a, w2_ref[c * fc:(c + 1) * fc, :])
    out_ref[...] = x + mod_ref[5] * _rms(acc, g_post_mlp[...])


def _outmlp(o_parts, x, mod, mod_spec, l, g_post_mix, g_pre_mlp, g_post_mlp, wo, w1, w2):
    t = x.shape[0]
    tm = TOKEN_TILE
    part_spec = pl.BlockSpec((tm, GROUP_W), lambda i: (i, 0))
    return pl.pallas_call(
        _outmlp_body,
        out_shape=jax.ShapeDtypeStruct((t, D_MODEL), F32),
        grid=(t // tm,),
        in_specs=[part_spec] * 4 + [
            pl.BlockSpec((tm, D_MODEL), lambda i: (i, 0)),
            mod_spec,
        ] + [_layer_spec(a, l) for a in (g_post_mix, g_pre_mlp, g_post_mlp, wo, w1, w2)],
        out_specs=pl.BlockSpec((tm, D_MODEL), lambda i: (i, 0)),
        compiler_params=_params(("arbitrary",)),
        name="outproj_mlp",
    )(*o_parts, x, mod, g_post_mix, g_pre_mlp, g_post_mlp, wo, w1, w2)


def _sum_lane(h):
    return ((h + 1) % HEADS) * HEAD_DIM


def _rows(ref, j, n):
    return ref.at[pl.ds(j * n, n), :]


def _mla_body(*refs, seq, n_ctx, rope, emit_ckv, row_chunk, sub):
    it = iter(refs)
    q_ref, kv_ref = next(it), next(it)
    ctx_ref = next(it) if n_ctx else None
    g_ref, w_ref = next(it), next(it)
    tabs = tuple(next(it) for _ in range(4)) if rope else None
    o_ref = next(it)
    ckv_out = next(it) if emit_ckv else None
    k_s, v_s = next(it), next(it)
    tq = q_ref.shape[0] // sub
    for j in range(sub):
        _mla_one(_rows(q_ref, j, tq), _rows(kv_ref, j, seq), ctx_ref, g_ref, w_ref, tabs, _rows(o_ref, j, tq),
                 _rows(ckv_out, j, seq) if emit_ckv else None, k_s.at[j], v_s.at[j],
                 seq=seq, n_ctx=n_ctx, rope=rope, emit_ckv=emit_ckv, row_chunk=row_chunk)


def _mla_one(q_ref, kv_ref, ctx_ref, g_ref, w_ref, tabs, o_ref, ckv_out, k_s, v_s, *,
             seq, n_ctx, rope, emit_ckv, row_chunk):
    if rope:
        cq_ref, sq_ref, ck_ref, sk_ref = tabs

    def expand(kin, r0, n):
        kv = _dot(kin.astype(BF16), w_ref[...])
        k_s[r0:r0 + n, :] = kv[:, :HEADS * 128].astype(BF16)
        for h in range(HEADS):
            c0 = HEADS * 128 + h * GROUP_W
            v_s[h, r0:r0 + n, :] = kv[:, c0:c0 + GROUP_W].astype(BF16)

    @pl.when(pl.program_id(1) == 0)
    def _prep():
        for r0 in range(0, seq, row_chunk):
            blk = kv_ref[r0:r0 + row_chunk, :]
            cn = _rms(blk[:, :MLA_KV_RANK], g_ref[...])
            pe = blk[:, MLA_KV_RANK:]
            if rope:
                pe = _rope(pe, ck_ref[r0:r0 + row_chunk, :], sk_ref[r0:r0 + row_chunk, :])
            if emit_ckv:
                ckv_out[r0:r0 + row_chunk, :] = cn
            pe = jnp.where(lax.broadcasted_iota(jnp.int32, pe.shape, 1) == 127, 1.0, pe)
            expand(jnp.concatenate([cn, pe], axis=1), r0, row_chunk)
        if n_ctx:
            expand(ctx_ref[...], seq, n_ctx)

    scale = MLA_QK ** -0.5
    lane_head = lax.broadcasted_iota(jnp.int32, o_ref.shape, 1) // HEAD_DIM
    acc = jnp.zeros(o_ref.shape, F32)
    for h in range(HEADS):
        qh = q_ref[:, h * 128:(h + 1) * 128]
        if rope:
            qh = _rope(qh, cq_ref[...], sq_ref[...])
        s = _dot_g(qh * scale, k_s[:, h * 128:(h + 1) * 128], _NT)
        p = jnp.exp(s - jnp.max(s, axis=-1, keepdims=True))
        pv = _dot(p, v_s[h])
        inv = 1.0 / pv[:, _sum_lane(h):_sum_lane(h) + 1]
        acc = acc + jnp.where(lane_head == h, pv * inv, 0.0)
    o_ref[...] = acc.astype(BF16)


def _mla(proj, n_batch, seq, l, g, w, ctx=None, tables=None, emit_ckv=False):
    tq = min(seq, 1024)
    nq = seq // tq
    sub = 4 if nq == 1 else 1
    n_ctx = 0 if ctx is None else ctx.shape[2]
    rope = tables is not None
    lk = seq + n_ctx
    in_specs = [
        pl.BlockSpec((sub * tq, HEADS * 128), lambda b, i: (b * nq + i, 0)),
        pl.BlockSpec((sub * seq, 256), lambda b, i: (b, 2)),
    ]
    args = [proj, proj]
    if n_ctx:
        in_specs.append(pl.BlockSpec((None, None, n_ctx, 256), lambda b, i: (b, l, 0, 0)))
        args.append(ctx)
    in_specs += [_layer_spec(g, l), _layer_spec(w, l)]
    args += [g, w]
    if rope:
        in_specs += [pl.BlockSpec((tq, 128), lambda b, i: (i, 0))] * 2 + [_const_spec((seq, 128))] * 2
        args += list(tables)
    out_shape = [jax.ShapeDtypeStruct((n_batch * seq, GROUP_W), BF16)]
    out_specs = [pl.BlockSpec((sub * tq, GROUP_W), lambda b, i: (b * nq + i, 0))]
    if emit_ckv:
        out_shape.append(jax.ShapeDtypeStruct((n_batch * seq, MLA_KV_RANK), F32))
        out_specs.append(pl.BlockSpec((sub * seq, MLA_KV_RANK), lambda b, i: (b, 0)))
    body = functools.partial(_mla_body, seq=seq, n_ctx=n_ctx, rope=rope, emit_ckv=emit_ckv,
                             row_chunk=min(seq, 512), sub=sub)
    return pl.pallas_call(
        body, out_shape=out_shape, grid=(n_batch // sub, nq), in_specs=in_specs, out_specs=out_specs,
        scratch_shapes=[pltpu.VMEM((sub, lk, HEADS * 128), BF16), pltpu.VMEM((sub, HEADS, lk, GROUP_W), BF16)],
        compiler_params=_params(("arbitrary", "arbitrary")),
        name="mla_latent" if rope else "mla_context",
    )(*args)


def _swa_body(*refs, seq, n_ctx, local, row_chunk, sub):
    it = iter(refs)
    q_ref, kv_ref = next(it), next(it)
    ctx_refs = (next(it), next(it)) if n_ctx else None
    consts = (next(it), next(it), next(it))
    tabs = tuple(next(it) for _ in range(4)) if local else None
    o_ref = next(it)
    k_s, v_s = next(it), next(it)
    tq = q_ref.shape[0] // sub
    for j in range(sub):
        _swa_one(_rows(q_ref, j, tq), _rows(kv_ref, j, seq), ctx_refs, consts, tabs, _rows(o_ref, j, tq),
                 k_s.at[j], v_s.at[j], seq=seq, n_ctx=n_ctx, local=local, row_chunk=row_chunk)


def _swa_one(q_ref, kv_ref, ctx_refs, consts, tabs, o_ref, k_s, v_s, *, seq, n_ctx, local, row_chunk):
    if n_ctx:
        kc_ref, vc_ref = ctx_refs
    sink_ref, pk_ref, pv_ref = consts
    if local:
        cq_ref, sq_ref, ck_ref, sk_ref = tabs
    tq = q_ref.shape[0]
    blk_id = pl.program_id(1)

    def expand(k, v, r0, n):
        k_s[r0:r0 + n, :] = _dot(k.astype(BF16), pk_ref[...]).astype(BF16)
        vb = v.astype(BF16)
        lane = lax.broadcasted_iota(jnp.int32, (n, GROUP_W), 1)
        for h in range(HEADS):
            ve = _dot(vb, pv_ref[:, h * GROUP_W:(h + 1) * GROUP_W])
            v_s[h, r0:r0 + n, :] = jnp.where(lane == _sum_lane(h), 1.0, ve).astype(BF16)

    @pl.when(blk_id == 0)
    def _prep():
        for r0 in range(0, seq, row_chunk):
            k = kv_ref[r0:r0 + row_chunk, 0:128]
            v = kv_ref[r0:r0 + row_chunk, 128:256]
            if local:
                k = _rope(k, ck_ref[r0:r0 + row_chunk, :], sk_ref[r0:r0 + row_chunk, :])
            expand(k, v, r0, row_chunk)
        if n_ctx:
            expand(kc_ref[...], vc_ref[...], seq, n_ctx)

    scale = HEAD_DIM ** -0.5
    q = q_ref[...]
    if local:
        q = _rope(q, cq_ref[...], sq_ref[...])
        win = tq + 2 * SWA_WINDOW
        start = jnp.clip(blk_id * tq - SWA_WINDOW, 0, seq - win)
        start = pl.multiple_of(start, SWA_WINDOW)
        qpos = blk_id * tq + lax.broadcasted_iota(jnp.int32, (tq, win), 0)
        kpos = start + lax.broadcasted_iota(jnp.int32, (tq, win), 1)
        valid = jnp.abs(qpos - kpos) <= SWA_WINDOW
    q = q * scale
    lane_head = lax.broadcasted_iota(jnp.int32, q.shape, 1) // HEAD_DIM
    acc = jnp.zeros(o_ref.shape, F32)
    for h in range(HEADS):
        qm = jnp.where(lane_head == h, q, 0.0).astype(BF16)
        sink = sink_ref[:, h:h + 1]
        if local:
            s_loc = jnp.where(valid, _dot_g(qm, k_s[pl.ds(start, win), :], _NT), -jnp.inf)
            s_ctx = _dot_g(qm, k_s[seq:seq + n_ctx, :], _NT)
            m = jnp.maximum(jnp.maximum(jnp.max(s_loc, axis=-1, keepdims=True),
                                        jnp.max(s_ctx, axis=-1, keepdims=True)), sink)
            o = (_dot(jnp.exp(s_loc - m), v_s[h, pl.ds(start, win), :])
                 + _dot(jnp.exp(s_ctx - m), v_s[h, seq:seq + n_ctx, :]))
        else:
            s = _dot_g(qm, k_s[...], _NT)
            m = jnp.maximum(jnp.max(s, axis=-1, keepdims=True), sink)
            o = _dot(jnp.exp(s - m), v_s[h])
        den = o[:, _sum_lane(h):_sum_lane(h) + 1] + jnp.exp(sink - m)
        acc = acc + jnp.where(lane_head == h, o * (1.0 / den), 0.0)
    o_ref[...] = acc.astype(BF16)


def _swa(proj, n_batch, seq, l, sink, pk, pv, ctx_kv=None, tables=None):
    local = tables is not None
    tq = 2 * SWA_WINDOW if local else seq
    nq = seq // tq
    sub = 4 if nq == 1 else 1
    n_ctx = 0 if ctx_kv is None else ctx_kv[0].shape[2]
    lk = seq + n_ctx
    in_specs = [
        pl.BlockSpec((sub * tq, GROUP_W), lambda b, i: (b * nq + i, 0)),
        pl.BlockSpec((sub * seq, 256), lambda b, i: (b, 1)),
    ]
    args = [proj, proj]
    if n_ctx:
        in_specs += [pl.BlockSpec((None, None, n_ctx, 128), lambda b, i: (b, l, 0, 0))] * 2
        args += list(ctx_kv)
    in_specs += [_layer_spec(sink, l), _const_spec(pk.shape), _const_spec(pv.shape)]
    args += [sink, pk, pv]
    if local:
        in_specs += [pl.BlockSpec((tq, GROUP_W), lambda b, i: (i, 0))] * 2 + [_const_spec((seq, 128))] * 2
        args += list(tables)
    body = functools.partial(_swa_body, seq=seq, n_ctx=n_ctx, local=local, row_chunk=min(seq, 512), sub=sub)
    return pl.pallas_call(
        body,
        out_shape=jax.ShapeDtypeStruct((n_batch * seq, GROUP_W), BF16),
        grid=(n_batch // sub, nq), in_specs=in_specs,
        out_specs=pl.BlockSpec((sub * tq, GROUP_W), lambda b, i: (b * nq + i, 0)),
        scratch_shapes=[pltpu.VMEM((sub, lk, GROUP_W), BF16), pltpu.VMEM((sub, HEADS, lk, GROUP_W), BF16)],
        compiler_params=_params(("arbitrary", "arbitrary")),
        name="swa_latent" if local else "swa_context",
    )(*args)


def _conv3(x, w_ref):
    n = x.shape[0]
    row = lax.broadcasted_iota(jnp.int32, x.shape, 0)
    prev = jnp.where(row == 0, 0.0, pltpu.roll(x, 1, axis=0))
    nxt = jnp.where(row == n - 1, 0.0, pltpu.roll(x, n - 1, axis=0))
    return prev * w_ref[0:1, :] + x * w_ref[1:2, :] + nxt * w_ref[2:3, :]


def _gdn_group(seq):
    n_pair = seq // (2 * GDN_CHUNK)
    return n_pair if n_pair <= GDN_GROUP else 1


def _gdn_body(*refs, seq, has_state, emit_state, sub):
    it = iter(refs)
    x_ref, gt_ref = next(it), next(it)
    s0_ref = next(it) if has_state else None
    consts = tuple(next(it) for _ in range(7))
    o_ref = next(it)
    st_ref = next(it) if emit_state else None
    scratch = tuple(it)
    for j in range(sub):
        _gdn_one(_rows(x_ref, j, seq), gt_ref.at[:, pl.ds(j * seq, seq)], s0_ref, consts, _rows(o_ref, j, seq),
                 st_ref.at[j] if emit_state else None, tuple(s.at[j] for s in scratch),
                 seq=seq, has_state=has_state, emit_state=emit_state)


def _gdn_one(x_ref, gt_ref, s0_ref, consts, o_ref, st_ref, scratch, *, seq, has_state, emit_state):
    cw_ref, gpar_ref, gpart_ref, gnorm_ref, ones_ref, tri_ref, trit_ref = consts
    q_s, k_s, v_s, of_s, ob_s, dec_s, dect_s, gate_s, s_s = scratch[:9]
    st_uv, st_w, st_qd, st_attn = scratch[9:]
    pair = 2 * GDN_CHUNK
    n_pair = seq // pair
    grp = _gdn_group(seq)

    def conv_act(g):
        cols = slice(g * GROUP_W, (g + 1) * GROUP_W)
        return _silu(_conv3(x_ref[:, cols], cw_ref.at[:, cols]))

    def head_sums(x):
        return _dot(jnp.concatenate(_split_bf16(x, 2), axis=1), ones_ref[...])

    def head_l2(x):
        return x * lax.rsqrt(head_sums(x * x) + EPS)

    def to_heads(dst, x):
        for h in range(HEADS):
            dst[h] = x[:, h * HEAD_DIM:(h + 1) * HEAD_DIM]

    to_heads(q_s, head_l2(conv_act(0)) * (HEAD_DIM ** -0.5))
    to_heads(k_s, head_l2(conv_act(1)))
    to_heads(v_s, conv_act(2))
    gab = x_ref[:, 4 * GROUP_W:4 * GROUP_W + 128]
    glog = -jnp.exp(gpar_ref[0:1, :]) * jax.nn.softplus(gab + gpar_ref[1:2, :])
    gate_s[...] = jax.nn.sigmoid(gab)
    lane = lax.broadcasted_iota(jnp.int32, (pair, 128), 1)
    glogt = -jnp.exp(gpart_ref[:, 0:1]) * jax.nn.softplus(gt_ref[...] + gpart_ref[:, 1:2])
    rowi = lax.broadcasted_iota(jnp.int32, (16, pair), 0)
    g3 = _split_bf16(glog, 3)
    gt3 = _split_bf16(glogt, 3)
    for p in range(n_pair):
        r = slice(p * pair, (p + 1) * pair)
        gcol = jnp.concatenate([g[r, :] for g in g3], axis=0)
        grow = jnp.concatenate([g[:, r] for g in gt3], axis=1)
        dec_s[r, :] = jnp.where(lane < HEADS, _dot(tri_ref[0], gcol), _dot(tri_ref[1], gcol))
        dect_s[:, r] = jnp.where(rowi < HEADS, _dot(grow, trit_ref[1]), _dot(grow, trit_ref[0]))

    if has_state:
        s_s[...] = s0_ref[...]
    else:
        s_s[...] = jnp.zeros(s_s.shape, F32)

    ri = lax.broadcasted_iota(jnp.int32, (pair, pair), 0)
    ci = lax.broadcasted_iota(jnp.int32, (pair, pair), 1)
    same = (ri // GDN_CHUNK) == (ci // GDN_CHUNK)
    nb = 2 * HEADS
    incl = (same & (ri >= ci), same & (ri <= ci))
    strict = (same & (ri > ci), same & (ri < ci))
    eye = (ri == ci).astype(F32)
    merge = [((ri // (2 * s)) == (ci // (2 * s))) & ((ri // s) != (ci // s))
             for s in (1, 2, 4, 8, 16, 32)]
    c = GDN_CHUNK

    def first_second(x):
        return (jnp.concatenate([x[:HEADS, :c], x[HEADS:, c:]], axis=0),
                jnp.concatenate([x[:HEADS, c:], x[HEADS:, :c]], axis=0))

    def row_order(first, second):
        return jnp.concatenate([jnp.concatenate([first[:HEADS], second[:HEADS]], axis=1),
                                jnp.concatenate([second[HEADS:], first[HEADS:]], axis=1)], axis=0)

    def rows_of(i):
        fwd, bwd = i * pair, (n_pair - 1 - i) * pair
        if not isinstance(i, int):
            fwd, bwd = pl.multiple_of(fwd, pair), pl.multiple_of(bwd, pair)
        return pl.ds(fwd, pair), pl.ds(bwd, pair)

    def solve_stage(steps, filler=iter(())):
        ab, t, rhs_v, rhs_w, dst = [], [], [], [], []
        for i, d in [(i, d) for i in steps for d in range(2)]:
            slot = i % (2 * grp)
            r = rows_of(i)[d]
            gd_, dd_, td_ = gate_s[r, :], dec_s[r, :], dect_s[:, r]
            for h in range(HEADS):
                p, col = d * HEADS + h, d * HEADS + h
                dst.append((slot, p))
                q, k, v = q_s[h, r, :], k_s[h, r, :], v_s[h, r, :]
                beta = gd_[:, 8 + col:9 + col]
                dcol = dd_[:, col:col + 1]
                gam = jnp.where(incl[d], jnp.exp(jnp.where(incl[d], dcol - td_[col:col + 1, :], 0.0)), 0.0)
                kb = k * beta
                kk_qk = _dot_g(jnp.concatenate([kb, q], axis=0), k, _NT)
                a = jnp.where(strict[d], kk_qk[:pair] * gam, 0.0)
                edec = jnp.exp(dcol)
                st_attn[slot, p] = (kk_qk[pair:] * gam).astype(BF16)
                st_qd[slot, p] = (q * edec).astype(BF16)
                ab.append(a.astype(BF16))
                t.append((eye - jnp.where(merge[0], a, 0.0)).astype(BF16))
                rhs_v.append(v * beta)
                rhs_w.append(kb * edec)
        n_prob = len(dst)
        for m in merge[1:]:
            for _ in steps:
                next(filler, None)
            x = [_dot(t[j], jnp.where(m, ab[j], 0.0)) for j in range(n_prob)]
            x = [_dot(x[j], t[j]) for j in range(n_prob)]
            t = [t[j] - x[j].astype(BF16) for j in range(n_prob)]
        for j, (slot, p) in enumerate(dst):
            st_uv[slot, p] = _dot(t[j], rhs_v[j])
            st_w[slot, p] = _dot(t[j], rhs_w[j]).astype(BF16)
        for _ in filler:
            pass

    def state_steps(i):
        rf, rb = rows_of(i)
        slot = i % (2 * grp)
        uv1, uv2 = first_second(st_uv[slot])
        w1, w2 = first_second(st_w[slot])
        qd1, qd2 = first_second(st_qd[slot])
        k1, k2 = first_second(jnp.stack([k_s[h, r, :] for r in (rf, rb) for h in range(HEADS)]))
        dec_f, dec_b = dec_s[rf, :], dec_s[rb, :]
        d1, d2 = first_second(jnp.stack([dec_f[:, j:j + 1] for j in range(HEADS)]
                                        + [dec_b[:, HEADS + j:HEADS + j + 1] for j in range(HEADS)]))
        dl1 = jnp.concatenate([d1[:HEADS, c - 1:c], d1[HEADS:, 0:1]], axis=0)
        dl2 = jnp.concatenate([d2[:HEADS, c - 1:c], d2[HEADS:, 0:1]], axis=0)
        s = s_s[...].reshape(nb, HEAD_DIM, HEAD_DIM)
        u1 = uv1 - _bmm(w1, s, _B_NN)
        o1 = _bmm(qd1, s, _B_NN)
        yield
        s = s * jnp.exp(dl1) + _bmm(k1 * jnp.exp(dl1 - d1), u1, _B_TN)
        yield
        u2 = uv2 - _bmm(w2, s, _B_NN)
        o2 = _bmm(qd2, s, _B_NN)
        yield
        s = s * jnp.exp(dl2) + _bmm(k2 * jnp.exp(dl2 - d2), u2, _B_TN)
        s_s[...] = s.reshape(2, HEADS, HEAD_DIM, HEAD_DIM)
        yield
        o = row_order(o1, o2) + _bmm(st_attn[slot], row_order(u1, u2), _B_NN)
        for j in range(HEADS // 2):
            of_s[j, rf, :] = jnp.concatenate([o[2 * j], o[2 * j + 1]], axis=1)
            ob_s[j, rb, :] = jnp.concatenate([o[HEADS + 2 * j], o[HEADS + 2 * j + 1]], axis=1)

    def group(g):
        return tuple(g * grp + j for j in range(grp))

    def state_chain(g):
        for i in group(g):
            yield from state_steps(i)

    def loop_body(g, carry):
        solve_stage(group(g + 1), state_chain(g))
        return carry

    n_group = n_pair // grp
    solve_stage(group(0))
    if n_group > 1:
        lax.fori_loop(0, n_group - 1, loop_body, 0)
    for _ in state_chain(n_group - 1):
        pass

    o = jnp.concatenate([of_s[j] + ob_s[j] for j in range(HEADS // 2)], axis=1)
    ms = head_sums(o * o) * (1.0 / HEAD_DIM)
    o = o * lax.rsqrt(ms + EPS) * gnorm_ref[...]
    o_ref[...] = (o * _silu(x_ref[:, 3 * GROUP_W:4 * GROUP_W])).astype(BF16)
    if emit_state:
        st_ref[...] = s_s[...]


def _gdn(proj, gate_t, n_batch, seq, l, layer_params, tables, state=None, emit_state=False):
    has_state = state is not None
    sub = 1 if has_state else 2
    in_specs = [pl.BlockSpec((sub * seq, W_GDN), lambda b: (b, 0), pipeline_mode=pl.Buffered(2 if sub > 1 else 1)),
                pl.BlockSpec((16, sub * seq), lambda b: (0, b))]
    args = [proj, gate_t]
    state_block = (sub, 2, HEADS, HEAD_DIM, HEAD_DIM)
    if has_state:
        in_specs.append(pl.BlockSpec((None, None) + state_block[1:], lambda b: (b, l, 0, 0, 0, 0)))
        args.append(state)
    for p in layer_params:
        in_specs.append(_layer_spec(p, l))
        args.append(p)
    for c in tables:
        in_specs.append(_const_spec(c.shape))
        args.append(c)
    out_shape = [jax.ShapeDtypeStruct((n_batch * seq, GROUP_W), BF16)]
    out_specs = [pl.BlockSpec((sub * seq, GROUP_W), lambda b: (b, 0))]
    if emit_state:
        out_shape.append(jax.ShapeDtypeStruct((n_batch, 2, HEADS, HEAD_DIM, HEAD_DIM), F32))
        out_specs.append(pl.BlockSpec(state_block, lambda b: (b, 0, 0, 0, 0)))
    vmem = lambda shape, dtype: pltpu.VMEM((sub,) + shape, dtype)
    scratch = [vmem((HEADS, seq, HEAD_DIM), F32)] * 3 + [vmem((HEADS // 2, seq, 128), F32)] * 2 + [
        vmem((seq, 128), F32), vmem((16, seq), F32), vmem((seq, 128), F32),
        vmem((2, HEADS, HEAD_DIM, HEAD_DIM), F32)]
    nb, pair, ns = 2 * HEADS, 2 * GDN_CHUNK, 2 * _gdn_group(seq)
    scratch += [vmem((ns, nb, pair, HEAD_DIM), F32), vmem((ns, nb, pair, HEAD_DIM), BF16),
                vmem((ns, nb, pair, HEAD_DIM), BF16), vmem((ns, nb, pair, pair), BF16)]
    body = functools.partial(_gdn_body, seq=seq, has_state=has_state, emit_state=emit_state, sub=sub)
    return pl.pallas_call(
        body, out_shape=out_shape, grid=(n_batch // sub,), in_specs=in_specs, out_specs=out_specs,
        scratch_shapes=scratch, compiler_params=_params(("arbitrary",)),
        name="gdn_latent" if has_state else "gdn_context",
    )(*args)


def _hy_filter_body(feat_ref, dist_ref, w1_ref, b1_ref, w2_ref, b2_ref, w3_ref, freq_ref, decay_ref,
                    cf_ref, sf_ref, rot_ref, hr_ref, hi_ref, nyq_ref, *, seq):
    h = jnp.sin(freq_ref[0:1, :] * (_dot(feat_ref[...], w1_ref[...], HIGHEST) + b1_ref[...]))
    h = jnp.sin(freq_ref[1:2, :] * (_dot(h, w2_ref[...], HIGHEST) + b2_ref[...]))
    h = _dot(h, w3_ref[...], HIGHEST)
    filt = h * jnp.exp(-dist_ref[...] * jnp.abs(decay_ref[...]))
    kf = cf_ref.shape[0]
    t = lax.broadcasted_iota(jnp.int32, (seq, 1), 0)
    alt = jnp.where((t & 1) == 0, 1.0, -1.0)
    nyq_ref[...] = jnp.sum(filt * alt, axis=0, keepdims=True) * (1.0 / (2 * kf))
    fb = filt.astype(BF16)
    tk = min(kf, 512)
    for k0 in range(0, kf, tk):
        hr = _dot(cf_ref[k0:k0 + tk, :], fb)
        hs = _dot(sf_ref[k0:k0 + tk, :], fb)
        wc, ws = rot_ref[k0:k0 + tk, 0:1], rot_ref[k0:k0 + tk, 1:2]
        hr_ref[k0:k0 + tk, :] = hr * wc + hs * ws
        hi_ref[k0:k0 + tk, :] = hr * ws - hs * wc


def _hy_filter(seq, feats, dist, layer_params, dft):
    cf, sf, _, _, rot = dft
    kf = cf.shape[0]
    per_layer = lambda a: pl.BlockSpec((None,) + a.shape[1:], lambda l: (l,) + (0,) * (a.ndim - 1))
    out_block = lambda rows: pl.BlockSpec((None, rows, 2 * GROUP_W), lambda l: (l, 0, 0))
    return pl.pallas_call(
        functools.partial(_hy_filter_body, seq=seq),
        out_shape=[jax.ShapeDtypeStruct((DEPTH, kf, 2 * GROUP_W), F32),
                   jax.ShapeDtypeStruct((DEPTH, kf, 2 * GROUP_W), F32),
                   jax.ShapeDtypeStruct((DEPTH, 1, 2 * GROUP_W), F32)],
        grid=(DEPTH,),
        in_specs=[_const_spec(feats.shape), _const_spec(dist.shape)] + [per_layer(a) for a in layer_params]
        + [_const_spec(cf.shape), _const_spec(sf.shape), _const_spec(rot.shape)],
        out_specs=[out_block(kf), out_block(kf), out_block(1)],
        compiler_params=_params(("arbitrary",)),
        name="hyena_filter",
    )(feats, dist, *layer_params, cf, sf, rot)


def _hy_conv_body(x_ref, cw_ref, bias_ref, cf_ref, sf_ref, cft_ref, sft_ref, hr_ref, hi_ref, nyq_ref, o_ref,
                  yr_s, yi_s, *, seq, sub):
    for j in range(sub):
        _hy_conv_one(_rows(x_ref, j, seq), cw_ref, bias_ref, cf_ref, sf_ref, cft_ref, sft_ref, hr_ref, hi_ref,
                     nyq_ref, _rows(o_ref, j, seq), yr_s.at[j], yi_s.at[j], seq=seq)


def _hy_conv_one(x_ref, cw_ref, bias_ref, cf_ref, sf_ref, cft_ref, sft_ref, hr_ref, hi_ref, nyq_ref, o_ref,
                 yr_s, yi_s, *, seq):
    t = lax.broadcasted_iota(jnp.int32, (seq, 1), 0)
    alt = jnp.where((t & 1) == 0, 1.0, -1.0)
    kf = cf_ref.shape[0]
    tk = min(kf, 512)

    def long_conv(a, o):
        cols = slice(o * GROUP_W, (o + 1) * GROUP_W)
        ab = a.astype(BF16)
        for k0 in range(0, kf, tk):
            ur = _dot(cf_ref[k0:k0 + tk, :], ab)
            us = _dot(sf_ref[k0:k0 + tk, :], ab)
            hr = hr_ref[k0:k0 + tk, cols]
            hi = hi_ref[k0:k0 + tk, cols]
            yr_s[k0:k0 + tk, :] = (ur * hr + us * hi).astype(BF16)
            yi_s[k0:k0 + tk, :] = (ur * hi - us * hr).astype(BF16)
        u_nyq = jnp.sum(a * alt, axis=0, keepdims=True)
        y = _dot(cft_ref[...], yr_s[...]) - _dot(sft_ref[...], yi_s[...])
        return y + alt * (u_nyq * nyq_ref[:, cols])

    v = _conv3(x_ref[:, 0:GROUP_W], cw_ref.at[:, 0:GROUP_W])
    x1 = _conv3(x_ref[:, GROUP_W:2 * GROUP_W], cw_ref.at[:, GROUP_W:2 * GROUP_W])
    z = x1 * (long_conv(v, 0) + v * bias_ref[0:1, :])
    x2 = _conv3(x_ref[:, 2 * GROUP_W:3 * GROUP_W], cw_ref.at[:, 2 * GROUP_W:3 * GROUP_W])
    o_ref[...] = (x2 * (long_conv(z, 1) + z * bias_ref[1:2, :])).astype(BF16)


def _hy_conv(proj, n_batch, seq, l, cw, bias, dft, hr, hi, nyq):
    cf, sf, cft, sft, _ = dft
    kf = cf.shape[0]
    sub = 4 if seq <= 512 else 1
    return pl.pallas_call(
        functools.partial(_hy_conv_body, seq=seq, sub=sub),
        out_shape=jax.ShapeDtypeStruct((n_batch * seq, GROUP_W), BF16),
        grid=(n_batch // sub,),
        in_specs=[pl.BlockSpec((sub * seq, W_HY), lambda b: (b, 0), pipeline_mode=pl.Buffered(2 if sub > 1 else 1)),
                  _layer_spec(cw, l), _layer_spec(bias, l)] + [_const_spec(a.shape) for a in (cf, sf, cft, sft)]
        + [_layer_spec(hr, l), _layer_spec(hi, l), _layer_spec(nyq, l)],
        out_specs=pl.BlockSpec((sub * seq, GROUP_W), lambda b: (b, 0)),
        scratch_shapes=[pltpu.VMEM((sub, kf, GROUP_W), BF16)] * 2,
        compiler_params=_params(("arbitrary",)),
        name="hyena_conv",
    )(proj, cw, bias, cf, sf, cft, sft, hr, hi, nyq)


def _rope_tables(seq):
    rows = seq // GRID_W
    row = np.repeat(np.arange(rows), GRID_W).astype(np.float64)
    col = np.tile(np.arange(GRID_W), rows).astype(np.float64)

    def pair_tables(dim):
        n_freq = dim // 4
        inv = (ROPE_BASE ** (-np.arange(n_freq, dtype=np.float32) / n_freq)).astype(np.float64)
        ang = np.concatenate([row[:, None] * inv, col[:, None] * inv], axis=-1).astype(np.float32)
        cos = np.repeat(np.cos(ang), 2, axis=-1)
        sin = np.repeat(np.sin(ang), 2, axis=-1) * np.tile(np.array([-1.0, 1.0], np.float32), dim // 2)
        return cos, sin

    ca, sa = pair_tables(MLA_ROPE)
    one = lambda n: np.ones((seq, n), np.float32)
    zero = lambda n: np.zeros((seq, n), np.float32)
    mla = (np.concatenate([one(MLA_NOPE), ca, one(32)], 1), np.concatenate([zero(MLA_NOPE), sa, zero(32)], 1),
           np.concatenate([ca, one(96)], 1), np.concatenate([sa, zero(96)], 1))
    cb, sb = pair_tables(HEAD_DIM)
    swa = (np.tile(cb, (1, HEADS)), np.tile(sb, (1, HEADS)),
           np.tile(cb, (1, SWA_KV_HEADS)), np.tile(sb, (1, SWA_KV_HEADS)))
    as_f32 = lambda ts: tuple(jnp.asarray(t, F32) for t in ts)
    return as_f32(mla), as_f32(swa)


def _dft_tables(seq):
    n = 3 * seq // 2 if (3 * seq // 4) % 256 == 0 else 2 * seq
    kf = n // 2
    k = np.arange(kf, dtype=np.int64)
    s = np.arange(seq, dtype=np.int64)
    ang = ((k[:, None] * s[None, :]) % n).astype(np.float64) * (2.0 * math.pi / n)
    cos, sin = np.cos(ang), np.sin(ang)
    theta = ((k * (seq // 2)) % n).astype(np.float64) * (2.0 * math.pi / n)
    wgt = np.where(k == 0, 1.0 / n, 2.0 / n)
    rot = np.zeros((kf, 128), np.float64)
    rot[:, 0], rot[:, 1] = wgt * np.cos(theta), wgt * np.sin(theta)
    bf = lambda a: jnp.asarray(a, F32).astype(BF16)
    return bf(cos), bf(sin), bf(cos.T), bf(sin.T), jnp.asarray(rot, F32)


def _hy_features(seq):
    t = np.arange(seq, dtype=np.float32)
    t01 = t / np.float32(max(seq - 1, 1))
    w = (np.float32(2.0 * math.pi) * t / np.float32(seq)).astype(np.float64)
    bands = np.linspace(1e-4, HY_BANDS - 1, HY_BANDS, dtype=np.float32).astype(np.float64)
    feats = np.concatenate([t01[:, None].astype(np.float64), np.cos(w[:, None] * bands), -np.sin(w[:, None] * bands)],
                           axis=-1)
    feats = np.pad(feats, ((0, 0), (0, 128 - HY_EMB)))
    dist = (np.abs(t - (seq // 2)) / np.float32(seq / 2))[:, None]
    return jnp.asarray(feats, F32), jnp.asarray(dist, F32)


def _swa_placement():
    pk = np.zeros((128, GROUP_W), np.float32)
    pv = np.zeros((128, HEADS, GROUP_W), np.float32)
    group = HEADS // SWA_KV_HEADS
    for h in range(HEADS):
        for e in range(HEAD_DIM):
            src = (h // group) * HEAD_DIM + e
            pk[src, h * HEAD_DIM + e] = 1.0
            pv[src, h, h * HEAD_DIM + e] = 1.0
    return jnp.asarray(pk, BF16), jnp.asarray(pv.reshape(128, HEADS * GROUP_W), BF16)


def _gdn_tables():
    pair = 2 * GDN_CHUNK
    i = np.arange(pair)
    same = (i[:, None] // GDN_CHUNK) == (i[None, :] // GDN_CHUNK)
    lower = (same & (i[:, None] >= i[None, :])).astype(np.float32)
    upper = (same & (i[:, None] <= i[None, :])).astype(np.float32)
    j = np.arange(GROUP_W)
    ones_bd = ((j[:, None] // HEAD_DIM) == (j[None, :] // HEAD_DIM)).astype(np.float32)
    ones2 = np.concatenate([ones_bd] * 2, axis=0)
    tri3 = np.stack([np.concatenate([m] * 3, axis=1) for m in (lower, upper)])
    trit3 = np.stack([np.concatenate([m] * 3, axis=0) for m in (lower, upper)])
    return jnp.asarray(ones2, BF16), jnp.asarray(tri3, BF16), jnp.asarray(trit3, BF16)


def _pad_last(x, n):
    return jnp.pad(x, [(0, 0)] * (x.ndim - 1) + [(0, n - x.shape[-1])])


def _layout_w_in(w):
    dep, d, _ = w.shape
    mq = _pad_last(w[..., :384].reshape(dep, d, HEADS, MLA_QK), 128).reshape(dep, d, HEADS * 128)
    mla = _pad_last(jnp.concatenate([mq, w[..., 384:544]], axis=-1), W_MLA)
    gdn = _pad_last(w[..., 1056:2096], W_GDN)
    w_all = jnp.concatenate([mla, w[..., 544:1056], gdn, w[..., 2096:2864]], axis=-1).astype(BF16)
    w_gate_t = jnp.swapaxes(w[..., 2080:2096], 1, 2).astype(BF16)
    return w_all, w_gate_t


def _layout_w_ukv(w):
    dep = w.shape[0]
    w = w.reshape(dep, MLA_KV_RANK, HEADS, 128)
    wk = _pad_last(w[..., :MLA_NOPE], 128).reshape(dep, MLA_KV_RANK, HEADS * 128)
    place = np.zeros((128, HEADS, 128), np.float32)
    for j in range(MLA_ROPE):
        place[j, :, MLA_NOPE + j] = 1.0
    place = jnp.broadcast_to(jnp.asarray(place.reshape(128, HEADS * 128)), (dep, 128, HEADS * 128))
    wk = jnp.concatenate([wk, place], axis=1)
    eye = jnp.asarray(np.eye(HEADS, dtype=np.float32))
    wv = (w[..., MLA_NOPE:][:, :, :, None, :] * eye[None, None, :, :, None]).reshape(dep, MLA_KV_RANK, HEADS * GROUP_W)
    ones_route = np.zeros((128, HEADS * GROUP_W), np.float32)
    for h in range(HEADS):
        ones_route[127, h * GROUP_W + _sum_lane(h)] = 1.0
    wv = jnp.concatenate([wv, jnp.broadcast_to(jnp.asarray(ones_route), (dep,) + ones_route.shape)], axis=1)
    return jnp.concatenate([wk, wv], axis=-1).astype(BF16)


def _layer_pass(x, n_batch, seq, l, mod, mod_spec, P, mla_ctx=None, swa_ctx=None, state=None, tables=None):
    is_ctx = tables is None
    p_mla, p_swa, p_gdn, p_hy, gate_t = _inproj(x, mod, mod_spec, l, P["g_pre_mix"], P["w_in"], P["w_gate_t"])
    mla_out = _mla(p_mla, n_batch, seq, l, P["mla_kv_norm"], P["w_ukv"], ctx=mla_ctx,
                   tables=None if is_ctx else tables[0], emit_ckv=is_ctx)
    o_a = mla_out[0]
    o_b = _swa(p_swa, n_batch, seq, l, P["swa_sink"], P["swa_pk"], P["swa_pv"], ctx_kv=swa_ctx,
               tables=None if is_ctx else tables[1])
    gdn_out = _gdn(p_gdn, gate_t, n_batch, seq, l, P["gdn_params"], P["gdn_tables"], state=state, emit_state=is_ctx)
    o_c = gdn_out[0]
    o_d = _hy_conv(p_hy, n_batch, seq, l, P["hy_conv"], P["hy_bias"], *(P["hy_ctx"] if is_ctx else P["hy_lat"]))
    y = _outmlp((o_a, o_b, o_c, o_d), x, mod, mod_spec, l, P["g_post_mix"], P["g_pre_mlp"], P["g_post_mlp"],
                P["w_out"], P["mlp_w1"], P["mlp_w2"])
    if not is_ctx:
        return y, None
    kpe0 = HEADS * 128 + MLA_KV_RANK
    new = (mla_out[1].reshape(n_batch, seq, MLA_KV_RANK),
           p_mla[:, kpe0:kpe0 + MLA_ROPE].reshape(n_batch, seq, MLA_ROPE),
           p_swa[:, 256:384].reshape(n_batch, seq, SWA_KV_HEADS, HEAD_DIM),
           p_swa[:, 384:512].reshape(n_batch, seq, SWA_KV_HEADS, HEAD_DIM),
           gdn_out[1])
    return y, new


def kernel(x_prompt, x_sample, cache_mla_ckv, cache_mla_kpe, cache_swa_k, cache_swa_v, state_gdn, c, c_ctx, w_ada, b_ada, g_pre_mix, g_post_mix, g_pre_mlp, g_post_mlp, w_in, w_out, mla_kv_norm, mla_w_ukv, swa_sink, gdn_conv, gdn_a_log, gdn_dt_bias, gdn_norm, hy_conv, hy_w1, hy_b1, hy_w2, hy_b2, hy_w3, hy_freq, hy_decay, hy_bias, mlp_w1, mlp_w2):
    n_ctx_b, seq_ctx, d = x_prompt.shape
    n_lat_b, seq_lat, _ = x_sample.shape
    past = cache_mla_ckv.shape[2]

    cond8 = jnp.concatenate([c_ctx[None, :], c, jnp.zeros((8 - 1 - n_lat_b, d), F32)], axis=0)
    mod = _modulation(cond8, w_ada, b_ada).reshape(DEPTH, 8, 6, 1, d)

    w_in_all, w_gate_t = _layout_w_in(w_in)
    gates = jnp.stack([gdn_a_log.reshape(DEPTH, 8), gdn_dt_bias.reshape(DEPTH, 8)], axis=1)
    gates_t = jnp.pad(jnp.swapaxes(gates, 1, 2), ((0, 0), (0, 8), (0, 0)))
    swa_pk, swa_pv = _swa_placement()
    P = dict(
        g_pre_mix=g_pre_mix[:, None], g_post_mix=g_post_mix[:, None],
        g_pre_mlp=g_pre_mlp[:, None], g_post_mlp=g_post_mlp[:, None],
        w_in=w_in_all, w_gate_t=w_gate_t,
        w_out=w_out.reshape(DEPTH, HEADS, GROUP_W, d).astype(BF16),
        mlp_w1=mlp_w1.astype(BF16), mlp_w2=mlp_w2.astype(BF16),
        mla_kv_norm=mla_kv_norm[:, None], w_ukv=_layout_w_ukv(mla_w_ukv),
        swa_sink=_pad_last(swa_sink, 128)[:, None], swa_pk=swa_pk, swa_pv=swa_pv,
        gdn_params=(gdn_conv, _pad_last(gates, 128), _pad_last(gates_t, 128), jnp.tile(gdn_norm, (1, HEADS))[:, None]),
        gdn_tables=_gdn_tables(),
        hy_conv=hy_conv, hy_bias=hy_bias,
    )
    hy_params = (jnp.pad(hy_w1, ((0, 0), (0, 128 - HY_EMB), (0, 0))), hy_b1[:, None], hy_w2, hy_b2[:, None], hy_w3,
                 hy_freq, hy_decay[:, None])
    for name, s in (("hy_ctx", seq_ctx), ("hy_lat", seq_lat)):
        dft = _dft_tables(s)
        P[name] = (dft,) + tuple(_hy_filter(s, *_hy_features(s), hy_params, dft))

    tables = _rope_tables(seq_lat)
    mla_ctx = jnp.concatenate([cache_mla_ckv, _pad_last(cache_mla_kpe, 127),
                               jnp.ones(cache_mla_kpe.shape[:-1] + (1,), F32)], axis=-1)
    swa_ctx = (cache_swa_k.reshape(n_lat_b, DEPTH, past, 128), cache_swa_v.reshape(n_lat_b, DEPTH, past, 128))
    lat_tiles = seq_lat // TOKEN_TILE

    xp = x_prompt.reshape(n_ctx_b * seq_ctx, d)
    xs = x_sample.reshape(n_lat_b * seq_lat, d)
    news = []
    for l in range(DEPTH):
        xp, new = _layer_pass(xp, n_ctx_b, seq_ctx, l, mod, _mod_spec(l, 0, None), P)
        news.append(new)
        xs, _ = _layer_pass(xs, n_lat_b, seq_lat, l, mod, _mod_spec(l, 1, lat_tiles), P, mla_ctx=mla_ctx,
                            swa_ctx=swa_ctx, state=state_gdn, tables=tables)

    stacked = tuple(jnp.stack([news[l][i] for l in range(DEPTH)], axis=1) for i in range(5))
    return (xp.reshape(n_ctx_b, seq_ctx, d), xs.reshape(n_lat_b, seq_lat, d)) + stacked
```

```python
import functools
import math

import jax
import jax.numpy as jnp
import numpy as np
from jax import lax
from jax.experimental import pallas as pl
from jax.experimental.pallas import tpu as pltpu

F32 = jnp.float32
BF16 = jnp.bfloat16

D_MODEL = 1024
DEPTH = 2
GRID_W = 64
HEADS = 4
HEAD_DIM = 64
GROUP_W = 256
MLA_NOPE = 64
MLA_ROPE = 32
MLA_QK = 96
MLA_KV_RANK = 128
SWA_KV_HEADS = 2
SWA_WINDOW = 128
GDN_CHUNK = 64
GDN_GROUP = 2
HY_BANDS = 8
HY_EMB = 17
HY_FF = 64
D_FF = 4096
ROPE_BASE = 10000.0
EPS = 1e-6

W_MLA = 768
W_SWA = 512
W_GDN = 1152
W_HY = 768
W_ALL = W_MLA + W_SWA + W_GDN + W_HY

TOKEN_TILE = 1024
VMEM_LIMIT = 56 * 1024 * 1024

_NT = (((1,), (1,)), ((), ()))


def _params(sem):
    return pltpu.CompilerParams(dimension_semantics=sem, vmem_limit_bytes=VMEM_LIMIT)


def _const_spec(shape):
    nd = len(shape)
    return pl.BlockSpec(shape, lambda *_: (0,) * nd, pipeline_mode=pl.Buffered(1))


def _layer_spec(arr, l):
    shape = arr.shape[1:]
    return pl.BlockSpec((None,) + shape, lambda *_: (l,) + (0,) * len(shape), pipeline_mode=pl.Buffered(1))


def _dot(a, b):
    return jnp.dot(a.astype(BF16), b.astype(BF16), preferred_element_type=F32)


def _dot_g(a, b, dims):
    return lax.dot_general(a.astype(BF16), b.astype(BF16), dims, preferred_element_type=F32)


_B_NN = (((2,), (1,)), ((0,), (0,)))
_B_NT = (((2,), (2,)), ((0,), (0,)))
_B_TN = (((1,), (1,)), ((0,), (0,)))


def _bmm(a, b, dims):
    return lax.dot_general(a.astype(BF16), b.astype(BF16), dims, preferred_element_type=F32)


def _split_bf16(x, parts):
    out = []
    for _ in range(parts):
        p = x.astype(BF16)
        out.append(p)
        x = x - p.astype(F32)
    return out


def _dot3(a, b):
    a_hi, a_lo = _split_bf16(a, 2)
    b_hi, b_lo = _split_bf16(b, 2)
    return _dot(jnp.concatenate([a_hi, a_hi, a_lo], axis=1), jnp.concatenate([b_hi, b_lo, b_hi], axis=0))


def _rms(x, g):
    return x * lax.rsqrt(jnp.mean(x * x, axis=-1, keepdims=True) + EPS) * g


def _silu(x):
    return x * jax.nn.sigmoid(x)


def _swap_pairs(x):
    n = x.shape[-1]
    nxt = pltpu.roll(x, n - 1, axis=1)
    prv = pltpu.roll(x, 1, axis=1)
    lane = lax.broadcasted_iota(jnp.int32, x.shape, 1)
    return jnp.where((lane & 1) == 0, nxt, prv)


def _rope(x, cos, sin_signed):
    return x * cos + _swap_pairs(x) * sin_signed


def _mod_body(c_ref, w_ref, b_ref, o_ref):
    s = _silu(c_ref[...]).astype(BF16)
    o_ref[...] = _dot(s, w_ref[...].astype(BF16)) + b_ref[...]


def _modulation(cond8, w_ada, b_ada):
    n = 6 * D_MODEL
    tn = 1536
    return pl.pallas_call(
        _mod_body,
        out_shape=jax.ShapeDtypeStruct((DEPTH, 8, n), F32),
        grid=(DEPTH, n // tn),
        in_specs=[
            pl.BlockSpec((8, D_MODEL), lambda l, j: (0, 0)),
            pl.BlockSpec((None, D_MODEL, tn), lambda l, j: (l, 0, j)),
            pl.BlockSpec((None, 1, tn), lambda l, j: (l, 0, j)),
        ],
        out_specs=pl.BlockSpec((None, 8, tn), lambda l, j: (l, 0, j)),
        compiler_params=_params(("arbitrary", "arbitrary")),
        name="modulation",
    )(cond8, w_ada, b_ada.reshape(DEPTH, 1, n))


def _mod_spec(l, row0, tiles_per_row):
    blk = (None, None, 6, 1, D_MODEL)
    if tiles_per_row is None:
        return pl.BlockSpec(blk, lambda i: (l, row0, 0, 0, 0))
    return pl.BlockSpec(blk, lambda i: (l, row0 + i // tiles_per_row, 0, 0, 0))


def _inproj_body(x_ref, mod_ref, g_ref, w_mla, w_swa, w_gdn, w_hy, wgt_ref, o_mla, o_swa, o_gdn, o_hy, o_gate_t):
    h = _rms(x_ref[...], g_ref[...]) * (1.0 + mod_ref[1]) + mod_ref[0]
    hb = h.astype(BF16)
    for o, w_ref in ((o_mla, w_mla), (o_swa, w_swa), (o_gdn, w_gdn), (o_hy, w_hy)):
        o[...] = _dot(hb, w_ref[...])
    o_gate_t[...] = _dot_g(wgt_ref[...], hb, _NT)


def _inproj(x, mod, mod_spec, l, g, w, w_gate_t):
    t = x.shape[0]
    tm = TOKEN_TILE
    widths = (W_MLA, W_SWA, W_GDN, W_HY)
    return pl.pallas_call(
        _inproj_body,
        out_shape=[jax.ShapeDtypeStruct((t, n), F32) for n in widths] + [jax.ShapeDtypeStruct((16, t), F32)],
        grid=(t // tm,),
        in_specs=[
            pl.BlockSpec((tm, D_MODEL), lambda i: (i, 0)),
            mod_spec,
            _layer_spec(g, l)] + [_layer_spec(wg, l) for wg in w] + [_layer_spec(w_gate_t, l),
        ],
        out_specs=[pl.BlockSpec((tm, n), lambda i: (i, 0)) for n in widths]
        + [pl.BlockSpec((16, tm), lambda i: (0, i))],
        compiler_params=_params(("arbitrary",)),
        name="inproj",
    )(x, mod, g, *w, w_gate_t)


def _outmlp_body(oa, ob, oc, od, x_ref, mod_ref, g_post_mix, g_pre_mlp, g_post_mlp,
                 wo_ref, w1_ref, w2_ref, out_ref):
    o = (_dot(oa[...], wo_ref[0]) + _dot(ob[...], wo_ref[1])
         + _dot(oc[...], wo_ref[2]) + _dot(od[...], wo_ref[3]))
    x = x_ref[...] + mod_ref[2] * _rms(o, g_post_mix[...])
    hb = (_rms(x, g_pre_mlp[...]) * (1.0 + mod_ref[4]) + mod_ref[3]).astype(BF16)
    acc = jnp.zeros(x.shape, F32)
    fc = 1024
    for c in range(D_FF // fc):
        a = _dot(hb, w1_ref[:, c * fc:(c + 1) * fc])
        a = jnp.square(jnp.maximum(a, 0.0)).astype(BF16)
        acc = acc + _dot(a, w2_ref[c * fc:(c + 1) * fc, :])
    out_ref[...] = x + mod_ref[5] * _rms(acc, g_post_mlp[...])


def _outmlp(o_parts, x, mod, mod_spec, l, g_post_mix, g_pre_mlp, g_post_mlp, wo, w1, w2):
    t = x.shape[0]
    tm = TOKEN_TILE
    part_spec = pl.BlockSpec((tm, GROUP_W), lambda i: (i, 0))
    return pl.pallas_call(
        _outmlp_body,
        out_shape=jax.ShapeDtypeStruct((t, D_MODEL), F32),
        grid=(t // tm,),
        in_specs=[part_spec] * 4 + [
            pl.BlockSpec((tm, D_MODEL), lambda i: (i, 0)),
            mod_spec,
        ] + [_layer_spec(a, l) for a in (g_post_mix, g_pre_mlp, g_post_mlp, wo, w1, w2)],
        out_specs=pl.BlockSpec((tm, D_MODEL), lambda i: (i, 0)),
        compiler_params=_params(("arbitrary",)),
        name="outproj_mlp",
    )(*o_parts, x, mod, g_post_mix, g_pre_mlp, g_post_mlp, wo, w1, w2)


def _sum_lane(h):
    return ((h + 1) % HEADS) * HEAD_DIM


def _rows(ref, j, n):
    return ref.at[pl.ds(j * n, n), :]


def _mla_body(*refs, seq, n_ctx, rope, emit_ckv, row_chunk, sub):
    it = iter(refs)
    q_ref, kv_ref = next(it), next(it)
    ctx_ref = next(it) if n_ctx else None
    g_ref, w_ref = next(it), next(it)
    tabs = tuple(next(it) for _ in range(4)) if rope else None
    o_ref = next(it)
    ckv_out = next(it) if emit_ckv else None
    k_s, v_s = next(it), next(it)
    tq = q_ref.shape[0] // sub
    for j in range(sub):
        _mla_one(_rows(q_ref, j, tq), _rows(kv_ref, j, seq), ctx_ref, g_ref, w_ref, tabs, _rows(o_ref, j, tq),
                 _rows(ckv_out, j, seq) if emit_ckv else None, k_s.at[j], v_s.at[j],
                 seq=seq, n_ctx=n_ctx, rope=rope, emit_ckv=emit_ckv, row_chunk=row_chunk)


def _mla_one(q_ref, kv_ref, ctx_ref, g_ref, w_ref, tabs, o_ref, ckv_out, k_s, v_s, *,
             seq, n_ctx, rope, emit_ckv, row_chunk):
    if rope:
        cq_ref, sq_ref, ck_ref, sk_ref = tabs

    def expand(kin, r0, n):
        kv = _dot(kin.astype(BF16), w_ref[...])
        k_s[r0:r0 + n, :] = kv[:, :HEADS * 128].astype(BF16)
        for h in range(HEADS):
            c0 = HEADS * 128 + h * GROUP_W
            v_s[h, r0:r0 + n, :] = kv[:, c0:c0 + GROUP_W].astype(BF16)

    @pl.when(pl.program_id(1) == 0)
    def _prep():
        for r0 in range(0, seq, row_chunk):
            blk = kv_ref[r0:r0 + row_chunk, :]
            cn = _rms(blk[:, :MLA_KV_RANK], g_ref[...])
            pe = blk[:, MLA_KV_RANK:]
            if rope:
                pe = _rope(pe, ck_ref[r0:r0 + row_chunk, :], sk_ref[r0:r0 + row_chunk, :])
            if emit_ckv:
                ckv_out[r0:r0 + row_chunk, :] = cn
            pe = jnp.where(lax.broadcasted_iota(jnp.int32, pe.shape, 1) == 127, 1.0, pe)
            expand(jnp.concatenate([cn, pe], axis=1), r0, row_chunk)
        if n_ctx:
            expand(ctx_ref[...], seq, n_ctx)

    scale = MLA_QK ** -0.5
    lane_head = lax.broadcasted_iota(jnp.int32, o_ref.shape, 1) // HEAD_DIM
    acc = jnp.zeros(o_ref.shape, F32)
    for h in range(HEADS):
        qh = q_ref[:, h * 128:(h + 1) * 128]
        if rope:
            qh = _rope(qh, cq_ref[...], sq_ref[...])
        s = _dot_g(qh * scale, k_s[:, h * 128:(h + 1) * 128], _NT)
        p = jnp.exp(s - jnp.max(s, axis=-1, keepdims=True))
        pv = _dot(p, v_s[h])
        inv = 1.0 / pv[:, _sum_lane(h):_sum_lane(h) + 1]
        acc = acc + jnp.where(lane_head == h, pv * inv, 0.0)
    o_ref[...] = acc.astype(BF16)


def _mla(proj, n_batch, seq, l, g, w, ctx=None, tables=None, emit_ckv=False):
    tq = min(seq, 1024)
    nq = seq // tq
    sub = 4 if nq == 1 else 1
    n_ctx = 0 if ctx is None else ctx.shape[2]
    rope = tables is not None
    lk = seq + n_ctx
    in_specs = [
        pl.BlockSpec((sub * tq, HEADS * 128), lambda b, i: (b * nq + i, 0)),
        pl.BlockSpec((sub * seq, 256), lambda b, i: (b, 2)),
    ]
    args = [proj, proj]
    if n_ctx:
        in_specs.append(pl.BlockSpec((None, None, n_ctx, 256), lambda b, i: (b, l, 0, 0)))
        args.append(ctx)
    in_specs += [_layer_spec(g, l), _layer_spec(w, l)]
    args += [g, w]
    if rope:
        in_specs += [pl.BlockSpec((tq, 128), lambda b, i: (i, 0))] * 2 + [_const_spec((seq, 128))] * 2
        args += list(tables)
    out_shape = [jax.ShapeDtypeStruct((n_batch * seq, GROUP_W), BF16)]
    out_specs = [pl.BlockSpec((sub * tq, GROUP_W), lambda b, i: (b * nq + i, 0))]
    if emit_ckv:
        out_shape.append(jax.ShapeDtypeStruct((n_batch * seq, MLA_KV_RANK), F32))
        out_specs.append(pl.BlockSpec((sub * seq, MLA_KV_RANK), lambda b, i: (b, 0)))
    body = functools.partial(_mla_body, seq=seq, n_ctx=n_ctx, rope=rope, emit_ckv=emit_ckv,
                             row_chunk=min(seq, 512), sub=sub)
    return pl.pallas_call(
        body, out_shape=out_shape, grid=(n_batch // sub, nq), in_specs=in_specs, out_specs=out_specs,
        scratch_shapes=[pltpu.VMEM((sub, lk, HEADS * 128), BF16), pltpu.VMEM((sub, HEADS, lk, GROUP_W), BF16)],
        compiler_params=_params(("arbitrary", "arbitrary")),
        name="mla_latent" if rope else "mla_context",
    )(*args)


def _swa_body(*refs, seq, n_ctx, local, row_chunk, sub):
    it = iter(refs)
    q_ref, kv_ref = next(it), next(it)
    ctx_refs = (next(it), next(it)) if n_ctx else None
    consts = (next(it), next(it), next(it))
    tabs = tuple(next(it) for _ in range(4)) if local else None
    o_ref = next(it)
    k_s, v_s = next(it), next(it)
    tq = q_ref.shape[0] // sub
    for j in range(sub):
        _swa_one(_rows(q_ref, j, tq), _rows(kv_ref, j, seq), ctx_refs, consts, tabs, _rows(o_ref, j, tq),
                 k_s.at[j], v_s.at[j], seq=seq, n_ctx=n_ctx, local=local, row_chunk=row_chunk)


def _swa_one(q_ref, kv_ref, ctx_refs, consts, tabs, o_ref, k_s, v_s, *, seq, n_ctx, local, row_chunk):
    if n_ctx:
        kc_ref, vc_ref = ctx_refs
    sink_ref, pk_ref, pv_ref = consts
    if local:
        cq_ref, sq_ref, ck_ref, sk_ref = tabs
    tq = q_ref.shape[0]
    blk_id = pl.program_id(1)

    def expand(k, v, r0, n):
        k_s[r0:r0 + n, :] = _dot(k.astype(BF16), pk_ref[...]).astype(BF16)
        vb = v.astype(BF16)
        lane = lax.broadcasted_iota(jnp.int32, (n, GROUP_W), 1)
        for h in range(HEADS):
            ve = _dot(vb, pv_ref[:, h * GROUP_W:(h + 1) * GROUP_W])
            v_s[h, r0:r0 + n, :] = jnp.where(lane == _sum_lane(h), 1.0, ve).astype(BF16)

    @pl.when(blk_id == 0)
    def _prep():
        for r0 in range(0, seq, row_chunk):
            k = kv_ref[r0:r0 + row_chunk, 0:128]
            v = kv_ref[r0:r0 + row_chunk, 128:256]
            if local:
                k = _rope(k, ck_ref[r0:r0 + row_chunk, :], sk_ref[r0:r0 + row_chunk, :])
            expand(k, v, r0, row_chunk)
        if n_ctx:
            expand(kc_ref[...], vc_ref[...], seq, n_ctx)

    scale = HEAD_DIM ** -0.5
    q = q_ref[...]
    if local:
        q = _rope(q, cq_ref[...], sq_ref[...])
        win = tq + 2 * SWA_WINDOW
        start = jnp.clip(blk_id * tq - SWA_WINDOW, 0, seq - win)
        start = pl.multiple_of(start, SWA_WINDOW)
        qpos = blk_id * tq + lax.broadcasted_iota(jnp.int32, (tq, win), 0)
        kpos = start + lax.broadcasted_iota(jnp.int32, (tq, win), 1)
        valid = jnp.abs(qpos - kpos) <= SWA_WINDOW
    q = q * scale
    lane_head = lax.broadcasted_iota(jnp.int32, q.shape, 1) // HEAD_DIM
    acc = jnp.zeros(o_ref.shape, F32)
    for h in range(HEADS):
        qm = jnp.where(lane_head == h, q, 0.0).astype(BF16)
        sink = sink_ref[:, h:h + 1]
        if local:
            s_loc = jnp.where(valid, _dot_g(qm, k_s[pl.ds(start, win), :], _NT), -jnp.inf)
            s_ctx = _dot_g(qm, k_s[seq:seq + n_ctx, :], _NT)
            m = jnp.maximum(jnp.maximum(jnp.max(s_loc, axis=-1, keepdims=True),
                                        jnp.max(s_ctx, axis=-1, keepdims=True)), sink)
            o = (_dot(jnp.exp(s_loc - m), v_s[h, pl.ds(start, win), :])
                 + _dot(jnp.exp(s_ctx - m), v_s[h, seq:seq + n_ctx, :]))
        else:
            s = _dot_g(qm, k_s[...], _NT)
            m = jnp.maximum(jnp.max(s, axis=-1, keepdims=True), sink)
            o = _dot(jnp.exp(s - m), v_s[h])
        den = o[:, _sum_lane(h):_sum_lane(h) + 1] + jnp.exp(sink - m)
        acc = acc + jnp.where(lane_head == h, o * (1.0 / den), 0.0)
    o_ref[...] = acc.astype(BF16)


def _swa(proj, n_batch, seq, l, sink, pk, pv, ctx_kv=None, tables=None):
    local = tables is not None
    tq = 2 * SWA_WINDOW if local else seq
    nq = seq // tq
    sub = 4 if nq == 1 else 1
    n_ctx = 0 if ctx_kv is None else ctx_kv[0].shape[2]
    lk = seq + n_ctx
    in_specs = [
        pl.BlockSpec((sub * tq, GROUP_W), lambda b, i: (b * nq + i, 0)),
        pl.BlockSpec((sub * seq, 256), lambda b, i: (b, 1)),
    ]
    args = [proj, proj]
    if n_ctx:
        in_specs += [pl.BlockSpec((None, None, n_ctx, 128), lambda b, i: (b, l, 0, 0))] * 2
        args += list(ctx_kv)
    in_specs += [_layer_spec(sink, l), _const_spec(pk.shape), _const_spec(pv.shape)]
    args += [sink, pk, pv]
    if local:
        in_specs += [pl.BlockSpec((tq, GROUP_W), lambda b, i: (i, 0))] * 2 + [_const_spec((seq, 128))] * 2
        args += list(tables)
    body = functools.partial(_swa_body, seq=seq, n_ctx=n_ctx, local=local, row_chunk=min(seq, 512), sub=sub)
    return pl.pallas_call(
        body,
        out_shape=jax.ShapeDtypeStruct((n_batch * seq, GROUP_W), BF16),
        grid=(n_batch // sub, nq), in_specs=in_specs,
        out_specs=pl.BlockSpec((sub * tq, GROUP_W), lambda b, i: (b * nq + i, 0)),
        scratch_shapes=[pltpu.VMEM((sub, lk, GROUP_W), BF16), pltpu.VMEM((sub, HEADS, lk, GROUP_W), BF16)],
        compiler_params=_params(("arbitrary", "arbitrary")),
        name="swa_latent" if local else "swa_context",
    )(*args)


def _conv3(x, w_ref):
    n = x.shape[0]
    row = lax.broadcasted_iota(jnp.int32, x.shape, 0)
    prev = jnp.where(row == 0, 0.0, pltpu.roll(x, 1, axis=0))
    nxt = jnp.where(row == n - 1, 0.0, pltpu.roll(x, n - 1, axis=0))
    return prev * w_ref[0:1, :] + x * w_ref[1:2, :] + nxt * w_ref[2:3, :]


def _gdn_group(seq):
    n_pair = seq // (2 * GDN_CHUNK)
    return n_pair if n_pair <= GDN_GROUP else 1


def _gdn_body(*refs, seq, has_state, emit_state, sub):
    it = iter(refs)
    x_ref, gt_ref = next(it), next(it)
    s0_ref = next(it) if has_state else None
    consts = tuple(next(it) for _ in range(7))
    o_ref = next(it)
    st_ref = next(it) if emit_state else None
    scratch = tuple(it)
    for j in range(sub):
        _gdn_one(_rows(x_ref, j, seq), gt_ref.at[:, pl.ds(j * seq, seq)], s0_ref, consts, _rows(o_ref, j, seq),
                 st_ref.at[j] if emit_state else None, tuple(s.at[j] for s in scratch),
                 seq=seq, has_state=has_state, emit_state=emit_state)


def _gdn_one(x_ref, gt_ref, s0_ref, consts, o_ref, st_ref, scratch, *, seq, has_state, emit_state):
    cw_ref, gpar_ref, gpart_ref, gnorm_ref, ones_ref, tri_ref, trit_ref = consts
    q_s, k_s, v_s, of_s, ob_s, dec_s, dect_s, gate_s, s_s = scratch[:9]
    st_uv, st_w, st_qd, st_attn = scratch[9:]
    pair = 2 * GDN_CHUNK
    n_pair = seq // pair
    grp = _gdn_group(seq)

    def conv_act(g):
        cols = slice(g * GROUP_W, (g + 1) * GROUP_W)
        return _silu(_conv3(x_ref[:, cols], cw_ref.at[:, cols]))

    def head_sums(x):
        return _dot(jnp.concatenate(_split_bf16(x, 2), axis=1), ones_ref[...])

    def head_l2(x):
        return x * lax.rsqrt(head_sums(x * x) + EPS)

    def to_heads(dst, x):
        for h in range(HEADS):
            dst[h] = x[:, h * HEAD_DIM:(h + 1) * HEAD_DIM]

    to_heads(q_s, head_l2(conv_act(0)) * (HEAD_DIM ** -0.5))
    to_heads(k_s, head_l2(conv_act(1)))
    to_heads(v_s, conv_act(2))
    gab = x_ref[:, 4 * GROUP_W:4 * GROUP_W + 128]
    glog = -jnp.exp(gpar_ref[0:1, :]) * jax.nn.softplus(gab + gpar_ref[1:2, :])
    gate_s[...] = jax.nn.sigmoid(gab)
    lane = lax.broadcasted_iota(jnp.int32, (pair, 128), 1)
    glogt = -jnp.exp(gpart_ref[:, 0:1]) * jax.nn.softplus(gt_ref[...] + gpart_ref[:, 1:2])
    rowi = lax.broadcasted_iota(jnp.int32, (16, pair), 0)
    g3 = _split_bf16(glog, 3)
    gt3 = _split_bf16(glogt, 3)
    for p in range(n_pair):
        r = slice(p * pair, (p + 1) * pair)
        gcol = jnp.concatenate([g[r, :] for g in g3], axis=0)
        grow = jnp.concatenate([g[:, r] for g in gt3], axis=1)
        dec_s[r, :] = jnp.where(lane < HEADS, _dot(tri_ref[0], gcol), _dot(tri_ref[1], gcol))
        dect_s[:, r] = jnp.where(rowi < HEADS, _dot(grow, trit_ref[1]), _dot(grow, trit_ref[0]))

    if has_state:
        s_s[...] = s0_ref[...]
    else:
        s_s[...] = jnp.zeros(s_s.shape, F32)

    ri = lax.broadcasted_iota(jnp.int32, (pair, pair), 0)
    ci = lax.broadcasted_iota(jnp.int32, (pair, pair), 1)
    same = (ri // GDN_CHUNK) == (ci // GDN_CHUNK)
    nb = 2 * HEADS
    incl = (same & (ri >= ci), same & (ri <= ci))
    strict = (same & (ri > ci), same & (ri < ci))
    eye = (ri == ci).astype(F32)
    merge = [((ri // (2 * s)) == (ci // (2 * s))) & ((ri // s) != (ci // s))
             for s in (1, 2, 4, 8, 16, 32)]
    c = GDN_CHUNK

    def first_second(x):
        return (jnp.concatenate([x[:HEADS, :c], x[HEADS:, c:]], axis=0),
                jnp.concatenate([x[:HEADS, c:], x[HEADS:, :c]], axis=0))

    def row_order(first, second):
        return jnp.concatenate([jnp.concatenate([first[:HEADS], second[:HEADS]], axis=1),
                                jnp.concatenate([second[HEADS:], first[HEADS:]], axis=1)], axis=0)

    def rows_of(i):
        fwd, bwd = i * pair, (n_pair - 1 - i) * pair
        if not isinstance(i, int):
            fwd, bwd = pl.multiple_of(fwd, pair), pl.multiple_of(bwd, pair)
        return pl.ds(fwd, pair), pl.ds(bwd, pair)

    def solve_stage(steps, filler=iter(())):
        ab, t, rhs_v, rhs_w, dst = [], [], [], [], []
        for i, d in [(i, d) for i in steps for d in range(2)]:
            slot = i % (2 * grp)
            r = rows_of(i)[d]
            gd_, dd_, td_ = gate_s[r, :], dec_s[r, :], dect_s[:, r]
            for h in range(HEADS):
                p, col = d * HEADS + h, d * HEADS + h
                dst.append((slot, p))
                q, k, v = q_s[h, r, :], k_s[h, r, :], v_s[h, r, :]
                beta = gd_[:, 8 + col:9 + col]
                dcol = dd_[:, col:col + 1]
                gam = jnp.where(incl[d], jnp.exp(jnp.where(incl[d], dcol - td_[col:col + 1, :], 0.0)), 0.0)
                kb = k * beta
                kk_qk = _dot_g(jnp.concatenate([kb, q], axis=0), k, _NT)
                a = jnp.where(strict[d], kk_qk[:pair] * gam, 0.0)
                edec = jnp.exp(dcol)
                st_attn[slot, p] = (kk_qk[pair:] * gam).astype(BF16)
                st_qd[slot, p] = (q * edec).astype(BF16)
                ab.append(a.astype(BF16))
                t.append((eye - jnp.where(merge[0], a, 0.0)).astype(BF16))
                rhs_v.append(v * beta)
                rhs_w.append(kb * edec)
        n_prob = len(dst)
        for m in merge[1:]:
            for _ in steps:
                next(filler, None)
            x = [_dot(t[j], jnp.where(m, ab[j], 0.0)) for j in range(n_prob)]
            x = [_dot(x[j], t[j]) for j in range(n_prob)]
            t = [t[j] - x[j].astype(BF16) for j in range(n_prob)]
        for j, (slot, p) in enumerate(dst):
            st_uv[slot, p] = _dot(t[j], rhs_v[j])
            st_w[slot, p] = _dot(t[j], rhs_w[j]).astype(BF16)
        for _ in filler:
            pass

    def state_steps(i):
        rf, rb = rows_of(i)
        slot = i % (2 * grp)
        uv1, uv2 = first_second(st_uv[slot])
        w1, w2 = first_second(st_w[slot])
        qd1, qd2 = first_second(st_qd[slot])
        k1, k2 = first_second(jnp.stack([k_s[h, r, :] for r in (rf, rb) for h in range(HEADS)]))
        dec_f, dec_b = dec_s[rf, :], dec_s[rb, :]
        d1, d2 = first_second(jnp.stack([dec_f[:, j:j + 1] for j in range(HEADS)]
                                        + [dec_b[:, HEADS + j:HEADS + j + 1] for j in range(HEADS)]))
        dl1 = jnp.concatenate([d1[:HEADS, c - 1:c], d1[HEADS:, 0:1]], axis=0)
        dl2 = jnp.concatenate([d2[:HEADS, c - 1:c], d2[HEADS:, 0:1]], axis=0)
        s = s_s[...].reshape(nb, HEAD_DIM, HEAD_DIM)
        u1 = uv1 - _bmm(w1, s, _B_NN)
        o1 = _bmm(qd1, s, _B_NN)
        yield
        s = s * jnp.exp(dl1) + _bmm(k1 * jnp.exp(dl1 - d1), u1, _B_TN)
        yield
        u2 = uv2 - _bmm(w2, s, _B_NN)
        o2 = _bmm(qd2, s, _B_NN)
        yield
        s = s * jnp.exp(dl2) + _bmm(k2 * jnp.exp(dl2 - d2), u2, _B_TN)
        s_s[...] = s.reshape(2, HEADS, HEAD_DIM, HEAD_DIM)
        yield
        o = row_order(o1, o2) + _bmm(st_attn[slot], row_order(u1, u2), _B_NN)
        for j in range(HEADS // 2):
            of_s[j, rf, :] = jnp.concatenate([o[2 * j], o[2 * j + 1]], axis=1)
            ob_s[j, rb, :] = jnp.concatenate([o[HEADS + 2 * j], o[HEADS + 2 * j + 1]], axis=1)

    def group(g):
        return tuple(g * grp + j for j in range(grp))

    def state_chain(g):
        for i in group(g):
            yield from state_steps(i)

    def loop_body(g, carry):
        solve_stage(group(g + 1), state_chain(g))
        return carry

    n_group = n_pair // grp
    solve_stage(group(0))
    if n_group > 1:
        lax.fori_loop(0, n_group - 1, loop_body, 0)
    for _ in state_chain(n_group - 1):
        pass

    o = jnp.concatenate([of_s[j] + ob_s[j] for j in range(HEADS // 2)], axis=1)
    ms = head_sums(o * o) * (1.0 / HEAD_DIM)
    o = o * lax.rsqrt(ms + EPS) * gnorm_ref[...]
    o_ref[...] = (o * _silu(x_ref[:, 3 * GROUP_W:4 * GROUP_W])).astype(BF16)
    if emit_state:
        st_ref[...] = s_s[...]


def _gdn(proj, gate_t, n_batch, seq, l, layer_params, tables, state=None, emit_state=False):
    has_state = state is not None
    sub = 1 if has_state else 2
    in_specs = [pl.BlockSpec((sub * seq, W_GDN), lambda b: (b, 0), pipeline_mode=pl.Buffered(2 if sub > 1 else 1)),
                pl.BlockSpec((16, sub * seq), lambda b: (0, b))]
    args = [proj, gate_t]
    state_block = (sub, 2, HEADS, HEAD_DIM, HEAD_DIM)
    if has_state:
        in_specs.append(pl.BlockSpec((None, None) + state_block[1:], lambda b: (b, l, 0, 0, 0, 0)))
        args.append(state)
    for p in layer_params:
        in_specs.append(_layer_spec(p, l))
        args.append(p)
    for c in tables:
        in_specs.append(_const_spec(c.shape))
        args.append(c)
    out_shape = [jax.ShapeDtypeStruct((n_batch * seq, GROUP_W), BF16)]
    out_specs = [pl.BlockSpec((sub * seq, GROUP_W), lambda b: (b, 0))]
    if emit_state:
        out_shape.append(jax.ShapeDtypeStruct((n_batch, 2, HEADS, HEAD_DIM, HEAD_DIM), F32))
        out_specs.append(pl.BlockSpec(state_block, lambda b: (b, 0, 0, 0, 0)))
    vmem = lambda shape, dtype: pltpu.VMEM((sub,) + shape, dtype)
    scratch = [vmem((HEADS, seq, HEAD_DIM), F32)] * 3 + [vmem((HEADS // 2, seq, 128), F32)] * 2 + [
        vmem((seq, 128), F32), vmem((16, seq), F32), vmem((seq, 128), F32),
        vmem((2, HEADS, HEAD_DIM, HEAD_DIM), F32)]
    nb, pair, ns = 2 * HEADS, 2 * GDN_CHUNK, 2 * _gdn_group(seq)
    scratch += [vmem((ns, nb, pair, HEAD_DIM), F32), vmem((ns, nb, pair, HEAD_DIM), BF16),
                vmem((ns, nb, pair, HEAD_DIM), BF16), vmem((ns, nb, pair, pair), BF16)]
    body = functools.partial(_gdn_body, seq=seq, has_state=has_state, emit_state=emit_state, sub=sub)
    return pl.pallas_call(
        body, out_shape=out_shape, grid=(n_batch // sub,), in_specs=in_specs, out_specs=out_specs,
        scratch_shapes=scratch, compiler_params=_params(("arbitrary",)),
        name="gdn_latent" if has_state else "gdn_context",
    )(*args)


def _hy_filter_body(feat_ref, dist_ref, w1_ref, b1_ref, w2_ref, b2_ref, w3_ref, freq_ref, decay_ref,
                    cf_ref, sf_ref, rot_ref, hr_ref, hi_ref, nyq_ref, *, seq):
    h = jnp.sin(freq_ref[0:1, :] * (_dot3(feat_ref[...], w1_ref[...]) + b1_ref[...]))
    h = jnp.sin(freq_ref[1:2, :] * (_dot3(h, w2_ref[...]) + b2_ref[...]))
    h = _dot3(h, w3_ref[...])
    filt = h * jnp.exp(-dist_ref[...] * jnp.abs(decay_ref[...]))
    kf = cf_ref.shape[0]
    t = lax.broadcasted_iota(jnp.int32, (seq, 1), 0)
    alt = jnp.where((t & 1) == 0, 1.0, -1.0)
    nyq_ref[...] = jnp.sum(filt * alt, axis=0, keepdims=True) * (1.0 / (2 * kf))
    fb = filt.astype(BF16)
    tk = min(kf, 512)
    for k0 in range(0, kf, tk):
        hr = _dot(cf_ref[k0:k0 + tk, :], fb)
        hs = _dot(sf_ref[k0:k0 + tk, :], fb)
        wc, ws = rot_ref[k0:k0 + tk, 0:1], rot_ref[k0:k0 + tk, 1:2]
        hr_ref[k0:k0 + tk, :] = hr * wc + hs * ws
        hi_ref[k0:k0 + tk, :] = hr * ws - hs * wc


def _hy_filter(seq, feats, dist, layer_params, dft):
    cf, sf, _, _, rot = dft
    kf = cf.shape[0]
    per_layer = lambda a: pl.BlockSpec((None,) + a.shape[1:], lambda l: (l,) + (0,) * (a.ndim - 1))
    out_block = lambda rows: pl.BlockSpec((None, rows, 2 * GROUP_W), lambda l: (l, 0, 0))
    return pl.pallas_call(
        functools.partial(_hy_filter_body, seq=seq),
        out_shape=[jax.ShapeDtypeStruct((DEPTH, kf, 2 * GROUP_W), F32),
                   jax.ShapeDtypeStruct((DEPTH, kf, 2 * GROUP_W), F32),
                   jax.ShapeDtypeStruct((DEPTH, 1, 2 * GROUP_W), F32)],
        grid=(DEPTH,),
        in_specs=[_const_spec(feats.shape), _const_spec(dist.shape)] + [per_layer(a) for a in layer_params]
        + [_const_spec(cf.shape), _const_spec(sf.shape), _const_spec(rot.shape)],
        out_specs=[out_block(kf), out_block(kf), out_block(1)],
        compiler_params=_params(("arbitrary",)),
        name="hyena_filter",
    )(feats, dist, *layer_params, cf, sf, rot)


def _hy_conv_body(x_ref, cw_ref, bias_ref, cf_ref, sf_ref, cft_ref, sft_ref, hr_ref, hi_ref, nyq_ref, o_ref,
                  yr_s, yi_s, *, seq, sub):
    for j in range(sub):
        _hy_conv_one(_rows(x_ref, j, seq), cw_ref, bias_ref, cf_ref, sf_ref, cft_ref, sft_ref, hr_ref, hi_ref,
                     nyq_ref, _rows(o_ref, j, seq), yr_s.at[j], yi_s.at[j], seq=seq)


def _hy_conv_one(x_ref, cw_ref, bias_ref, cf_ref, sf_ref, cft_ref, sft_ref, hr_ref, hi_ref, nyq_ref, o_ref,
                 yr_s, yi_s, *, seq):
    t = lax.broadcasted_iota(jnp.int32, (seq, 1), 0)
    alt = jnp.where((t & 1) == 0, 1.0, -1.0)
    kf = cf_ref.shape[0]
    tk = min(kf, 512)

    def long_conv(a, o):
        cols = slice(o * GROUP_W, (o + 1) * GROUP_W)
        ab = a.astype(BF16)
        for k0 in range(0, kf, tk):
            ur = _dot(cf_ref[k0:k0 + tk, :], ab)
            us = _dot(sf_ref[k0:k0 + tk, :], ab)
            hr = hr_ref[k0:k0 + tk, cols]
            hi = hi_ref[k0:k0 + tk, cols]
            yr_s[k0:k0 + tk, :] = (ur * hr + us * hi).astype(BF16)
            yi_s[k0:k0 + tk, :] = (ur * hi - us * hr).astype(BF16)
        u_nyq = jnp.sum(a * alt, axis=0, keepdims=True)
        y = _dot(cft_ref[...], yr_s[...]) - _dot(sft_ref[...], yi_s[...])
        return y + alt * (u_nyq * nyq_ref[:, cols])

    v = _conv3(x_ref[:, 0:GROUP_W], cw_ref.at[:, 0:GROUP_W])
    x1 = _conv3(x_ref[:, GROUP_W:2 * GROUP_W], cw_ref.at[:, GROUP_W:2 * GROUP_W])
    z = x1 * (long_conv(v, 0) + v * bias_ref[0:1, :])
    x2 = _conv3(x_ref[:, 2 * GROUP_W:3 * GROUP_W], cw_ref.at[:, 2 * GROUP_W:3 * GROUP_W])
    o_ref[...] = (x2 * (long_conv(z, 1) + z * bias_ref[1:2, :])).astype(BF16)


def _hy_conv(proj, n_batch, seq, l, cw, bias, dft, hr, hi, nyq):
    cf, sf, cft, sft, _ = dft
    kf = cf.shape[0]
    sub = 4 if seq <= 512 else 1
    return pl.pallas_call(
        functools.partial(_hy_conv_body, seq=seq, sub=sub),
        out_shape=jax.ShapeDtypeStruct((n_batch * seq, GROUP_W), BF16),
        grid=(n_batch // sub,),
        in_specs=[pl.BlockSpec((sub * seq, W_HY), lambda b: (b, 0), pipeline_mode=pl.Buffered(2 if sub > 1 else 1)),
                  _layer_spec(cw, l), _layer_spec(bias, l)] + [_const_spec(a.shape) for a in (cf, sf, cft, sft)]
        + [_layer_spec(hr, l), _layer_spec(hi, l), _layer_spec(nyq, l)],
        out_specs=pl.BlockSpec((sub * seq, GROUP_W), lambda b: (b, 0)),
        scratch_shapes=[pltpu.VMEM((sub, kf, GROUP_W), BF16)] * 2,
        compiler_params=_params(("arbitrary",)),
        name="hyena_conv",
    )(proj, cw, bias, cf, sf, cft, sft, hr, hi, nyq)


def _rope_tables(seq):
    rows = seq // GRID_W
    row = np.repeat(np.arange(rows), GRID_W).astype(np.float64)
    col = np.tile(np.arange(GRID_W), rows).astype(np.float64)

    def pair_tables(dim):
        n_freq = dim // 4
        inv = (ROPE_BASE ** (-np.arange(n_freq, dtype=np.float32) / n_freq)).astype(np.float64)
        ang = np.concatenate([row[:, None] * inv, col[:, None] * inv], axis=-1).astype(np.float32)
        cos = np.repeat(np.cos(ang), 2, axis=-1)
        sin = np.repeat(np.sin(ang), 2, axis=-1) * np.tile(np.array([-1.0, 1.0], np.float32), dim // 2)
        return cos, sin

    ca, sa = pair_tables(MLA_ROPE)
    one = lambda n: np.ones((seq, n), np.float32)
    zero = lambda n: np.zeros((seq, n), np.float32)
    mla = (np.concatenate([one(MLA_NOPE), ca, one(32)], 1), np.concatenate([zero(MLA_NOPE), sa, zero(32)], 1),
           np.concatenate([ca, one(96)], 1), np.concatenate([sa, zero(96)], 1))
    cb, sb = pair_tables(HEAD_DIM)
    swa = (np.tile(cb, (1, HEADS)), np.tile(sb, (1, HEADS)),
           np.tile(cb, (1, SWA_KV_HEADS)), np.tile(sb, (1, SWA_KV_HEADS)))
    as_f32 = lambda ts: tuple(jnp.asarray(t, F32) for t in ts)
    return as_f32(mla), as_f32(swa)


def _dft_tables(seq):
    n = 3 * seq // 2 if (3 * seq // 4) % 256 == 0 else 2 * seq
    kf = n // 2
    k = np.arange(kf, dtype=np.int64)
    s = np.arange(seq, dtype=np.int64)
    ang = ((k[:, None] * s[None, :]) % n).astype(np.float64) * (2.0 * math.pi / n)
    cos, sin = np.cos(ang), np.sin(ang)
    theta = ((k * (seq // 2)) % n).astype(np.float64) * (2.0 * math.pi / n)
    wgt = np.where(k == 0, 1.0 / n, 2.0 / n)
    rot = np.zeros((kf, 128), np.float64)
    rot[:, 0], rot[:, 1] = wgt * np.cos(theta), wgt * np.sin(theta)
    bf = lambda a: jnp.asarray(a, F32).astype(BF16)
    return bf(cos), bf(sin), bf(cos.T), bf(sin.T), jnp.asarray(rot, F32)


def _hy_features(seq):
    t = np.arange(seq, dtype=np.float32)
    t01 = t / np.float32(max(seq - 1, 1))
    w = (np.float32(2.0 * math.pi) * t / np.float32(seq)).astype(np.float64)
    bands = np.linspace(1e-4, HY_BANDS - 1, HY_BANDS, dtype=np.float32).astype(np.float64)
    feats = np.concatenate([t01[:, None].astype(np.float64), np.cos(w[:, None] * bands), -np.sin(w[:, None] * bands)],
                           axis=-1)
    feats = np.pad(feats, ((0, 0), (0, 128 - HY_EMB)))
    dist = (np.abs(t - (seq // 2)) / np.float32(seq / 2))[:, None]
    return jnp.asarray(feats, F32), jnp.asarray(dist, F32)


def _swa_placement():
    pk = np.zeros((128, GROUP_W), np.float32)
    pv = np.zeros((128, HEADS, GROUP_W), np.float32)
    group = HEADS // SWA_KV_HEADS
    for h in range(HEADS):
        for e in range(HEAD_DIM):
            src = (h // group) * HEAD_DIM + e
            pk[src, h * HEAD_DIM + e] = 1.0
            pv[src, h, h * HEAD_DIM + e] = 1.0
    return jnp.asarray(pk, BF16), jnp.asarray(pv.reshape(128, HEADS * GROUP_W), BF16)


def _gdn_tables():
    pair = 2 * GDN_CHUNK
    i = np.arange(pair)
    same = (i[:, None] // GDN_CHUNK) == (i[None, :] // GDN_CHUNK)
    lower = (same & (i[:, None] >= i[None, :])).astype(np.float32)
    upper = (same & (i[:, None] <= i[None, :])).astype(np.float32)
    j = np.arange(GROUP_W)
    ones_bd = ((j[:, None] // HEAD_DIM) == (j[None, :] // HEAD_DIM)).astype(np.float32)
    ones2 = np.concatenate([ones_bd] * 2, axis=0)
    tri3 = np.stack([np.concatenate([m] * 3, axis=1) for m in (lower, upper)])
    trit3 = np.stack([np.concatenate([m] * 3, axis=0) for m in (lower, upper)])
    return jnp.asarray(ones2, BF16), jnp.asarray(tri3, BF16), jnp.asarray(trit3, BF16)


def _pad_last(x, n):
    return jnp.pad(x, [(0, 0)] * (x.ndim - 1) + [(0, n - x.shape[-1])])


def _layout_w_in(w):
    dep, d, _ = w.shape
    mq = _pad_last(w[..., :384].reshape(dep, d, HEADS, MLA_QK), 128).reshape(dep, d, HEADS * 128)
    mla = _pad_last(jnp.concatenate([mq, w[..., 384:544]], axis=-1), W_MLA)
    gdn = _pad_last(w[..., 1056:2096], W_GDN)
    groups = tuple(g.astype(BF16) for g in (mla, w[..., 544:1056], gdn, w[..., 2096:2864]))
    w_gate_t = jnp.swapaxes(w[..., 2080:2096], 1, 2).astype(BF16)
    return groups, w_gate_t


def _layout_w_ukv(w):
    dep = w.shape[0]
    w = w.reshape(dep, MLA_KV_RANK, HEADS, 128)
    wk = _pad_last(w[..., :MLA_NOPE], 128).reshape(dep, MLA_KV_RANK, HEADS * 128)
    place = np.zeros((128, HEADS, 128), np.float32)
    for j in range(MLA_ROPE):
        place[j, :, MLA_NOPE + j] = 1.0
    place = jnp.broadcast_to(jnp.asarray(place.reshape(128, HEADS * 128)), (dep, 128, HEADS * 128))
    wk = jnp.concatenate([wk, place], axis=1)
    eye = jnp.asarray(np.eye(HEADS, dtype=np.float32))
    wv = (w[..., MLA_NOPE:][:, :, :, None, :] * eye[None, None, :, :, None]).reshape(dep, MLA_KV_RANK, HEADS * GROUP_W)
    ones_route = np.zeros((128, HEADS * GROUP_W), np.float32)
    for h in range(HEADS):
        ones_route[127, h * GROUP_W + _sum_lane(h)] = 1.0
    wv = jnp.concatenate([wv, jnp.broadcast_to(jnp.asarray(ones_route), (dep,) + ones_route.shape)], axis=1)
    return jnp.concatenate([wk, wv], axis=-1).astype(BF16)


def _layer_pass(x, n_batch, seq, l, mod, mod_spec, P, mla_ctx=None, swa_ctx=None, state=None, tables=None):
    is_ctx = tables is None
    p_mla, p_swa, p_gdn, p_hy, gate_t = _inproj(x, mod, mod_spec, l, P["g_pre_mix"], P["w_in"], P["w_gate_t"])
    mla_out = _mla(p_mla, n_batch, seq, l, P["mla_kv_norm"], P["w_ukv"], ctx=mla_ctx,
                   tables=None if is_ctx else tables[0], emit_ckv=is_ctx)
    o_a = mla_out[0]
    o_b = _swa(p_swa, n_batch, seq, l, P["swa_sink"], P["swa_pk"], P["swa_pv"], ctx_kv=swa_ctx,
               tables=None if is_ctx else tables[1])
    gdn_out = _gdn(p_gdn, gate_t, n_batch, seq, l, P["gdn_params"], P["gdn_tables"], state=state, emit_state=is_ctx)
    o_c = gdn_out[0]
    o_d = _hy_conv(p_hy, n_batch, seq, l, P["hy_conv"], P["hy_bias"], *(P["hy_ctx"] if is_ctx else P["hy_lat"]))
    y = _outmlp((o_a, o_b, o_c, o_d), x, mod, mod_spec, l, P["g_post_mix"], P["g_pre_mlp"], P["g_post_mlp"],
                P["w_out"], P["mlp_w1"], P["mlp_w2"])
    if not is_ctx:
        return y, None
    kpe0 = HEADS * 128 + MLA_KV_RANK
    new = (mla_out[1].reshape(n_batch, seq, MLA_KV_RANK),
           p_mla[:, kpe0:kpe0 + MLA_ROPE].reshape(n_batch, seq, MLA_ROPE),
           p_swa[:, 256:384].reshape(n_batch, seq, SWA_KV_HEADS, HEAD_DIM),
           p_swa[:, 384:512].reshape(n_batch, seq, SWA_KV_HEADS, HEAD_DIM),
           gdn_out[1])
    return y, new


def kernel(x_prompt, x_sample, cache_mla_ckv, cache_mla_kpe, cache_swa_k, cache_swa_v, state_gdn, c, c_ctx, w_ada, b_ada, g_pre_mix, g_post_mix, g_pre_mlp, g_post_mlp, w_in, w_out, mla_kv_norm, mla_w_ukv, swa_sink, gdn_conv, gdn_a_log, gdn_dt_bias, gdn_norm, hy_conv, hy_w1, hy_b1, hy_w2, hy_b2, hy_w3, hy_freq, hy_decay, hy_bias, mlp_w1, mlp_w2):
    n_ctx_b, seq_ctx, d = x_prompt.shape
    n_lat_b, seq_lat, _ = x_sample.shape
    past = cache_mla_ckv.shape[2]

    cond8 = jnp.concatenate([c_ctx[None, :], c, jnp.zeros((8 - 1 - n_lat_b, d), F32)], axis=0)
    mod = _modulation(cond8, w_ada, b_ada).reshape(DEPTH, 8, 6, 1, d)

    w_in_all, w_gate_t = _layout_w_in(w_in)
    gates = jnp.stack([gdn_a_log.reshape(DEPTH, 8), gdn_dt_bias.reshape(DEPTH, 8)], axis=1)
    gates_t = jnp.pad(jnp.swapaxes(gates, 1, 2), ((0, 0), (0, 8), (0, 0)))
    swa_pk, swa_pv = _swa_placement()
    P = dict(
        g_pre_mix=g_pre_mix[:, None], g_post_mix=g_post_mix[:, None],
        g_pre_mlp=g_pre_mlp[:, None], g_post_mlp=g_post_mlp[:, None],
        w_in=w_in_all, w_gate_t=w_gate_t,
        w_out=w_out.reshape(DEPTH, HEADS, GROUP_W, d).astype(BF16),
        mlp_w1=mlp_w1.astype(BF16), mlp_w2=mlp_w2.astype(BF16),
        mla_kv_norm=mla_kv_norm[:, None], w_ukv=_layout_w_ukv(mla_w_ukv),
        swa_sink=_pad_last(swa_sink, 128)[:, None], swa_pk=swa_pk, swa_pv=swa_pv,
        gdn_params=(gdn_conv, _pad_last(gates, 128), _pad_last(gates_t, 128), jnp.tile(gdn_norm, (1, HEADS))[:, None]),
        gdn_tables=_gdn_tables(),
        hy_conv=hy_conv, hy_bias=hy_bias,
    )
    hy_params = (jnp.pad(hy_w1, ((0, 0), (0, 128 - HY_EMB), (0, 0))), hy_b1[:, None], hy_w2, hy_b2[:, None], hy_w3,
                 hy_freq, hy_decay[:, None])
    for name, s in (("hy_ctx", seq_ctx), ("hy_lat", seq_lat)):
        dft = _dft_tables(s)
        P[name] = (dft,) + tuple(_hy_filter(s, *_hy_features(s), hy_params, dft))

    tables = _rope_tables(seq_lat)
    mla_ctx = jnp.concatenate([cache_mla_ckv, _pad_last(cache_mla_kpe, 127),
                               jnp.ones(cache_mla_kpe.shape[:-1] + (1,), F32)], axis=-1)
    swa_ctx = (cache_swa_k.reshape(n_lat_b, DEPTH, past, 128), cache_swa_v.reshape(n_lat_b, DEPTH, past, 128))
    lat_tiles = seq_lat // TOKEN_TILE

    xp = x_prompt.reshape(n_ctx_b * seq_ctx, d)
    xs = x_sample.reshape(n_lat_b * seq_lat, d)
    news = []
    for l in range(DEPTH):
        xp, new = _layer_pass(xp, n_ctx_b, seq_ctx, l, mod, _mod_spec(l, 0, None), P)
        news.append(new)
        xs, _ = _layer_pass(xs, n_lat_b, seq_lat, l, mod, _mod_spec(l, 1, lat_tiles), P, mla_ctx=mla_ctx,
                            swa_ctx=swa_ctx, state=state_gdn, tables=tables)

    stacked = tuple(jnp.stack([news[l][i] for l in range(DEPTH)], axis=1) for i in range(5))
    return (xp.reshape(n_ctx_b, seq_ctx, d), xs.reshape(n_lat_b, seq_lat, d)) + stacked
```

```python
import functools
import math

import jax
import jax.numpy as jnp
import numpy as np
from jax import lax
from jax.experimental import pallas as pl
from jax.experimental.pallas import tpu as pltpu

F32 = jnp.float32
BF16 = jnp.bfloat16

D_MODEL = 1024
DEPTH = 2
GRID_W = 64
HEADS = 4
HEAD_DIM = 64
GROUP_W = 256
MLA_NOPE = 64
MLA_ROPE = 32
MLA_QK = 96
MLA_KV_RANK = 128
SWA_KV_HEADS = 2
SWA_WINDOW = 128
GDN_CHUNK = 64
GDN_GROUP = 2
HY_BANDS = 8
HY_EMB = 17
HY_FF = 64
D_FF = 4096
ROPE_BASE = 10000.0
EPS = 1e-6

W_MLA = 768
W_SWA = 512
W_GDN = 1152
W_HY = 768
W_ALL = W_MLA + W_SWA + W_GDN + W_HY

TOKEN_TILE = 1024
MLA_ROW_TILE = 256
VMEM_LIMIT = 56 * 1024 * 1024

_NT = (((1,), (1,)), ((), ()))


def _params(sem):
    return pltpu.CompilerParams(dimension_semantics=sem, vmem_limit_bytes=VMEM_LIMIT)


def _const_spec(shape):
    nd = len(shape)
    return pl.BlockSpec(shape, lambda *_: (0,) * nd, pipeline_mode=pl.Buffered(1))


def _layer_spec(arr, l):
    shape = arr.shape[1:]
    return pl.BlockSpec((None,) + shape, lambda *_: (l,) + (0,) * len(shape), pipeline_mode=pl.Buffered(1))


def _dot(a, b):
    return jnp.dot(a.astype(BF16), b.astype(BF16), preferred_element_type=F32)


def _dot_g(a, b, dims):
    return lax.dot_general(a.astype(BF16), b.astype(BF16), dims, preferred_element_type=F32)


_B_NN = (((2,), (1,)), ((0,), (0,)))
_B_NT = (((2,), (2,)), ((0,), (0,)))
_B_TN = (((1,), (1,)), ((0,), (0,)))


def _bmm(a, b, dims):
    return lax.dot_general(a.astype(BF16), b.astype(BF16), dims, preferred_element_type=F32)


def _split_bf16(x, parts):
    out = []
    for _ in range(parts):
        p = x.astype(BF16)
        out.append(p)
        x = x - p.astype(F32)
    return out


def _dot3(a, b):
    a_hi, a_lo = _split_bf16(a, 2)
    b_hi, b_lo = _split_bf16(b, 2)
    return _dot(jnp.concatenate([a_hi, a_hi, a_lo], axis=1), jnp.concatenate([b_hi, b_lo, b_hi], axis=0))


def _rms(x, g):
    return x * lax.rsqrt(jnp.mean(x * x, axis=-1, keepdims=True) + EPS) * g


def _silu(x):
    return x * jax.nn.sigmoid(x)


def _swap_pairs(x):
    n = x.shape[-1]
    nxt = pltpu.roll(x, n - 1, axis=1)
    prv = pltpu.roll(x, 1, axis=1)
    lane = lax.broadcasted_iota(jnp.int32, x.shape, 1)
    return jnp.where((lane & 1) == 0, nxt, prv)


def _rope(x, cos, sin_signed):
    return x * cos + _swap_pairs(x) * sin_signed


def _mod_body(c_ref, w_ref, b_ref, o_ref):
    s = _silu(c_ref[...]).astype(BF16)
    o_ref[...] = _dot(s, w_ref[...].astype(BF16)) + b_ref[...]


def _modulation(cond8, w_ada, b_ada):
    n = 6 * D_MODEL
    tn = 1536
    return pl.pallas_call(
        _mod_body,
        out_shape=jax.ShapeDtypeStruct((DEPTH, 8, n), F32),
        grid=(DEPTH, n // tn),
        in_specs=[
            pl.BlockSpec((8, D_MODEL), lambda l, j: (0, 0)),
            pl.BlockSpec((None, D_MODEL, tn), lambda l, j: (l, 0, j)),
            pl.BlockSpec((None, 1, tn), lambda l, j: (l, 0, j)),
        ],
        out_specs=pl.BlockSpec((None, 8, tn), lambda l, j: (l, 0, j)),
        compiler_params=_params(("arbitrary", "arbitrary")),
        name="modulation",
    )(cond8, w_ada, b_ada.reshape(DEPTH, 1, n))


def _mod_spec(l, row0, tiles_per_row):
    blk = (None, None, 6, 1, D_MODEL)
    if tiles_per_row is None:
        return pl.BlockSpec(blk, lambda i: (l, row0, 0, 0, 0))
    return pl.BlockSpec(blk, lambda i: (l, row0 + i // tiles_per_row, 0, 0, 0))


def _inproj_body(x_ref, mod_ref, g_ref, w_mla, w_swa, w_gdn, w_hy, wgt_ref, o_mla, o_swa, o_gdn, o_hy, o_gate_t):
    h = _rms(x_ref[...], g_ref[...]) * (1.0 + mod_ref[1]) + mod_ref[0]
    hb = h.astype(BF16)
    for o, w_ref in ((o_mla, w_mla), (o_swa, w_swa), (o_gdn, w_gdn), (o_hy, w_hy)):
        o[...] = _dot(hb, w_ref[...])
    o_gate_t[...] = _dot_g(wgt_ref[...], hb, _NT)


def _inproj(x, mod, mod_spec, l, g, w, w_gate_t):
    t = x.shape[0]
    tm = TOKEN_TILE
    widths = (W_MLA, W_SWA, W_GDN, W_HY)
    return pl.pallas_call(
        _inproj_body,
        out_shape=[jax.ShapeDtypeStruct((t, n), F32) for n in widths] + [jax.ShapeDtypeStruct((16, t), F32)],
        grid=(t // tm,),
        in_specs=[
            pl.BlockSpec((tm, D_MODEL), lambda i: (i, 0)),
            mod_spec,
            _layer_spec(g, l)] + [_layer_spec(wg, l) for wg in w] + [_layer_spec(w_gate_t, l),
        ],
        out_specs=[pl.BlockSpec((tm, n), lambda i: (i, 0)) for n in widths]
        + [pl.BlockSpec((16, tm), lambda i: (0, i))],
        compiler_params=_params(("arbitrary",)),
        name="inproj",
    )(x, mod, g, *w, w_gate_t)


def _outmlp_body(oa, ob, oc, od, x_ref, mod_ref, g_post_mix, g_pre_mlp, g_post_mlp,
                 wo_ref, w1_ref, w2_ref, out_ref):
    o = (_dot(oa[...], wo_ref[0]) + _dot(ob[...], wo_ref[1])
         + _dot(oc[...], wo_ref[2]) + _dot(od[...], wo_ref[3]))
    x = x_ref[...] + mod_ref[2] * _rms(o, g_post_mix[...])
    hb = (_rms(x, g_pre_mlp[...]) * (1.0 + mod_ref[4]) + mod_ref[3]).astype(BF16)
    acc = jnp.zeros(x.shape, F32)
    fc = 1024
    for c in range(D_FF // fc):
        a = _dot(hb, w1_ref[:, c * fc:(c + 1) * fc])
        a = jnp.square(jnp.maximum(a, 0.0)).astype(BF16)
        acc = acc + _dot(a, w2_ref[c * fc:(c + 1) * fc, :])
    out_ref[...] = x + mod_ref[5] * _rms(acc, g_post_mlp[...])


def _outmlp(o_parts, x, mod, mod_spec, l, g_post_mix, g_pre_mlp, g_post_mlp, wo, w1, w2):
    t = x.shape[0]
    tm = TOKEN_TILE
    part_spec = pl.BlockSpec((tm, GROUP_W), lambda i: (i, 0))
    return pl.pallas_call(
        _outmlp_body,
        out_shape=jax.ShapeDtypeStruct((t, D_MODEL), F32),
        grid=(t // tm,),
        in_specs=[part_spec] * 4 + [
            pl.BlockSpec((tm, D_MODEL), lambda i: (i, 0)),
            mod_spec,
        ] + [_layer_spec(a, l) for a in (g_post_mix, g_pre_mlp, g_post_mlp, wo, w1, w2)],
        out_specs=pl.BlockSpec((tm, D_MODEL), lambda i: (i, 0)),
        compiler_params=_params(("arbitrary",)),
        name="outproj_mlp",
    )(*o_parts, x, mod, g_post_mix, g_pre_mlp, g_post_mlp, wo, w1, w2)


def _sum_lane(h):
    return ((h + 1) % HEADS) * HEAD_DIM


def _rows(ref, j, n):
    return ref.at[pl.ds(j * n, n), :]


def _mla_body(*refs, seq, n_ctx, rope, emit_ckv, row_chunk, sub):
    it = iter(refs)
    q_ref, kv_ref = next(it), next(it)
    ctx_ref = next(it) if n_ctx else None
    g_ref, w_ref = next(it), next(it)
    tabs = tuple(next(it) for _ in range(4)) if rope else None
    o_ref = next(it)
    ckv_out = next(it) if emit_ckv else None
    k_s, v_s = next(it), next(it)
    tq = q_ref.shape[0] // sub
    for j in range(sub):
        _mla_one(_rows(q_ref, j, tq), _rows(kv_ref, j, seq), ctx_ref, g_ref, w_ref, tabs, _rows(o_ref, j, tq),
                 _rows(ckv_out, j, seq) if emit_ckv else None, k_s.at[j], v_s.at[j],
                 seq=seq, n_ctx=n_ctx, rope=rope, emit_ckv=emit_ckv, row_chunk=row_chunk)


def _mla_one(q_ref, kv_ref, ctx_ref, g_ref, w_ref, tabs, o_ref, ckv_out, k_s, v_s, *,
             seq, n_ctx, rope, emit_ckv, row_chunk):
    if rope:
        cq_ref, sq_ref, ck_ref, sk_ref = tabs

    def expand(kin, r0, n):
        kv = _dot(kin.astype(BF16), w_ref[...])
        k_s[r0:r0 + n, :] = kv[:, :HEADS * 128].astype(BF16)
        for h in range(HEADS):
            c0 = HEADS * 128 + h * GROUP_W
            v_s[h, r0:r0 + n, :] = kv[:, c0:c0 + GROUP_W].astype(BF16)

    @pl.when(pl.program_id(1) == 0)
    def _prep():
        for r0 in range(0, seq, row_chunk):
            blk = kv_ref[r0:r0 + row_chunk, :]
            cn = _rms(blk[:, :MLA_KV_RANK], g_ref[...])
            pe = blk[:, MLA_KV_RANK:]
            if rope:
                pe = _rope(pe, ck_ref[r0:r0 + row_chunk, :], sk_ref[r0:r0 + row_chunk, :])
            if emit_ckv:
                ckv_out[r0:r0 + row_chunk, :] = cn
            pe = jnp.where(lax.broadcasted_iota(jnp.int32, pe.shape, 1) == 127, 1.0, pe)
            expand(jnp.concatenate([cn, pe], axis=1), r0, row_chunk)
        if n_ctx:
            expand(ctx_ref[...], seq, n_ctx)

    scale = MLA_QK ** -0.5
    rt = min(q_ref.shape[0], MLA_ROW_TILE)
    lane_head = lax.broadcasted_iota(jnp.int32, (rt, GROUP_W), 1) // HEAD_DIM
    for r0 in range(0, q_ref.shape[0], rt):
        acc = jnp.zeros((rt, GROUP_W), F32)
        for h in range(HEADS):
            qh = q_ref[r0:r0 + rt, h * 128:(h + 1) * 128]
            if rope:
                qh = _rope(qh, cq_ref[r0:r0 + rt, :], sq_ref[r0:r0 + rt, :])
            s = _dot_g(qh * scale, k_s[:, h * 128:(h + 1) * 128], _NT)
            p = jnp.exp(s - jnp.max(s, axis=-1, keepdims=True))
            pv = _dot(p, v_s[h])
            inv = 1.0 / pv[:, _sum_lane(h):_sum_lane(h) + 1]
            acc = acc + jnp.where(lane_head == h, pv * inv, 0.0)
        o_ref[r0:r0 + rt, :] = acc.astype(BF16)


def _mla(proj, n_batch, seq, l, g, w, ctx=None, tables=None, emit_ckv=False):
    tq = min(seq, 1024)
    nq = seq // tq
    sub = 4 if nq == 1 else 1
    n_ctx = 0 if ctx is None else ctx.shape[2]
    rope = tables is not None
    lk = seq + n_ctx
    in_specs = [
        pl.BlockSpec((sub * tq, HEADS * 128), lambda b, i: (b * nq + i, 0)),
        pl.BlockSpec((sub * seq, 256), lambda b, i: (b, 2)),
    ]
    args = [proj, proj]
    if n_ctx:
        in_specs.append(pl.BlockSpec((None, None, n_ctx, 256), lambda b, i: (b, l, 0, 0)))
        args.append(ctx)
    in_specs += [_layer_spec(g, l), _layer_spec(w, l)]
    args += [g, w]
    if rope:
        in_specs += [pl.BlockSpec((tq, 128), lambda b, i: (i, 0))] * 2 + [_const_spec((seq, 128))] * 2
        args += list(tables)
    out_shape = [jax.ShapeDtypeStruct((n_batch * seq, GROUP_W), BF16)]
    out_specs = [pl.BlockSpec((sub * tq, GROUP_W), lambda b, i: (b * nq + i, 0))]
    if emit_ckv:
        out_shape.append(jax.ShapeDtypeStruct((n_batch * seq, MLA_KV_RANK), F32))
        out_specs.append(pl.BlockSpec((sub * seq, MLA_KV_RANK), lambda b, i: (b, 0)))
    body = functools.partial(_mla_body, seq=seq, n_ctx=n_ctx, rope=rope, emit_ckv=emit_ckv,
                             row_chunk=min(seq, 512), sub=sub)
    return pl.pallas_call(
        body, out_shape=out_shape, grid=(n_batch // sub, nq), in_specs=in_specs, out_specs=out_specs,
        scratch_shapes=[pltpu.VMEM((sub, lk, HEADS * 128), BF16), pltpu.VMEM((sub, HEADS, lk, GROUP_W), BF16)],
        compiler_params=_params(("arbitrary", "arbitrary")),
        name="mla_latent" if rope else "mla_context",
    )(*args)


def _swa_body(*refs, seq, n_ctx, local, row_chunk, sub):
    it = iter(refs)
    q_ref, kv_ref = next(it), next(it)
    ctx_refs = (next(it), next(it)) if n_ctx else None
    consts = (next(it), next(it), next(it))
    tabs = tuple(next(it) for _ in range(4)) if local else None
    o_ref = next(it)
    k_s, v_s = next(it), next(it)
    tq = q_ref.shape[0] // sub
    for j in range(sub):
        _swa_one(_rows(q_ref, j, tq), _rows(kv_ref, j, seq), ctx_refs, consts, tabs, _rows(o_ref, j, tq),
                 k_s.at[j], v_s.at[j], seq=seq, n_ctx=n_ctx, local=local, row_chunk=row_chunk)


def _swa_one(q_ref, kv_ref, ctx_refs, consts, tabs, o_ref, k_s, v_s, *, seq, n_ctx, local, row_chunk):
    if n_ctx:
        kc_ref, vc_ref = ctx_refs
    sink_ref, pk_ref, pv_ref = consts
    if local:
        cq_ref, sq_ref, ck_ref, sk_ref = tabs
    tq = q_ref.shape[0]
    blk_id = pl.program_id(1)

    def expand(k, v, r0, n):
        k_s[r0:r0 + n, :] = _dot(k.astype(BF16), pk_ref[...]).astype(BF16)
        vb = v.astype(BF16)
        lane = lax.broadcasted_iota(jnp.int32, (n, GROUP_W), 1)
        for h in range(HEADS):
            ve = _dot(vb, pv_ref[:, h * GROUP_W:(h + 1) * GROUP_W])
            v_s[h, r0:r0 + n, :] = jnp.where(lane == _sum_lane(h), 1.0, ve).astype(BF16)

    @pl.when(blk_id == 0)
    def _prep():
        for r0 in range(0, seq, row_chunk):
            k = kv_ref[r0:r0 + row_chunk, 0:128]
            v = kv_ref[r0:r0 + row_chunk, 128:256]
            if local:
                k = _rope(k, ck_ref[r0:r0 + row_chunk, :], sk_ref[r0:r0 + row_chunk, :])
            expand(k, v, r0, row_chunk)
        if n_ctx:
            expand(kc_ref[...], vc_ref[...], seq, n_ctx)

    scale = HEAD_DIM ** -0.5
    rt = tq
    lane_head = lax.broadcasted_iota(jnp.int32, (rt, GROUP_W), 1) // HEAD_DIM
    for r0 in range(0, tq, rt):
        q = q_ref[r0:r0 + rt, :]
        if local:
            q = _rope(q, cq_ref[r0:r0 + rt, :], sq_ref[r0:r0 + rt, :])
            win = rt + 2 * SWA_WINDOW
            q0 = blk_id * tq + r0
            start = pl.multiple_of(jnp.clip(q0 - SWA_WINDOW, 0, seq - win), SWA_WINDOW)
            qpos = q0 + lax.broadcasted_iota(jnp.int32, (rt, win), 0)
            kpos = start + lax.broadcasted_iota(jnp.int32, (rt, win), 1)
            valid = jnp.abs(qpos - kpos) <= SWA_WINDOW
        q = q * scale
        acc = jnp.zeros((rt, GROUP_W), F32)
        for h in range(HEADS):
            qm = jnp.where(lane_head == h, q, 0.0).astype(BF16)
            sink = sink_ref[:, h:h + 1]
            if local:
                s_loc = jnp.where(valid, _dot_g(qm, k_s[pl.ds(start, win), :], _NT), -jnp.inf)
                s_ctx = _dot_g(qm, k_s[seq:seq + n_ctx, :], _NT)
                m = jnp.maximum(jnp.maximum(jnp.max(s_loc, axis=-1, keepdims=True),
                                            jnp.max(s_ctx, axis=-1, keepdims=True)), sink)
                o = (_dot(jnp.exp(s_loc - m), v_s[h, pl.ds(start, win), :])
                     + _dot(jnp.exp(s_ctx - m), v_s[h, seq:seq + n_ctx, :]))
            else:
                s = _dot_g(qm, k_s[...], _NT)
                m = jnp.maximum(jnp.max(s, axis=-1, keepdims=True), sink)
                o = _dot(jnp.exp(s - m), v_s[h])
            den = o[:, _sum_lane(h):_sum_lane(h) + 1] + jnp.exp(sink - m)
            acc = acc + jnp.where(lane_head == h, o * (1.0 / den), 0.0)
        o_ref[r0:r0 + rt, :] = acc.astype(BF16)


def _swa(proj, n_batch, seq, l, sink, pk, pv, ctx_kv=None, tables=None):
    local = tables is not None
    tq = 2 * SWA_WINDOW if local else seq
    nq = seq // tq
    sub = 4 if nq == 1 else 1
    n_ctx = 0 if ctx_kv is None else ctx_kv[0].shape[2]
    lk = seq + n_ctx
    in_specs = [
        pl.BlockSpec((sub * tq, GROUP_W), lambda b, i: (b * nq + i, 0)),
        pl.BlockSpec((sub * seq, 256), lambda b, i: (b, 1)),
    ]
    args = [proj, proj]
    if n_ctx:
        in_specs += [pl.BlockSpec((None, None, n_ctx, 128), lambda b, i: (b, l, 0, 0))] * 2
        args += list(ctx_kv)
    in_specs += [_layer_spec(sink, l), _const_spec(pk.shape), _const_spec(pv.shape)]
    args += [sink, pk, pv]
    if local:
        in_specs += [pl.BlockSpec((tq, GROUP_W), lambda b, i: (i, 0))] * 2 + [_const_spec((seq, 128))] * 2
        args += list(tables)
    body = functools.partial(_swa_body, seq=seq, n_ctx=n_ctx, local=local, row_chunk=min(seq, 512), sub=sub)
    return pl.pallas_call(
        body,
        out_shape=jax.ShapeDtypeStruct((n_batch * seq, GROUP_W), BF16),
        grid=(n_batch // sub, nq), in_specs=in_specs,
        out_specs=pl.BlockSpec((sub * tq, GROUP_W), lambda b, i: (b * nq + i, 0)),
        scratch_shapes=[pltpu.VMEM((sub, lk, GROUP_W), BF16), pltpu.VMEM((sub, HEADS, lk, GROUP_W), BF16)],
        compiler_params=_params(("arbitrary", "arbitrary")),
        name="swa_latent" if local else "swa_context",
    )(*args)


def _conv3(x, w_ref):
    n = x.shape[0]
    row = lax.broadcasted_iota(jnp.int32, x.shape, 0)
    prev = jnp.where(row == 0, 0.0, pltpu.roll(x, 1, axis=0))
    nxt = jnp.where(row == n - 1, 0.0, pltpu.roll(x, n - 1, axis=0))
    return prev * w_ref[0:1, :] + x * w_ref[1:2, :] + nxt * w_ref[2:3, :]


def _gdn_group(seq):
    n_pair = seq // (2 * GDN_CHUNK)
    return n_pair if n_pair <= GDN_GROUP else 1


def _gdn_body(*refs, seq, has_state, emit_state, sub):
    it = iter(refs)
    x_ref, gt_ref = next(it), next(it)
    s0_ref = next(it) if has_state else None
    consts = tuple(next(it) for _ in range(7))
    o_ref = next(it)
    st_ref = next(it) if emit_state else None
    scratch = tuple(it)
    for j in range(sub):
        _gdn_one(_rows(x_ref, j, seq), gt_ref.at[:, pl.ds(j * seq, seq)], s0_ref, consts, _rows(o_ref, j, seq),
                 st_ref.at[j] if emit_state else None, tuple(s.at[j] for s in scratch),
                 seq=seq, has_state=has_state, emit_state=emit_state)


def _gdn_one(x_ref, gt_ref, s0_ref, consts, o_ref, st_ref, scratch, *, seq, has_state, emit_state):
    cw_ref, gpar_ref, gpart_ref, gnorm_ref, ones_ref, tri_ref, trit_ref = consts
    q_s, k_s, v_s, of_s, ob_s, dec_s, dect_s, gate_s, s_s = scratch[:9]
    st_uv, st_w, st_qd, st_attn = scratch[9:]
    pair = 2 * GDN_CHUNK
    n_pair = seq // pair
    grp = _gdn_group(seq)

    def conv_act(g):
        cols = slice(g * GROUP_W, (g + 1) * GROUP_W)
        return _silu(_conv3(x_ref[:, cols], cw_ref.at[:, cols]))

    def head_sums(x):
        return _dot(jnp.concatenate(_split_bf16(x, 2), axis=1), ones_ref[...])

    def head_l2(x):
        return x * lax.rsqrt(head_sums(x * x) + EPS)

    def to_heads(dst, x):
        for h in range(HEADS):
            dst[h] = x[:, h * HEAD_DIM:(h + 1) * HEAD_DIM]

    to_heads(q_s, head_l2(conv_act(0)) * (HEAD_DIM ** -0.5))
    to_heads(k_s, head_l2(conv_act(1)))
    to_heads(v_s, conv_act(2))
    gab = x_ref[:, 4 * GROUP_W:4 * GROUP_W + 128]
    glog = -jnp.exp(gpar_ref[0:1, :]) * jax.nn.softplus(gab + gpar_ref[1:2, :])
    gate_s[...] = jax.nn.sigmoid(gab)
    lane = lax.broadcasted_iota(jnp.int32, (pair, 128), 1)
    glogt = -jnp.exp(gpart_ref[:, 0:1]) * jax.nn.softplus(gt_ref[...] + gpart_ref[:, 1:2])
    rowi = lax.broadcasted_iota(jnp.int32, (16, pair), 0)
    g3 = _split_bf16(glog, 3)
    gt3 = _split_bf16(glogt, 3)
    for p in range(n_pair):
        r = slice(p * pair, (p + 1) * pair)
        gcol = jnp.concatenate([g[r, :] for g in g3], axis=0)
        grow = jnp.concatenate([g[:, r] for g in gt3], axis=1)
        dec_s[r, :] = jnp.where(lane < HEADS, _dot(tri_ref[0], gcol), _dot(tri_ref[1], gcol))
        dect_s[:, r] = jnp.where(rowi < HEADS, _dot(grow, trit_ref[1]), _dot(grow, trit_ref[0]))

    if has_state:
        s_s[...] = s0_ref[...]
    else:
        s_s[...] = jnp.zeros(s_s.shape, F32)

    ri = lax.broadcasted_iota(jnp.int32, (pair, pair), 0)
    ci = lax.broadcasted_iota(jnp.int32, (pair, pair), 1)
    same = (ri // GDN_CHUNK) == (ci // GDN_CHUNK)
    nb = 2 * HEADS
    incl = (same & (ri >= ci), same & (ri <= ci))
    strict = (same & (ri > ci), same & (ri < ci))
    eye = (ri == ci).astype(F32)
    merge = [((ri // (2 * s)) == (ci // (2 * s))) & ((ri // s) != (ci // s))
             for s in (1, 2, 4, 8, 16, 32)]
    c = GDN_CHUNK

    def first_second(x):
        return (jnp.concatenate([x[:HEADS, :c], x[HEADS:, c:]], axis=0),
                jnp.concatenate([x[:HEADS, c:], x[HEADS:, :c]], axis=0))

    def row_order(first, second):
        return jnp.concatenate([jnp.concatenate([first[:HEADS], second[:HEADS]], axis=1),
                                jnp.concatenate([second[HEADS:], first[HEADS:]], axis=1)], axis=0)

    def rows_of(i):
        fwd, bwd = i * pair, (n_pair - 1 - i) * pair
        if not isinstance(i, int):
            fwd, bwd = pl.multiple_of(fwd, pair), pl.multiple_of(bwd, pair)
        return pl.ds(fwd, pair), pl.ds(bwd, pair)

    def solve_stage(steps, filler=iter(())):
        ab, t, rhs_v, rhs_w, dst = [], [], [], [], []
        for i, d in [(i, d) for i in steps for d in range(2)]:
            slot = i % (2 * grp)
            r = rows_of(i)[d]
            gd_, dd_, td_ = gate_s[r, :], dec_s[r, :], dect_s[:, r]
            for h in range(HEADS):
                p, col = d * HEADS + h, d * HEADS + h
                dst.append((slot, p))
                q, k, v = q_s[h, r, :], k_s[h, r, :], v_s[h, r, :]
                beta = gd_[:, 8 + col:9 + col]
                dcol = dd_[:, col:col + 1]
                gam = jnp.where(incl[d], jnp.exp(jnp.where(incl[d], dcol - td_[col:col + 1, :], 0.0)), 0.0)
                kb = k * beta
                kk_qk = _dot_g(jnp.concatenate([kb, q], axis=0), k, _NT)
                a = jnp.where(strict[d], kk_qk[:pair] * gam, 0.0)
                edec = jnp.exp(dcol)
                st_attn[slot, p] = (kk_qk[pair:] * gam).astype(BF16)
                st_qd[slot, p] = (q * edec).astype(BF16)
                ab.append(a.astype(BF16))
                t.append((eye - jnp.where(merge[0], a, 0.0)).astype(BF16))
                rhs_v.append(v * beta)
                rhs_w.append(kb * edec)
        n_prob = len(dst)
        for m in merge[1:]:
            for _ in steps:
                next(filler, None)
            x = [_dot(t[j], jnp.where(m, ab[j], 0.0)) for j in range(n_prob)]
            x = [_dot(x[j], t[j]) for j in range(n_prob)]
            t = [t[j] - x[j].astype(BF16) for j in range(n_prob)]
        for j, (slot, p) in enumerate(dst):
            st_uv[slot, p] = _dot(t[j], rhs_v[j])
            st_w[slot, p] = _dot(t[j], rhs_w[j]).astype(BF16)
        for _ in filler:
            pass

    def state_steps(i):
        rf, rb = rows_of(i)
        slot = i % (2 * grp)
        uv1, uv2 = first_second(st_uv[slot])
        w1, w2 = first_second(st_w[slot])
        qd1, qd2 = first_second(st_qd[slot])
        k1, k2 = first_second(jnp.stack([k_s[h, r, :] for r in (rf, rb) for h in range(HEADS)]))
        dec_f, dec_b = dec_s[rf, :], dec_s[rb, :]
        d1, d2 = first_second(jnp.stack([dec_f[:, j:j + 1] for j in range(HEADS)]
                                        + [dec_b[:, HEADS + j:HEADS + j + 1] for j in range(HEADS)]))
        dl1 = jnp.concatenate([d1[:HEADS, c - 1:c], d1[HEADS:, 0:1]], axis=0)
        dl2 = jnp.concatenate([d2[:HEADS, c - 1:c], d2[HEADS:, 0:1]], axis=0)
        s = s_s[...].reshape(nb, HEAD_DIM, HEAD_DIM)
        u1 = uv1 - _bmm(w1, s, _B_NN)
        o1 = _bmm(qd1, s, _B_NN)
        yield
        s = s * jnp.exp(dl1) + _bmm(k1 * jnp.exp(dl1 - d1), u1, _B_TN)
        yield
        u2 = uv2 - _bmm(w2, s, _B_NN)
        o2 = _bmm(qd2, s, _B_NN)
        yield
        s = s * jnp.exp(dl2) + _bmm(k2 * jnp.exp(dl2 - d2), u2, _B_TN)
        s_s[...] = s.reshape(2, HEADS, HEAD_DIM, HEAD_DIM)
        yield
        o = row_order(o1, o2) + _bmm(st_attn[slot], row_order(u1, u2), _B_NN)
        for j in range(HEADS // 2):
            of_s[j, rf, :] = jnp.concatenate([o[2 * j], o[2 * j + 1]], axis=1)
            ob_s[j, rb, :] = jnp.concatenate([o[HEADS + 2 * j], o[HEADS + 2 * j + 1]], axis=1)

    def group(g):
        return tuple(g * grp + j for j in range(grp))

    def state_chain(g):
        for i in group(g):
            yield from state_steps(i)

    def loop_body(g, carry):
        solve_stage(group(g + 1), state_chain(g))
        return carry

    n_group = n_pair // grp
    solve_stage(group(0))
    if n_group > 1:
        lax.fori_loop(0, n_group - 1, loop_body, 0)
    for _ in state_chain(n_group - 1):
        pass

    o = jnp.concatenate([of_s[j] + ob_s[j] for j in range(HEADS // 2)], axis=1)
    ms = head_sums(o * o) * (1.0 / HEAD_DIM)
    o = o * lax.rsqrt(ms + EPS) * gnorm_ref[...]
    o_ref[...] = (o * _silu(x_ref[:, 3 * GROUP_W:4 * GROUP_W])).astype(BF16)
    if emit_state:
        st_ref[...] = s_s[...]


def _gdn(proj, gate_t, n_batch, seq, l, layer_params, tables, state=None, emit_state=False):
    has_state = state is not None
    sub = 1 if has_state else 2
    in_specs = [pl.BlockSpec((sub * seq, W_GDN), lambda b: (b, 0), pipeline_mode=pl.Buffered(2 if sub > 1 else 1)),
                pl.BlockSpec((16, sub * seq), lambda b: (0, b))]
    args = [proj, gate_t]
    state_block = (sub, 2, HEADS, HEAD_DIM, HEAD_DIM)
    if has_state:
        in_specs.append(pl.BlockSpec((None, None) + state_block[1:], lambda b: (b, l, 0, 0, 0, 0)))
        args.append(state)
    for p in layer_params:
        in_specs.append(_layer_spec(p, l))
        args.append(p)
    for c in tables:
        in_specs.append(_const_spec(c.shape))
        args.append(c)
    out_shape = [jax.ShapeDtypeStruct((n_batch * seq, GROUP_W), BF16)]
    out_specs = [pl.BlockSpec((sub * seq, GROUP_W), lambda b: (b, 0))]
    if emit_state:
        out_shape.append(jax.ShapeDtypeStruct((n_batch, 2, HEADS, HEAD_DIM, HEAD_DIM), F32))
        out_specs.append(pl.BlockSpec(state_block, lambda b: (b, 0, 0, 0, 0)))
    vmem = lambda shape, dtype: pltpu.VMEM((sub,) + shape, dtype)
    scratch = [vmem((HEADS, seq, HEAD_DIM), F32)] * 3 + [vmem((HEADS // 2, seq, 128), F32)] * 2 + [
        vmem((seq, 128), F32), vmem((16, seq), F32), vmem((seq, 128), F32),
        vmem((2, HEADS, HEAD_DIM, HEAD_DIM), F32)]
    nb, pair, ns = 2 * HEADS, 2 * GDN_CHUNK, 2 * _gdn_group(seq)
    scratch += [vmem((ns, nb, pair, HEAD_DIM), F32), vmem((ns, nb, pair, HEAD_DIM), BF16),
                vmem((ns, nb, pair, HEAD_DIM), BF16), vmem((ns, nb, pair, pair), BF16)]
    body = functools.partial(_gdn_body, seq=seq, has_state=has_state, emit_state=emit_state, sub=sub)
    return pl.pallas_call(
        body, out_shape=out_shape, grid=(n_batch // sub,), in_specs=in_specs, out_specs=out_specs,
        scratch_shapes=scratch, compiler_params=_params(("arbitrary",)),
        name="gdn_latent" if has_state else "gdn_context",
    )(*args)


def _hy_filter_body(feat_ref, dist_ref, w1_ref, b1_ref, w2_ref, b2_ref, w3_ref, freq_ref, decay_ref,
                    cf_ref, sf_ref, rot_ref, hr_ref, hi_ref, nyq_ref, *, seq):
    h = jnp.sin(freq_ref[0:1, :] * (_dot3(feat_ref[...], w1_ref[...]) + b1_ref[...]))
    h = jnp.sin(freq_ref[1:2, :] * (_dot3(h, w2_ref[...]) + b2_ref[...]))
    h = _dot3(h, w3_ref[...])
    filt = h * jnp.exp(-dist_ref[...] * jnp.abs(decay_ref[...]))
    kf = cf_ref.shape[0]
    t = lax.broadcasted_iota(jnp.int32, (seq, 1), 0)
    alt = jnp.where((t & 1) == 0, 1.0, -1.0)
    nyq_ref[...] = jnp.sum(filt * alt, axis=0, keepdims=True) * (1.0 / (2 * kf))
    fb = filt.astype(BF16)
    tk = min(kf, 512)
    for k0 in range(0, kf, tk):
        hr = _dot(cf_ref[k0:k0 + tk, :], fb)
        hs = _dot(sf_ref[k0:k0 + tk, :], fb)
        wc, ws = rot_ref[k0:k0 + tk, 0:1], rot_ref[k0:k0 + tk, 1:2]
        hr_ref[k0:k0 + tk, :] = hr * wc + hs * ws
        hi_ref[k0:k0 + tk, :] = hr * ws - hs * wc


def _hy_filter(seq, feats, dist, layer_params, dft):
    cf, sf, _, _, rot = dft
    kf = cf.shape[0]
    per_layer = lambda a: pl.BlockSpec((None,) + a.shape[1:], lambda l: (l,) + (0,) * (a.ndim - 1))
    out_block = lambda rows: pl.BlockSpec((None, rows, 2 * GROUP_W), lambda l: (l, 0, 0))
    return pl.pallas_call(
        functools.partial(_hy_filter_body, seq=seq),
        out_shape=[jax.ShapeDtypeStruct((DEPTH, kf, 2 * GROUP_W), F32),
                   jax.ShapeDtypeStruct((DEPTH, kf, 2 * GROUP_W), F32),
                   jax.ShapeDtypeStruct((DEPTH, 1, 2 * GROUP_W), F32)],
        grid=(DEPTH,),
        in_specs=[_const_spec(feats.shape), _const_spec(dist.shape)] + [per_layer(a) for a in layer_params]
        + [_const_spec(cf.shape), _const_spec(sf.shape), _const_spec(rot.shape)],
        out_specs=[out_block(kf), out_block(kf), out_block(1)],
        compiler_params=_params(("arbitrary",)),
        name="hyena_filter",
    )(feats, dist, *layer_params, cf, sf, rot)


def _hy_conv_body(x_ref, cw_ref, bias_ref, cf_ref, sf_ref, cft_ref, sft_ref, hr_ref, hi_ref, nyq_ref, o_ref,
                  yr_s, yi_s, *, seq, sub):
    for j in range(sub):
        _hy_conv_one(_rows(x_ref, j, seq), cw_ref, bias_ref, cf_ref, sf_ref, cft_ref, sft_ref, hr_ref, hi_ref,
                     nyq_ref, _rows(o_ref, j, seq), yr_s.at[j], yi_s.at[j], seq=seq)


def _hy_conv_one(x_ref, cw_ref, bias_ref, cf_ref, sf_ref, cft_ref, sft_ref, hr_ref, hi_ref, nyq_ref, o_ref,
                 yr_s, yi_s, *, seq):
    t = lax.broadcasted_iota(jnp.int32, (seq, 1), 0)
    alt = jnp.where((t & 1) == 0, 1.0, -1.0)
    kf = cf_ref.shape[0]
    tk = min(kf, 512)

    def long_conv(a, o):
        cols = slice(o * GROUP_W, (o + 1) * GROUP_W)
        ab = a.astype(BF16)
        for k0 in range(0, kf, tk):
            ur = _dot(cf_ref[k0:k0 + tk, :], ab)
            us = _dot(sf_ref[k0:k0 + tk, :], ab)
            hr = hr_ref[k0:k0 + tk, cols]
            hi = hi_ref[k0:k0 + tk, cols]
            yr_s[k0:k0 + tk, :] = (ur * hr + us * hi).astype(BF16)
            yi_s[k0:k0 + tk, :] = (ur * hi - us * hr).astype(BF16)
        u_nyq = jnp.sum(a * alt, axis=0, keepdims=True)
        y = _dot(cft_ref[...], yr_s[...]) - _dot(sft_ref[...], yi_s[...])
        return y + alt * (u_nyq * nyq_ref[:, cols])

    v = _conv3(x_ref[:, 0:GROUP_W], cw_ref.at[:, 0:GROUP_W])
    x1 = _conv3(x_ref[:, GROUP_W:2 * GROUP_W], cw_ref.at[:, GROUP_W:2 * GROUP_W])
    z = x1 * (long_conv(v, 0) + v * bias_ref[0:1, :])
    x2 = _conv3(x_ref[:, 2 * GROUP_W:3 * GROUP_W], cw_ref.at[:, 2 * GROUP_W:3 * GROUP_W])
    o_ref[...] = (x2 * (long_conv(z, 1) + z * bias_ref[1:2, :])).astype(BF16)


def _hy_conv(proj, n_batch, seq, l, cw, bias, dft, hr, hi, nyq):
    cf, sf, cft, sft, _ = dft
    kf = cf.shape[0]
    sub = 4 if seq <= 512 else 1
    return pl.pallas_call(
        functools.partial(_hy_conv_body, seq=seq, sub=sub),
        out_shape=jax.ShapeDtypeStruct((n_batch * seq, GROUP_W), BF16),
        grid=(n_batch // sub,),
        in_specs=[pl.BlockSpec((sub * seq, W_HY), lambda b: (b, 0), pipeline_mode=pl.Buffered(2 if sub > 1 else 1)),
                  _layer_spec(cw, l), _layer_spec(bias, l)] + [_const_spec(a.shape) for a in (cf, sf, cft, sft)]
        + [_layer_spec(hr, l), _layer_spec(hi, l), _layer_spec(nyq, l)],
        out_specs=pl.BlockSpec((sub * seq, GROUP_W), lambda b: (b, 0)),
        scratch_shapes=[pltpu.VMEM((sub, kf, GROUP_W), BF16)] * 2,
        compiler_params=_params(("arbitrary",)),
        name="hyena_conv",
    )(proj, cw, bias, cf, sf, cft, sft, hr, hi, nyq)


def _rope_tables(seq):
    rows = seq // GRID_W
    row = np.repeat(np.arange(rows), GRID_W).astype(np.float64)
    col = np.tile(np.arange(GRID_W), rows).astype(np.float64)

    def pair_tables(dim):
        n_freq = dim // 4
        inv = (ROPE_BASE ** (-np.arange(n_freq, dtype=np.float32) / n_freq)).astype(np.float64)
        ang = np.concatenate([row[:, None] * inv, col[:, None] * inv], axis=-1).astype(np.float32)
        cos = np.repeat(np.cos(ang), 2, axis=-1)
        sin = np.repeat(np.sin(ang), 2, axis=-1) * np.tile(np.array([-1.0, 1.0], np.float32), dim // 2)
        return cos, sin

    ca, sa = pair_tables(MLA_ROPE)
    one = lambda n: np.ones((seq, n), np.float32)
    zero = lambda n: np.zeros((seq, n), np.float32)
    mla = (np.concatenate([one(MLA_NOPE), ca, one(32)], 1), np.concatenate([zero(MLA_NOPE), sa, zero(32)], 1),
           np.concatenate([ca, one(96)], 1), np.concatenate([sa, zero(96)], 1))
    cb, sb = pair_tables(HEAD_DIM)
    swa = (np.tile(cb, (1, HEADS)), np.tile(sb, (1, HEADS)),
           np.tile(cb, (1, SWA_KV_HEADS)), np.tile(sb, (1, SWA_KV_HEADS)))
    as_f32 = lambda ts: tuple(jnp.asarray(t, F32) for t in ts)
    return as_f32(mla), as_f32(swa)


def _dft_tables(seq):
    n = 3 * seq // 2 if (3 * seq // 4) % 256 == 0 else 2 * seq
    kf = n // 2
    k = np.arange(kf, dtype=np.int64)
    s = np.arange(seq, dtype=np.int64)
    ang = ((k[:, None] * s[None, :]) % n).astype(np.float64) * (2.0 * math.pi / n)
    cos, sin = np.cos(ang), np.sin(ang)
    theta = ((k * (seq // 2)) % n).astype(np.float64) * (2.0 * math.pi / n)
    wgt = np.where(k == 0, 1.0 / n, 2.0 / n)
    rot = np.zeros((kf, 128), np.float64)
    rot[:, 0], rot[:, 1] = wgt * np.cos(theta), wgt * np.sin(theta)
    bf = lambda a: jnp.asarray(a, F32).astype(BF16)
    return bf(cos), bf(sin), bf(cos.T), bf(sin.T), jnp.asarray(rot, F32)


def _hy_features(seq):
    t = np.arange(seq, dtype=np.float32)
    t01 = t / np.float32(max(seq - 1, 1))
    w = (np.float32(2.0 * math.pi) * t / np.float32(seq)).astype(np.float64)
    bands = np.linspace(1e-4, HY_BANDS - 1, HY_BANDS, dtype=np.float32).astype(np.float64)
    feats = np.concatenate([t01[:, None].astype(np.float64), np.cos(w[:, None] * bands), -np.sin(w[:, None] * bands)],
                           axis=-1)
    feats = np.pad(feats, ((0, 0), (0, 128 - HY_EMB)))
    dist = (np.abs(t - (seq // 2)) / np.float32(seq / 2))[:, None]
    return jnp.asarray(feats, F32), jnp.asarray(dist, F32)


def _swa_placement():
    pk = np.zeros((128, GROUP_W), np.float32)
    pv = np.zeros((128, HEADS, GROUP_W), np.float32)
    group = HEADS // SWA_KV_HEADS
    for h in range(HEADS):
        for e in range(HEAD_DIM):
            src = (h // group) * HEAD_DIM + e
            pk[src, h * HEAD_DIM + e] = 1.0
            pv[src, h, h * HEAD_DIM + e] = 1.0
    return jnp.asarray(pk, BF16), jnp.asarray(pv.reshape(128, HEADS * GROUP_W), BF16)


def _gdn_tables():
    pair = 2 * GDN_CHUNK
    i = np.arange(pair)
    same = (i[:, None] // GDN_CHUNK) == (i[None, :] // GDN_CHUNK)
    lower = (same & (i[:, None] >= i[None, :])).astype(np.float32)
    upper = (same & (i[:, None] <= i[None, :])).astype(np.float32)
    j = np.arange(GROUP_W)
    ones_bd = ((j[:, None] // HEAD_DIM) == (j[None, :] // HEAD_DIM)).astype(np.float32)
    ones2 = np.concatenate([ones_bd] * 2, axis=0)
    tri3 = np.stack([np.concatenate([m] * 3, axis=1) for m in (lower, upper)])
    trit3 = np.stack([np.concatenate([m] * 3, axis=0) for m in (lower, upper)])
    return jnp.asarray(ones2, BF16), jnp.asarray(tri3, BF16), jnp.asarray(trit3, BF16)


def _pad_last(x, n):
    return jnp.pad(x, [(0, 0)] * (x.ndim - 1) + [(0, n - x.shape[-1])])


def _layout_w_in(w):
    dep, d, _ = w.shape
    mq = _pad_last(w[..., :384].reshape(dep, d, HEADS, MLA_QK), 128).reshape(dep, d, HEADS * 128)
    mla = _pad_last(jnp.concatenate([mq, w[..., 384:544]], axis=-1), W_MLA)
    gdn = _pad_last(w[..., 1056:2096], W_GDN)
    groups = tuple(g.astype(BF16) for g in (mla, w[..., 544:1056], gdn, w[..., 2096:2864]))
    w_gate_t = jnp.swapaxes(w[..., 2080:2096], 1, 2).astype(BF16)
    return groups, w_gate_t


def _layout_w_ukv(w):
    dep = w.shape[0]
    w = w.reshape(dep, MLA_KV_RANK, HEADS, 128)
    wk = _pad_last(w[..., :MLA_NOPE], 128).reshape(dep, MLA_KV_RANK, HEADS * 128)
    place = np.zeros((128, HEADS, 128), np.float32)
    for j in range(MLA_ROPE):
        place[j, :, MLA_NOPE + j] = 1.0
    place = jnp.broadcast_to(jnp.asarray(place.reshape(128, HEADS * 128)), (dep, 128, HEADS * 128))
    wk = jnp.concatenate([wk, place], axis=1)
    eye = jnp.asarray(np.eye(HEADS, dtype=np.float32))
    wv = (w[..., MLA_NOPE:][:, :, :, None, :] * eye[None, None, :, :, None]).reshape(dep, MLA_KV_RANK, HEADS * GROUP_W)
    ones_route = np.zeros((128, HEADS * GROUP_W), np.float32)
    for h in range(HEADS):
        ones_route[127, h * GROUP_W + _sum_lane(h)] = 1.0
    wv = jnp.concatenate([wv, jnp.broadcast_to(jnp.asarray(ones_route), (dep,) + ones_route.shape)], axis=1)
    return jnp.concatenate([wk, wv], axis=-1).astype(BF16)


def _layer_pass(x, n_batch, seq, l, mod, mod_spec, P, mla_ctx=None, swa_ctx=None, state=None, tables=None):
    is_ctx = tables is None
    p_mla, p_swa, p_gdn, p_hy, gate_t = _inproj(x, mod, mod_spec, l, P["g_pre_mix"], P["w_in"], P["w_gate_t"])
    mla_out = _mla(p_mla, n_batch, seq, l, P["mla_kv_norm"], P["w_ukv"], ctx=mla_ctx,
                   tables=None if is_ctx else tables[0], emit_ckv=is_ctx)
    o_a = mla_out[0]
    o_b = _swa(p_swa, n_batch, seq, l, P["swa_sink"], P["swa_pk"], P["swa_pv"], ctx_kv=swa_ctx,
               tables=None if is_ctx else tables[1])
    gdn_out = _gdn(p_gdn, gate_t, n_batch, seq, l, P["gdn_params"], P["gdn_tables"], state=state, emit_state=is_ctx)
    o_c = gdn_out[0]
    o_d = _hy_conv(p_hy, n_batch, seq, l, P["hy_conv"], P["hy_bias"], *(P["hy_ctx"] if is_ctx else P["hy_lat"]))
    y = _outmlp((o_a, o_b, o_c, o_d), x, mod, mod_spec, l, P["g_post_mix"], P["g_pre_mlp"], P["g_post_mlp"],
                P["w_out"], P["mlp_w1"], P["mlp_w2"])
    if not is_ctx:
        return y, None
    kpe0 = HEADS * 128 + MLA_KV_RANK
    new = (mla_out[1].reshape(n_batch, seq, MLA_KV_RANK),
           p_mla[:, kpe0:kpe0 + MLA_ROPE].reshape(n_batch, seq, MLA_ROPE),
           p_swa[:, 256:384].reshape(n_batch, seq, SWA_KV_HEADS, HEAD_DIM),
           p_swa[:, 384:512].reshape(n_batch, seq, SWA_KV_HEADS, HEAD_DIM),
           gdn_out[1])
    return y, new


def kernel(x_prompt, x_sample, cache_mla_ckv, cache_mla_kpe, cache_swa_k, cache_swa_v, state_gdn, c, c_ctx, w_ada, b_ada, g_pre_mix, g_post_mix, g_pre_mlp, g_post_mlp, w_in, w_out, mla_kv_norm, mla_w_ukv, swa_sink, gdn_conv, gdn_a_log, gdn_dt_bias, gdn_norm, hy_conv, hy_w1, hy_b1, hy_w2, hy_b2, hy_w3, hy_freq, hy_decay, hy_bias, mlp_w1, mlp_w2):
    n_ctx_b, seq_ctx, d = x_prompt.shape
    n_lat_b, seq_lat, _ = x_sample.shape
    past = cache_mla_ckv.shape[2]

    cond8 = jnp.concatenate([c_ctx[None, :], c, jnp.zeros((8 - 1 - n_lat_b, d), F32)], axis=0)
    mod = _modulation(cond8, w_ada, b_ada).reshape(DEPTH, 8, 6, 1, d)

    w_in_all, w_gate_t = _layout_w_in(w_in)
    gates = jnp.stack([gdn_a_log.reshape(DEPTH, 8), gdn_dt_bias.reshape(DEPTH, 8)], axis=1)
    gates_t = jnp.pad(jnp.swapaxes(gates, 1, 2), ((0, 0), (0, 8), (0, 0)))
    swa_pk, swa_pv = _swa_placement()
    P = dict(
        g_pre_mix=g_pre_mix[:, None], g_post_mix=g_post_mix[:, None],
        g_pre_mlp=g_pre_mlp[:, None], g_post_mlp=g_post_mlp[:, None],
        w_in=w_in_all, w_gate_t=w_gate_t,
        w_out=w_out.reshape(DEPTH, HEADS, GROUP_W, d).astype(BF16),
        mlp_w1=mlp_w1.astype(BF16), mlp_w2=mlp_w2.astype(BF16),
        mla_kv_norm=mla_kv_norm[:, None], w_ukv=_layout_w_ukv(mla_w_ukv),
        swa_sink=_pad_last(swa_sink, 128)[:, None], swa_pk=swa_pk, swa_pv=swa_pv,
        gdn_params=(gdn_conv, _pad_last(gates, 128), _pad_last(gates_t, 128), jnp.tile(gdn_norm, (1, HEADS))[:, None]),
        gdn_tables=_gdn_tables(),
        hy_conv=hy_conv, hy_bias=hy_bias,
    )
    hy_params = (jnp.pad(hy_w1, ((0, 0), (0, 128 - HY_EMB), (0, 0))), hy_b1[:, None], hy_w2, hy_b2[:, None], hy_w3,
                 hy_freq, hy_decay[:, None])
    for name, s in (("hy_ctx", seq_ctx), ("hy_lat", seq_lat)):
        dft = _dft_tables(s)
        P[name] = (dft,) + tuple(_hy_filter(s, *_hy_features(s), hy_params, dft))

    tables = _rope_tables(seq_lat)
    mla_ctx = jnp.concatenate([cache_mla_ckv, _pad_last(cache_mla_kpe, 127),
                               jnp.ones(cache_mla_kpe.shape[:-1] + (1,), F32)], axis=-1)
    swa_ctx = (cache_swa_k.reshape(n_lat_b, DEPTH, past, 128), cache_swa_v.reshape(n_lat_b, DEPTH, past, 128))
    lat_tiles = seq_lat // TOKEN_TILE

    xp = x_prompt.reshape(n_ctx_b * seq_ctx, d)
    xs = x_sample.reshape(n_lat_b * seq_lat, d)
    news = []
    for l in range(DEPTH):
        xp, new = _layer_pass(xp, n_ctx_b, seq_ctx, l, mod, _mod_spec(l, 0, None), P)
        news.append(new)
        xs, _ = _layer_pass(xs, n_lat_b, seq_lat, l, mod, _mod_spec(l, 1, lat_tiles), P, mla_ctx=mla_ctx,
                            swa_ctx=swa_ctx, state=state_gdn, tables=tables)

    stacked = tuple(jnp.stack([news[l][i] for l in range(DEPTH)], axis=1) for i in range(5))
    return (xp.reshape(n_ctx_b, seq_ctx, d), xs.reshape(n_lat_b, seq_lat, d)) + stacked
```

```python
import functools
import math

import jax
import jax.numpy as jnp
import numpy as np
from jax import lax
from jax.experimental import pallas as pl
from jax.experimental.pallas import tpu as pltpu

F32 = jnp.float32
BF16 = jnp.bfloat16

D_MODEL = 1024
DEPTH = 2
GRID_W = 64
HEADS = 4
HEAD_DIM = 64
GROUP_W = 256
MLA_NOPE = 64
MLA_ROPE = 32
MLA_QK = 96
MLA_KV_RANK = 128
SWA_KV_HEADS = 2
SWA_WINDOW = 128
GDN_CHUNK = 64
GDN_GROUP = 2
HY_BANDS = 8
HY_EMB = 17
HY_FF = 64
D_FF = 4096
ROPE_BASE = 10000.0
EPS = 1e-6

W_MLA = 768
W_SWA = 512
W_GDN = 1152
W_HY = 768
W_ALL = W_MLA + W_SWA + W_GDN + W_HY

TOKEN_TILE = 1024
MLA_ROW_TILE = 256
SWA_ROW_TILE = 256
VMEM_LIMIT = 56 * 1024 * 1024

_NT = (((1,), (1,)), ((), ()))


def _params(sem):
    return pltpu.CompilerParams(dimension_semantics=sem, vmem_limit_bytes=VMEM_LIMIT)


def _const_spec(shape):
    nd = len(shape)
    return pl.BlockSpec(shape, lambda *_: (0,) * nd, pipeline_mode=pl.Buffered(1))


def _layer_spec(arr, l):
    shape = arr.shape[1:]
    return pl.BlockSpec((None,) + shape, lambda *_: (l,) + (0,) * len(shape), pipeline_mode=pl.Buffered(1))


def _dot(a, b):
    return jnp.dot(a.astype(BF16), b.astype(BF16), preferred_element_type=F32)


def _dot_g(a, b, dims):
    return lax.dot_general(a.astype(BF16), b.astype(BF16), dims, preferred_element_type=F32)


_B_NN = (((2,), (1,)), ((0,), (0,)))
_B_NT = (((2,), (2,)), ((0,), (0,)))
_B_TN = (((1,), (1,)), ((0,), (0,)))


def _bmm(a, b, dims):
    return lax.dot_general(a.astype(BF16), b.astype(BF16), dims, preferred_element_type=F32)


def _split_bf16(x, parts):
    out = []
    for _ in range(parts):
        p = x.astype(BF16)
        out.append(p)
        x = x - p.astype(F32)
    return out


def _dot3(a, b):
    a_hi, a_lo = _split_bf16(a, 2)
    b_hi, b_lo = _split_bf16(b, 2)
    return _dot(jnp.concatenate([a_hi, a_hi, a_lo], axis=1), jnp.concatenate([b_hi, b_lo, b_hi], axis=0))


def _rms(x, g):
    return x * lax.rsqrt(jnp.mean(x * x, axis=-1, keepdims=True) + EPS) * g


def _silu(x):
    return x * jax.nn.sigmoid(x)


def _swap_pairs(x):
    n = x.shape[-1]
    nxt = pltpu.roll(x, n - 1, axis=1)
    prv = pltpu.roll(x, 1, axis=1)
    lane = lax.broadcasted_iota(jnp.int32, x.shape, 1)
    return jnp.where((lane & 1) == 0, nxt, prv)


def _rope(x, cos, sin_signed):
    return x * cos + _swap_pairs(x) * sin_signed


def _mod_body(c_ref, w_ref, b_ref, o_ref):
    s = _silu(c_ref[...]).astype(BF16)
    o_ref[...] = _dot(s, w_ref[...].astype(BF16)) + b_ref[...]


def _modulation(cond8, w_ada, b_ada):
    n = 6 * D_MODEL
    tn = 1536
    return pl.pallas_call(
        _mod_body,
        out_shape=jax.ShapeDtypeStruct((DEPTH, 8, n), F32),
        grid=(DEPTH, n // tn),
        in_specs=[
            pl.BlockSpec((8, D_MODEL), lambda l, j: (0, 0)),
            pl.BlockSpec((None, D_MODEL, tn), lambda l, j: (l, 0, j)),
            pl.BlockSpec((None, 1, tn), lambda l, j: (l, 0, j)),
        ],
        out_specs=pl.BlockSpec((None, 8, tn), lambda l, j: (l, 0, j)),
        compiler_params=_params(("arbitrary", "arbitrary")),
        name="modulation",
    )(cond8, w_ada, b_ada.reshape(DEPTH, 1, n))


def _mod_spec(l, row0, tiles_per_row):
    blk = (None, None, 6, 1, D_MODEL)
    if tiles_per_row is None:
        return pl.BlockSpec(blk, lambda i: (l, row0, 0, 0, 0))
    return pl.BlockSpec(blk, lambda i: (l, row0 + i // tiles_per_row, 0, 0, 0))


def _inproj_body(x_ref, mod_ref, g_ref, w_mla, w_swa, w_gdn, w_hy, wgt_ref, o_mla, o_swa, o_gdn, o_hy, o_gate_t):
    h = _rms(x_ref[...], g_ref[...]) * (1.0 + mod_ref[1]) + mod_ref[0]
    hb = h.astype(BF16)
    for o, w_ref in ((o_mla, w_mla), (o_swa, w_swa), (o_gdn, w_gdn), (o_hy, w_hy)):
        o[...] = _dot(hb, w_ref[...])
    o_gate_t[...] = _dot_g(wgt_ref[...], hb, _NT)


def _inproj(x, mod, mod_spec, l, g, w, w_gate_t):
    t = x.shape[0]
    tm = TOKEN_TILE
    widths = (W_MLA, W_SWA, W_GDN, W_HY)
    return pl.pallas_call(
        _inproj_body,
        out_shape=[jax.ShapeDtypeStruct((t, n), F32) for n in widths] + [jax.ShapeDtypeStruct((16, t), F32)],
        grid=(t // tm,),
        in_specs=[
            pl.BlockSpec((tm, D_MODEL), lambda i: (i, 0)),
            mod_spec,
            _layer_spec(g, l)] + [_layer_spec(wg, l) for wg in w] + [_layer_spec(w_gate_t, l),
        ],
        out_specs=[pl.BlockSpec((tm, n), lambda i: (i, 0)) for n in widths]
        + [pl.BlockSpec((16, tm), lambda i: (0, i))],
        compiler_params=_params(("arbitrary",)),
        name="inproj",
    )(x, mod, g, *w, w_gate_t)


def _outmlp_body(oa, ob, oc, od, x_ref, mod_ref, g_post_mix, g_pre_mlp, g_post_mlp,
                 wo_ref, w1_ref, w2_ref, out_ref):
    o = (_dot(oa[...], wo_ref[0]) + _dot(ob[...], wo_ref[1])
         + _dot(oc[...], wo_ref[2]) + _dot(od[...], wo_ref[3]))
    x = x_ref[...] + mod_ref[2] * _rms(o, g_post_mix[...])
    hb = (_rms(x, g_pre_mlp[...]) * (1.0 + mod_ref[4]) + mod_ref[3]).astype(BF16)
    acc = jnp.zeros(x.shape, F32)
    fc = 1024
    for c in range(D_FF // fc):
        a = _dot(hb, w1_ref[:, c * fc:(c + 1) * fc])
        a = jnp.square(jnp.maximum(a, 0.0)).astype(BF16)
        acc = acc + _dot(a, w2_ref[c * fc:(c + 1) * fc, :])
    out_ref[...] = x + mod_ref[5] * _rms(acc, g_post_mlp[...])


def _outmlp(o_parts, x, mod, mod_spec, l, g_post_mix, g_pre_mlp, g_post_mlp, wo, w1, w2):
    t = x.shape[0]
    tm = TOKEN_TILE
    part_spec = pl.BlockSpec((tm, GROUP_W), lambda i: (i, 0))
    return pl.pallas_call(
        _outmlp_body,
        out_shape=jax.ShapeDtypeStruct((t, D_MODEL), F32),
        grid=(t // tm,),
        in_specs=[part_spec] * 4 + [
            pl.BlockSpec((tm, D_MODEL), lambda i: (i, 0)),
            mod_spec,
        ] + [_layer_spec(a, l) for a in (g_post_mix, g_pre_mlp, g_post_mlp, wo, w1, w2)],
        out_specs=pl.BlockSpec((tm, D_MODEL), lambda i: (i, 0)),
        compiler_params=_params(("arbitrary",)),
        name="outproj_mlp",
    )(*o_parts, x, mod, g_post_mix, g_pre_mlp, g_post_mlp, wo, w1, w2)


def _sum_lane(h):
    return ((h + 1) % HEADS) * HEAD_DIM


def _rows(ref, j, n):
    return ref.at[pl.ds(j * n, n), :]


def _mla_body(*refs, seq, n_ctx, rope, emit_ckv, row_chunk, sub):
    it = iter(refs)
    q_ref, kv_ref = next(it), next(it)
    ctx_ref = next(it) if n_ctx else None
    g_ref, w_ref = next(it), next(it)
    tabs = tuple(next(it) for _ in range(4)) if rope else None
    o_ref = next(it)
    ckv_out = next(it) if emit_ckv else None
    k_s, v_s = next(it), next(it)
    tq = q_ref.shape[0] // sub
    for j in range(sub):
        _mla_one(_rows(q_ref, j, tq), _rows(kv_ref, j, seq), ctx_ref, g_ref, w_ref, tabs, _rows(o_ref, j, tq),
                 _rows(ckv_out, j, seq) if emit_ckv else None, k_s.at[j], v_s.at[j],
                 seq=seq, n_ctx=n_ctx, rope=rope, emit_ckv=emit_ckv, row_chunk=row_chunk)


def _mla_one(q_ref, kv_ref, ctx_ref, g_ref, w_ref, tabs, o_ref, ckv_out, k_s, v_s, *,
             seq, n_ctx, rope, emit_ckv, row_chunk):
    if rope:
        cq_ref, sq_ref, ck_ref, sk_ref = tabs

    def expand(kin, r0, n):
        kv = _dot(kin.astype(BF16), w_ref[...])
        k_s[r0:r0 + n, :] = kv[:, :HEADS * 128].astype(BF16)
        for h in range(HEADS):
            c0 = HEADS * 128 + h * GROUP_W
            v_s[h, r0:r0 + n, :] = kv[:, c0:c0 + GROUP_W].astype(BF16)

    @pl.when(pl.program_id(1) == 0)
    def _prep():
        for r0 in range(0, seq, row_chunk):
            blk = kv_ref[r0:r0 + row_chunk, :]
            cn = _rms(blk[:, :MLA_KV_RANK], g_ref[...])
            pe = blk[:, MLA_KV_RANK:]
            if rope:
                pe = _rope(pe, ck_ref[r0:r0 + row_chunk, :], sk_ref[r0:r0 + row_chunk, :])
            if emit_ckv:
                ckv_out[r0:r0 + row_chunk, :] = cn
            pe = jnp.where(lax.broadcasted_iota(jnp.int32, pe.shape, 1) == 127, 1.0, pe)
            expand(jnp.concatenate([cn, pe], axis=1), r0, row_chunk)
        if n_ctx:
            expand(ctx_ref[...], seq, n_ctx)

    scale = MLA_QK ** -0.5
    rt = min(q_ref.shape[0], MLA_ROW_TILE)
    lane_head = lax.broadcasted_iota(jnp.int32, (rt, GROUP_W), 1) // HEAD_DIM
    for r0 in range(0, q_ref.shape[0], rt):
        acc = jnp.zeros((rt, GROUP_W), F32)
        for h in range(HEADS):
            qh = q_ref[r0:r0 + rt, h * 128:(h + 1) * 128]
            if rope:
                qh = _rope(qh, cq_ref[r0:r0 + rt, :], sq_ref[r0:r0 + rt, :])
            s = _dot_g(qh * scale, k_s[:, h * 128:(h + 1) * 128], _NT)
            p = jnp.exp(s - jnp.max(s, axis=-1, keepdims=True))
            pv = _dot(p, v_s[h])
            inv = 1.0 / pv[:, _sum_lane(h):_sum_lane(h) + 1]
            acc = acc + jnp.where(lane_head == h, pv * inv, 0.0)
        o_ref[r0:r0 + rt, :] = acc.astype(BF16)


def _mla(proj, n_batch, seq, l, g, w, ctx=None, tables=None, emit_ckv=False):
    tq = min(seq, 1024)
    nq = seq // tq
    sub = 4 if nq == 1 else 1
    n_ctx = 0 if ctx is None else ctx.shape[2]
    rope = tables is not None
    lk = seq + n_ctx
    in_specs = [
        pl.BlockSpec((sub * tq, HEADS * 128), lambda b, i: (b * nq + i, 0)),
        pl.BlockSpec((sub * seq, 256), lambda b, i: (b, 2)),
    ]
    args = [proj, proj]
    if n_ctx:
        in_specs.append(pl.BlockSpec((None, None, n_ctx, 256), lambda b, i: (b, l, 0, 0)))
        args.append(ctx)
    in_specs += [_layer_spec(g, l), _layer_spec(w, l)]
    args += [g, w]
    if rope:
        in_specs += [pl.BlockSpec((tq, 128), lambda b, i: (i, 0))] * 2 + [_const_spec((seq, 128))] * 2
        args += list(tables)
    out_shape = [jax.ShapeDtypeStruct((n_batch * seq, GROUP_W), BF16)]
    out_specs = [pl.BlockSpec((sub * tq, GROUP_W), lambda b, i: (b * nq + i, 0))]
    if emit_ckv:
        out_shape.append(jax.ShapeDtypeStruct((n_batch * seq, MLA_KV_RANK), F32))
        out_specs.append(pl.BlockSpec((sub * seq, MLA_KV_RANK), lambda b, i: (b, 0)))
    body = functools.partial(_mla_body, seq=seq, n_ctx=n_ctx, rope=rope, emit_ckv=emit_ckv,
                             row_chunk=min(seq, 512), sub=sub)
    return pl.pallas_call(
        body, out_shape=out_shape, grid=(n_batch // sub, nq), in_specs=in_specs, out_specs=out_specs,
        scratch_shapes=[pltpu.VMEM((sub, lk, HEADS * 128), BF16), pltpu.VMEM((sub, HEADS, lk, GROUP_W), BF16)],
        compiler_params=_params(("arbitrary", "arbitrary")),
        name="mla_latent" if rope else "mla_context",
    )(*args)


def _swa_body(*refs, seq, n_ctx, local, row_chunk, sub):
    it = iter(refs)
    q_ref, kv_ref = next(it), next(it)
    ctx_refs = (next(it), next(it)) if n_ctx else None
    consts = (next(it), next(it), next(it))
    tabs = tuple(next(it) for _ in range(4)) if local else None
    o_ref = next(it)
    k_s, v_s = next(it), next(it)
    tq = q_ref.shape[0] // sub
    for j in range(sub):
        _swa_one(_rows(q_ref, j, tq), _rows(kv_ref, j, seq), ctx_refs, consts, tabs, _rows(o_ref, j, tq),
                 k_s.at[j], v_s.at[j], seq=seq, n_ctx=n_ctx, local=local, row_chunk=row_chunk)


def _swa_one(q_ref, kv_ref, ctx_refs, consts, tabs, o_ref, k_s, v_s, *, seq, n_ctx, local, row_chunk):
    if n_ctx:
        kc_ref, vc_ref = ctx_refs
    sink_ref, pk_ref, pv_ref = consts
    if local:
        cq_ref, sq_ref, ck_ref, sk_ref = tabs
    tq = q_ref.shape[0]
    blk_id = pl.program_id(1)

    def expand(k, v, r0, n):
        k_s[r0:r0 + n, :] = _dot(k.astype(BF16), pk_ref[...]).astype(BF16)
        vb = v.astype(BF16)
        lane = lax.broadcasted_iota(jnp.int32, (n, GROUP_W), 1)
        for h in range(HEADS):
            ve = _dot(vb, pv_ref[:, h * GROUP_W:(h + 1) * GROUP_W])
            v_s[h, r0:r0 + n, :] = jnp.where(lane == _sum_lane(h), 1.0, ve).astype(BF16)

    @pl.when(blk_id == 0)
    def _prep():
        for r0 in range(0, seq, row_chunk):
            k = kv_ref[r0:r0 + row_chunk, 0:128]
            v = kv_ref[r0:r0 + row_chunk, 128:256]
            if local:
                k = _rope(k, ck_ref[r0:r0 + row_chunk, :], sk_ref[r0:r0 + row_chunk, :])
            expand(k, v, r0, row_chunk)
        if n_ctx:
            expand(kc_ref[...], vc_ref[...], seq, n_ctx)

    scale = HEAD_DIM ** -0.5
    rt = min(tq, SWA_ROW_TILE)
    lane_head = lax.broadcasted_iota(jnp.int32, (rt, GROUP_W), 1) // HEAD_DIM
    for r0 in range(0, tq, rt):
        q = q_ref[r0:r0 + rt, :]
        if local:
            q = _rope(q, cq_ref[r0:r0 + rt, :], sq_ref[r0:r0 + rt, :])
            win = rt + 2 * SWA_WINDOW
            q0 = blk_id * tq + r0
            start = pl.multiple_of(jnp.clip(q0 - SWA_WINDOW, 0, seq - win), SWA_WINDOW)
            qpos = q0 + lax.broadcasted_iota(jnp.int32, (rt, win), 0)
            kpos = start + lax.broadcasted_iota(jnp.int32, (rt, win), 1)
            valid = jnp.abs(qpos - kpos) <= SWA_WINDOW
        q = q * scale
        acc = jnp.zeros((rt, GROUP_W), F32)
        for h in range(HEADS):
            qm = jnp.where(lane_head == h, q, 0.0).astype(BF16)
            sink = sink_ref[:, h:h + 1]
            if local:
                s_loc = jnp.where(valid, _dot_g(qm, k_s[pl.ds(start, win), :], _NT), -jnp.inf)
                s_ctx = _dot_g(qm, k_s[seq:seq + n_ctx, :], _NT)
                m = jnp.maximum(jnp.maximum(jnp.max(s_loc, axis=-1, keepdims=True),
                                            jnp.max(s_ctx, axis=-1, keepdims=True)), sink)
                o = (_dot(jnp.exp(s_loc - m), v_s[h, pl.ds(start, win), :])
                     + _dot(jnp.exp(s_ctx - m), v_s[h, seq:seq + n_ctx, :]))
            else:
                s = _dot_g(qm, k_s[...], _NT)
                m = jnp.maximum(jnp.max(s, axis=-1, keepdims=True), sink)
                o = _dot(jnp.exp(s - m), v_s[h])
            den = o[:, _sum_lane(h):_sum_lane(h) + 1] + jnp.exp(sink - m)
            acc = acc + jnp.where(lane_head == h, o * (1.0 / den), 0.0)
        o_ref[r0:r0 + rt, :] = acc.astype(BF16)


def _swa(proj, n_batch, seq, l, sink, pk, pv, ctx_kv=None, tables=None):
    local = tables is not None
    tq = 4 * SWA_ROW_TILE if local else seq
    nq = seq // tq
    sub = 4 if nq == 1 else 1
    n_ctx = 0 if ctx_kv is None else ctx_kv[0].shape[2]
    lk = seq + n_ctx
    in_specs = [
        pl.BlockSpec((sub * tq, GROUP_W), lambda b, i: (b * nq + i, 0)),
        pl.BlockSpec((sub * seq, 256), lambda b, i: (b, 1)),
    ]
    args = [proj, proj]
    if n_ctx:
        in_specs += [pl.BlockSpec((None, None, n_ctx, 128), lambda b, i: (b, l, 0, 0))] * 2
        args += list(ctx_kv)
    in_specs += [_layer_spec(sink, l), _const_spec(pk.shape), _const_spec(pv.shape)]
    args += [sink, pk, pv]
    if local:
        in_specs += [pl.BlockSpec((tq, GROUP_W), lambda b, i: (i, 0))] * 2 + [_const_spec((seq, 128))] * 2
        args += list(tables)
    body = functools.partial(_swa_body, seq=seq, n_ctx=n_ctx, local=local, row_chunk=min(seq, 512), sub=sub)
    return pl.pallas_call(
        body,
        out_shape=jax.ShapeDtypeStruct((n_batch * seq, GROUP_W), BF16),
        grid=(n_batch // sub, nq), in_specs=in_specs,
        out_specs=pl.BlockSpec((sub * tq, GROUP_W), lambda b, i: (b * nq + i, 0)),
        scratch_shapes=[pltpu.VMEM((sub, lk, GROUP_W), BF16), pltpu.VMEM((sub, HEADS, lk, GROUP_W), BF16)],
        compiler_params=_params(("arbitrary", "arbitrary")),
        name="swa_latent" if local else "swa_context",
    )(*args)


def _conv3(x, w_ref):
    n = x.shape[0]
    row = lax.broadcasted_iota(jnp.int32, x.shape, 0)
    prev = jnp.where(row == 0, 0.0, pltpu.roll(x, 1, axis=0))
    nxt = jnp.where(row == n - 1, 0.0, pltpu.roll(x, n - 1, axis=0))
    return prev * w_ref[0:1, :] + x * w_ref[1:2, :] + nxt * w_ref[2:3, :]


def _gdn_group(seq):
    n_pair = seq // (2 * GDN_CHUNK)
    return n_pair if n_pair <= GDN_GROUP else 1


def _gdn_body(*refs, seq, has_state, emit_state, sub):
    it = iter(refs)
    x_ref, gt_ref = next(it), next(it)
    s0_ref = next(it) if has_state else None
    consts = tuple(next(it) for _ in range(7))
    o_ref = next(it)
    st_ref = next(it) if emit_state else None
    scratch = tuple(it)
    for j in range(sub):
        _gdn_one(_rows(x_ref, j, seq), gt_ref.at[:, pl.ds(j * seq, seq)], s0_ref, consts, _rows(o_ref, j, seq),
                 st_ref.at[j] if emit_state else None, tuple(s.at[j] for s in scratch),
                 seq=seq, has_state=has_state, emit_state=emit_state)


def _gdn_one(x_ref, gt_ref, s0_ref, consts, o_ref, st_ref, scratch, *, seq, has_state, emit_state):
    cw_ref, gpar_ref, gpart_ref, gnorm_ref, ones_ref, tri_ref, trit_ref = consts
    q_s, k_s, v_s, of_s, ob_s, dec_s, dect_s, gate_s, s_s = scratch[:9]
    st_uv, st_w, st_qd, st_attn = scratch[9:]
    pair = 2 * GDN_CHUNK
    n_pair = seq // pair
    grp = _gdn_group(seq)

    def conv_act(g):
        cols = slice(g * GROUP_W, (g + 1) * GROUP_W)
        return _silu(_conv3(x_ref[:, cols], cw_ref.at[:, cols]))

    def head_sums(x):
        return _dot(jnp.concatenate(_split_bf16(x, 2), axis=1), ones_ref[...])

    def head_l2(x):
        return x * lax.rsqrt(head_sums(x * x) + EPS)

    def to_heads(dst, x):
        for h in range(HEADS):
            dst[h] = x[:, h * HEAD_DIM:(h + 1) * HEAD_DIM]

    to_heads(q_s, head_l2(conv_act(0)) * (HEAD_DIM ** -0.5))
    to_heads(k_s, head_l2(conv_act(1)))
    to_heads(v_s, conv_act(2))
    gab = x_ref[:, 4 * GROUP_W:4 * GROUP_W + 128]
    glog = -jnp.exp(gpar_ref[0:1, :]) * jax.nn.softplus(gab + gpar_ref[1:2, :])
    gate_s[...] = jax.nn.sigmoid(gab)
    lane = lax.broadcasted_iota(jnp.int32, (pair, 128), 1)
    glogt = -jnp.exp(gpart_ref[:, 0:1]) * jax.nn.softplus(gt_ref[...] + gpart_ref[:, 1:2])
    rowi = lax.broadcasted_iota(jnp.int32, (16, pair), 0)
    g3 = _split_bf16(glog, 3)
    gt3 = _split_bf16(glogt, 3)
    for p in range(n_pair):
        r = slice(p * pair, (p + 1) * pair)
        gcol = jnp.concatenate([g[r, :] for g in g3], axis=0)
        grow = jnp.concatenate([g[:, r] for g in gt3], axis=1)
        dec_s[r, :] = jnp.where(lane < HEADS, _dot(tri_ref[0], gcol), _dot(tri_ref[1], gcol))
        dect_s[:, r] = jnp.where(rowi < HEADS, _dot(grow, trit_ref[1]), _dot(grow, trit_ref[0]))

    if has_state:
        s_s[...] = s0_ref[...]
    else:
        s_s[...] = jnp.zeros(s_s.shape, F32)

    ri = lax.broadcasted_iota(jnp.int32, (pair, pair), 0)
    ci = lax.broadcasted_iota(jnp.int32, (pair, pair), 1)
    same = (ri // GDN_CHUNK) == (ci // GDN_CHUNK)
    nb = 2 * HEADS
    incl = (same & (ri >= ci), same & (ri <= ci))
    strict = (same & (ri > ci), same & (ri < ci))
    eye = (ri == ci).astype(F32)
    merge = [((ri // (2 * s)) == (ci // (2 * s))) & ((ri // s) != (ci // s))
             for s in (1, 2, 4, 8, 16, 32)]
    c = GDN_CHUNK

    def first_second(x):
        return (jnp.concatenate([x[:HEADS, :c], x[HEADS:, c:]], axis=0),
                jnp.concatenate([x[:HEADS, c:], x[HEADS:, :c]], axis=0))

    def row_order(first, second):
        return jnp.concatenate([jnp.concatenate([first[:HEADS], second[:HEADS]], axis=1),
                                jnp.concatenate([second[HEADS:], first[HEADS:]], axis=1)], axis=0)

    def rows_of(i):
        fwd, bwd = i * pair, (n_pair - 1 - i) * pair
        if not isinstance(i, int):
            fwd, bwd = pl.multiple_of(fwd, pair), pl.multiple_of(bwd, pair)
        return pl.ds(fwd, pair), pl.ds(bwd, pair)

    def solve_stage(steps, filler=iter(())):
        ab, t, rhs_v, rhs_w, dst = [], [], [], [], []
        for i, d in [(i, d) for i in steps for d in range(2)]:
            slot = i % (2 * grp)
            r = rows_of(i)[d]
            gd_, dd_, td_ = gate_s[r, :], dec_s[r, :], dect_s[:, r]
            for h in range(HEADS):
                p, col = d * HEADS + h, d * HEADS + h
                dst.append((slot, p))
                q, k, v = q_s[h, r, :], k_s[h, r, :], v_s[h, r, :]
                beta = gd_[:, 8 + col:9 + col]
                dcol = dd_[:, col:col + 1]
                gam = jnp.where(incl[d], jnp.exp(jnp.where(incl[d], dcol - td_[col:col + 1, :], 0.0)), 0.0)
                kb = k * beta
                kk_qk = _dot_g(jnp.concatenate([kb, q], axis=0), k, _NT)
                a = jnp.where(strict[d], kk_qk[:pair] * gam, 0.0)
                edec = jnp.exp(dcol)
                st_attn[slot, p] = (kk_qk[pair:] * gam).astype(BF16)
                st_qd[slot, p] = (q * edec).astype(BF16)
                ab.append(a.astype(BF16))
                t.append((eye - jnp.where(merge[0], a, 0.0)).astype(BF16))
                rhs_v.append(v * beta)
                rhs_w.append(kb * edec)
        n_prob = len(dst)
        for m in merge[1:]:
            for _ in steps:
                next(filler, None)
            x = [_dot(t[j], jnp.where(m, ab[j], 0.0)) for j in range(n_prob)]
            x = [_dot(x[j], t[j]) for j in range(n_prob)]
            t = [t[j] - x[j].astype(BF16) for j in range(n_prob)]
        for j, (slot, p) in enumerate(dst):
            st_uv[slot, p] = _dot(t[j], rhs_v[j])
            st_w[slot, p] = _dot(t[j], rhs_w[j]).astype(BF16)
        for _ in filler:
            pass

    def state_steps(i):
        rf, rb = rows_of(i)
        slot = i % (2 * grp)
        uv1, uv2 = first_second(st_uv[slot])
        w1, w2 = first_second(st_w[slot])
        qd1, qd2 = first_second(st_qd[slot])
        k1, k2 = first_second(jnp.stack([k_s[h, r, :] for r in (rf, rb) for h in range(HEADS)]))
        dec_f, dec_b = dec_s[rf, :], dec_s[rb, :]
        d1, d2 = first_second(jnp.stack([dec_f[:, j:j + 1] for j in range(HEADS)]
                                        + [dec_b[:, HEADS + j:HEADS + j + 1] for j in range(HEADS)]))
        dl1 = jnp.concatenate([d1[:HEADS, c - 1:c], d1[HEADS:, 0:1]], axis=0)
        dl2 = jnp.concatenate([d2[:HEADS, c - 1:c], d2[HEADS:, 0:1]], axis=0)
        s = s_s[...].reshape(nb, HEAD_DIM, HEAD_DIM)
        u1 = uv1 - _bmm(w1, s, _B_NN)
        o1 = _bmm(qd1, s, _B_NN)
        yield
        s = s * jnp.exp(dl1) + _bmm(k1 * jnp.exp(dl1 - d1), u1, _B_TN)
        yield
        u2 = uv2 - _bmm(w2, s, _B_NN)
        o2 = _bmm(qd2, s, _B_NN)
        yield
        s = s * jnp.exp(dl2) + _bmm(k2 * jnp.exp(dl2 - d2), u2, _B_TN)
        s_s[...] = s.reshape(2, HEADS, HEAD_DIM, HEAD_DIM)
        yield
        o = row_order(o1, o2) + _bmm(st_attn[slot], row_order(u1, u2), _B_NN)
        for j in range(HEADS // 2):
            of_s[j, rf, :] = jnp.concatenate([o[2 * j], o[2 * j + 1]], axis=1)
            ob_s[j, rb, :] = jnp.concatenate([o[HEADS + 2 * j], o[HEADS + 2 * j + 1]], axis=1)

    def group(g):
        return tuple(g * grp + j for j in range(grp))

    def state_chain(g):
        for i in group(g):
            yield from state_steps(i)

    def loop_body(g, carry):
        solve_stage(group(g + 1), state_chain(g))
        return carry

    n_group = n_pair // grp
    solve_stage(group(0))
    if n_group > 1:
        lax.fori_loop(0, n_group - 1, loop_body, 0)
    for _ in state_chain(n_group - 1):
        pass

    o = jnp.concatenate([of_s[j] + ob_s[j] for j in range(HEADS // 2)], axis=1)
    ms = head_sums(o * o) * (1.0 / HEAD_DIM)
    o = o * lax.rsqrt(ms + EPS) * gnorm_ref[...]
    o_ref[...] = (o * _silu(x_ref[:, 3 * GROUP_W:4 * GROUP_W])).astype(BF16)
    if emit_state:
        st_ref[...] = s_s[...]


def _gdn(proj, gate_t, n_batch, seq, l, layer_params, tables, state=None, emit_state=False):
    has_state = state is not None
    sub = 1 if has_state else 2
    in_specs = [pl.BlockSpec((sub * seq, W_GDN), lambda b: (b, 0), pipeline_mode=pl.Buffered(2 if sub > 1 else 1)),
                pl.BlockSpec((16, sub * seq), lambda b: (0, b))]
    args = [proj, gate_t]
    state_block = (sub, 2, HEADS, HEAD_DIM, HEAD_DIM)
    if has_state:
        in_specs.append(pl.BlockSpec((None, None) + state_block[1:], lambda b: (b, l, 0, 0, 0, 0)))
        args.append(state)
    for p in layer_params:
        in_specs.append(_layer_spec(p, l))
        args.append(p)
    for c in tables:
        in_specs.append(_const_spec(c.shape))
        args.append(c)
    out_shape = [jax.ShapeDtypeStruct((n_batch * seq, GROUP_W), BF16)]
    out_specs = [pl.BlockSpec((sub * seq, GROUP_W), lambda b: (b, 0))]
    if emit_state:
        out_shape.append(jax.ShapeDtypeStruct((n_batch, 2, HEADS, HEAD_DIM, HEAD_DIM), F32))
        out_specs.append(pl.BlockSpec(state_block, lambda b: (b, 0, 0, 0, 0)))
    vmem = lambda shape, dtype: pltpu.VMEM((sub,) + shape, dtype)
    scratch = [vmem((HEADS, seq, HEAD_DIM), F32)] * 3 + [vmem((HEADS // 2, seq, 128), F32)] * 2 + [
        vmem((seq, 128), F32), vmem((16, seq), F32), vmem((seq, 128), F32),
        vmem((2, HEADS, HEAD_DIM, HEAD_DIM), F32)]
    nb, pair, ns = 2 * HEADS, 2 * GDN_CHUNK, 2 * _gdn_group(seq)
    scratch += [vmem((ns, nb, pair, HEAD_DIM), F32), vmem((ns, nb, pair, HEAD_DIM), BF16),
                vmem((ns, nb, pair, HEAD_DIM), BF16), vmem((ns, nb, pair, pair), BF16)]
    body = functools.partial(_gdn_body, seq=seq, has_state=has_state, emit_state=emit_state, sub=sub)
    return pl.pallas_call(
        body, out_shape=out_shape, grid=(n_batch // sub,), in_specs=in_specs, out_specs=out_specs,
        scratch_shapes=scratch, compiler_params=_params(("arbitrary",)),
        name="gdn_latent" if has_state else "gdn_context",
    )(*args)


def _hy_filter_body(feat_ref, dist_ref, w1_ref, b1_ref, w2_ref, b2_ref, w3_ref, freq_ref, decay_ref,
                    cf_ref, sf_ref, rot_ref, hr_ref, hi_ref, nyq_ref, *, seq):
    h = jnp.sin(freq_ref[0:1, :] * (_dot3(feat_ref[...], w1_ref[...]) + b1_ref[...]))
    h = jnp.sin(freq_ref[1:2, :] * (_dot3(h, w2_ref[...]) + b2_ref[...]))
    h = _dot3(h, w3_ref[...])
    filt = h * jnp.exp(-dist_ref[...] * jnp.abs(decay_ref[...]))
    kf = cf_ref.shape[0]
    t = lax.broadcasted_iota(jnp.int32, (seq, 1), 0)
    alt = jnp.where((t & 1) == 0, 1.0, -1.0)
    nyq_ref[...] = jnp.sum(filt * alt, axis=0, keepdims=True) * (1.0 / (2 * kf))
    fb = filt.astype(BF16)
    tk = min(kf, 512)
    for k0 in range(0, kf, tk):
        hr = _dot(cf_ref[k0:k0 + tk, :], fb)
        hs = _dot(sf_ref[k0:k0 + tk, :], fb)
        wc, ws = rot_ref[k0:k0 + tk, 0:1], rot_ref[k0:k0 + tk, 1:2]
        hr_ref[k0:k0 + tk, :] = hr * wc + hs * ws
        hi_ref[k0:k0 + tk, :] = hr * ws - hs * wc


def _hy_filter(seq, feats, dist, layer_params, dft):
    cf, sf, _, _, rot = dft
    kf = cf.shape[0]
    per_layer = lambda a: pl.BlockSpec((None,) + a.shape[1:], lambda l: (l,) + (0,) * (a.ndim - 1))
    out_block = lambda rows: pl.BlockSpec((None, rows, 2 * GROUP_W), lambda l: (l, 0, 0))
    return pl.pallas_call(
        functools.partial(_hy_filter_body, seq=seq),
        out_shape=[jax.ShapeDtypeStruct((DEPTH, kf, 2 * GROUP_W), F32),
                   jax.ShapeDtypeStruct((DEPTH, kf, 2 * GROUP_W), F32),
                   jax.ShapeDtypeStruct((DEPTH, 1, 2 * GROUP_W), F32)],
        grid=(DEPTH,),
        in_specs=[_const_spec(feats.shape), _const_spec(dist.shape)] + [per_layer(a) for a in layer_params]
        + [_const_spec(cf.shape), _const_spec(sf.shape), _const_spec(rot.shape)],
        out_specs=[out_block(kf), out_block(kf), out_block(1)],
        compiler_params=_params(("arbitrary",)),
        name="hyena_filter",
    )(feats, dist, *layer_params, cf, sf, rot)


def _hy_conv_body(x_ref, cw_ref, bias_ref, cf_ref, sf_ref, cft_ref, sft_ref, hr_ref, hi_ref, nyq_ref, o_ref,
                  yr_s, yi_s, *, seq, sub):
    for j in range(sub):
        _hy_conv_one(_rows(x_ref, j, seq), cw_ref, bias_ref, cf_ref, sf_ref, cft_ref, sft_ref, hr_ref, hi_ref,
                     nyq_ref, _rows(o_ref, j, seq), yr_s.at[j], yi_s.at[j], seq=seq)


def _hy_conv_one(x_ref, cw_ref, bias_ref, cf_ref, sf_ref, cft_ref, sft_ref, hr_ref, hi_ref, nyq_ref, o_ref,
                 yr_s, yi_s, *, seq):
    t = lax.broadcasted_iota(jnp.int32, (seq, 1), 0)
    alt = jnp.where((t & 1) == 0, 1.0, -1.0)
    kf = cf_ref.shape[0]
    tk = min(kf, 512)

    def long_conv(a, o):
        cols = slice(o * GROUP_W, (o + 1) * GROUP_W)
        ab = a.astype(BF16)
        for k0 in range(0, kf, tk):
            ur = _dot(cf_ref[k0:k0 + tk, :], ab)
            us = _dot(sf_ref[k0:k0 + tk, :], ab)
            hr = hr_ref[k0:k0 + tk, cols]
            hi = hi_ref[k0:k0 + tk, cols]
            yr_s[k0:k0 + tk, :] = (ur * hr + us * hi).astype(BF16)
            yi_s[k0:k0 + tk, :] = (ur * hi - us * hr).astype(BF16)
        u_nyq = jnp.sum(a * alt, axis=0, keepdims=True)
        y = _dot(cft_ref[...], yr_s[...]) - _dot(sft_ref[...], yi_s[...])
        return y + alt * (u_nyq * nyq_ref[:, cols])

    v = _conv3(x_ref[:, 0:GROUP_W], cw_ref.at[:, 0:GROUP_W])
    x1 = _conv3(x_ref[:, GROUP_W:2 * GROUP_W], cw_ref.at[:, GROUP_W:2 * GROUP_W])
    z = x1 * (long_conv(v, 0) + v * bias_ref[0:1, :])
    x2 = _conv3(x_ref[:, 2 * GROUP_W:3 * GROUP_W], cw_ref.at[:, 2 * GROUP_W:3 * GROUP_W])
    o_ref[...] = (x2 * (long_conv(z, 1) + z * bias_ref[1:2, :])).astype(BF16)


def _hy_conv(proj, n_batch, seq, l, cw, bias, dft, hr, hi, nyq):
    cf, sf, cft, sft, _ = dft
    kf = cf.shape[0]
    sub = 4 if seq <= 512 else 1
    return pl.pallas_call(
        functools.partial(_hy_conv_body, seq=seq, sub=sub),
        out_shape=jax.ShapeDtypeStruct((n_batch * seq, GROUP_W), BF16),
        grid=(n_batch // sub,),
        in_specs=[pl.BlockSpec((sub * seq, W_HY), lambda b: (b, 0), pipeline_mode=pl.Buffered(2 if sub > 1 else 1)),
                  _layer_spec(cw, l), _layer_spec(bias, l)] + [_const_spec(a.shape) for a in (cf, sf, cft, sft)]
        + [_layer_spec(hr, l), _layer_spec(hi, l), _layer_spec(nyq, l)],
        out_specs=pl.BlockSpec((sub * seq, GROUP_W), lambda b: (b, 0)),
        scratch_shapes=[pltpu.VMEM((sub, kf, GROUP_W), BF16)] * 2,
        compiler_params=_params(("arbitrary",)),
        name="hyena_conv",
    )(proj, cw, bias, cf, sf, cft, sft, hr, hi, nyq)


def _rope_tables(seq):
    rows = seq // GRID_W
    row = np.repeat(np.arange(rows), GRID_W).astype(np.float64)
    col = np.tile(np.arange(GRID_W), rows).astype(np.float64)

    def pair_tables(dim):
        n_freq = dim // 4
        inv = (ROPE_BASE ** (-np.arange(n_freq, dtype=np.float32) / n_freq)).astype(np.float64)
        ang = np.concatenate([row[:, None] * inv, col[:, None] * inv], axis=-1).astype(np.float32)
        cos = np.repeat(np.cos(ang), 2, axis=-1)
        sin = np.repeat(np.sin(ang), 2, axis=-1) * np.tile(np.array([-1.0, 1.0], np.float32), dim // 2)
        return cos, sin

    ca, sa = pair_tables(MLA_ROPE)
    one = lambda n: np.ones((seq, n), np.float32)
    zero = lambda n: np.zeros((seq, n), np.float32)
    mla = (np.concatenate([one(MLA_NOPE), ca, one(32)], 1), np.concatenate([zero(MLA_NOPE), sa, zero(32)], 1),
           np.concatenate([ca, one(96)], 1), np.concatenate([sa, zero(96)], 1))
    cb, sb = pair_tables(HEAD_DIM)
    swa = (np.tile(cb, (1, HEADS)), np.tile(sb, (1, HEADS)),
           np.tile(cb, (1, SWA_KV_HEADS)), np.tile(sb, (1, SWA_KV_HEADS)))
    as_f32 = lambda ts: tuple(jnp.asarray(t, F32) for t in ts)
    return as_f32(mla), as_f32(swa)


def _dft_tables(seq):
    n = 3 * seq // 2 if (3 * seq // 4) % 256 == 0 else 2 * seq
    kf = n // 2
    k = np.arange(kf, dtype=np.int64)
    s = np.arange(seq, dtype=np.int64)
    ang = ((k[:, None] * s[None, :]) % n).astype(np.float64) * (2.0 * math.pi / n)
    cos, sin = np.cos(ang), np.sin(ang)
    theta = ((k * (seq // 2)) % n).astype(np.float64) * (2.0 * math.pi / n)
    wgt = np.where(k == 0, 1.0 / n, 2.0 / n)
    rot = np.zeros((kf, 128), np.float64)
    rot[:, 0], rot[:, 1] = wgt * np.cos(theta), wgt * np.sin(theta)
    bf = lambda a: jnp.asarray(a, F32).astype(BF16)
    return bf(cos), bf(sin), bf(cos.T), bf(sin.T), jnp.asarray(rot, F32)


def _hy_features(seq):
    t = np.arange(seq, dtype=np.float32)
    t01 = t / np.float32(max(seq - 1, 1))
    w = (np.float32(2.0 * math.pi) * t / np.float32(seq)).astype(np.float64)
    bands = np.linspace(1e-4, HY_BANDS - 1, HY_BANDS, dtype=np.float32).astype(np.float64)
    feats = np.concatenate([t01[:, None].astype(np.float64), np.cos(w[:, None] * bands), -np.sin(w[:, None] * bands)],
                           axis=-1)
    feats = np.pad(feats, ((0, 0), (0, 128 - HY_EMB)))
    dist = (np.abs(t - (seq // 2)) / np.float32(seq / 2))[:, None]
    return jnp.asarray(feats, F32), jnp.asarray(dist, F32)


def _swa_placement():
    pk = np.zeros((128, GROUP_W), np.float32)
    pv = np.zeros((128, HEADS, GROUP_W), np.float32)
    group = HEADS // SWA_KV_HEADS
    for h in range(HEADS):
        for e in range(HEAD_DIM):
            src = (h // group) * HEAD_DIM + e
            pk[src, h * HEAD_DIM + e] = 1.0
            pv[src, h, h * HEAD_DIM + e] = 1.0
    return jnp.asarray(pk, BF16), jnp.asarray(pv.reshape(128, HEADS * GROUP_W), BF16)


def _gdn_tables():
    pair = 2 * GDN_CHUNK
    i = np.arange(pair)
    same = (i[:, None] // GDN_CHUNK) == (i[None, :] // GDN_CHUNK)
    lower = (same & (i[:, None] >= i[None, :])).astype(np.float32)
    upper = (same & (i[:, None] <= i[None, :])).astype(np.float32)
    j = np.arange(GROUP_W)
    ones_bd = ((j[:, None] // HEAD_DIM) == (j[None, :] // HEAD_DIM)).astype(np.float32)
    ones2 = np.concatenate([ones_bd] * 2, axis=0)
    tri3 = np.stack([np.concatenate([m] * 3, axis=1) for m in (lower, upper)])
    trit3 = np.stack([np.concatenate([m] * 3, axis=0) for m in (lower, upper)])
    return jnp.asarray(ones2, BF16), jnp.asarray(tri3, BF16), jnp.asarray(trit3, BF16)


def _pad_last(x, n):
    return jnp.pad(x, [(0, 0)] * (x.ndim - 1) + [(0, n - x.shape[-1])])


def _layout_w_in(w):
    dep, d, _ = w.shape
    mq = _pad_last(w[..., :384].reshape(dep, d, HEADS, MLA_QK), 128).reshape(dep, d, HEADS * 128)
    mla = _pad_last(jnp.concatenate([mq, w[..., 384:544]], axis=-1), W_MLA)
    gdn = _pad_last(w[..., 1056:2096], W_GDN)
    groups = tuple(g.astype(BF16) for g in (mla, w[..., 544:1056], gdn, w[..., 2096:2864]))
    w_gate_t = jnp.swapaxes(w[..., 2080:2096], 1, 2).astype(BF16)
    return groups, w_gate_t


def _layout_w_ukv(w):
    dep = w.shape[0]
    w = w.reshape(dep, MLA_KV_RANK, HEADS, 128)
    wk = _pad_last(w[..., :MLA_NOPE], 128).reshape(dep, MLA_KV_RANK, HEADS * 128)
    place = np.zeros((128, HEADS, 128), np.float32)
    for j in range(MLA_ROPE):
        place[j, :, MLA_NOPE + j] = 1.0
    place = jnp.broadcast_to(jnp.asarray(place.reshape(128, HEADS * 128)), (dep, 128, HEADS * 128))
    wk = jnp.concatenate([wk, place], axis=1)
    eye = jnp.asarray(np.eye(HEADS, dtype=np.float32))
    wv = (w[..., MLA_NOPE:][:, :, :, None, :] * eye[None, None, :, :, None]).reshape(dep, MLA_KV_RANK, HEADS * GROUP_W)
    ones_route = np.zeros((128, HEADS * GROUP_W), np.float32)
    for h in range(HEADS):
        ones_route[127, h * GROUP_W + _sum_lane(h)] = 1.0
    wv = jnp.concatenate([wv, jnp.broadcast_to(jnp.asarray(ones_route), (dep,) + ones_route.shape)], axis=1)
    return jnp.concatenate([wk, wv], axis=-1).astype(BF16)


def _layer_pass(x, n_batch, seq, l, mod, mod_spec, P, mla_ctx=None, swa_ctx=None, state=None, tables=None):
    is_ctx = tables is None
    p_mla, p_swa, p_gdn, p_hy, gate_t = _inproj(x, mod, mod_spec, l, P["g_pre_mix"], P["w_in"], P["w_gate_t"])
    mla_out = _mla(p_mla, n_batch, seq, l, P["mla_kv_norm"], P["w_ukv"], ctx=mla_ctx,
                   tables=None if is_ctx else tables[0], emit_ckv=is_ctx)
    o_a = mla_out[0]
    o_b = _swa(p_swa, n_batch, seq, l, P["swa_sink"], P["swa_pk"], P["swa_pv"], ctx_kv=swa_ctx,
               tables=None if is_ctx else tables[1])
    gdn_out = _gdn(p_gdn, gate_t, n_batch, seq, l, P["gdn_params"], P["gdn_tables"], state=state, emit_state=is_ctx)
    o_c = gdn_out[0]
    o_d = _hy_conv(p_hy, n_batch, seq, l, P["hy_conv"], P["hy_bias"], *(P["hy_ctx"] if is_ctx else P["hy_lat"]))
    y = _outmlp((o_a, o_b, o_c, o_d), x, mod, mod_spec, l, P["g_post_mix"], P["g_pre_mlp"], P["g_post_mlp"],
                P["w_out"], P["mlp_w1"], P["mlp_w2"])
    if not is_ctx:
        return y, None
    kpe0 = HEADS * 128 + MLA_KV_RANK
    new = (mla_out[1].reshape(n_batch, seq, MLA_KV_RANK),
           p_mla[:, kpe0:kpe0 + MLA_ROPE].reshape(n_batch, seq, MLA_ROPE),
           p_swa[:, 256:384].reshape(n_batch, seq, SWA_KV_HEADS, HEAD_DIM),
           p_swa[:, 384:512].reshape(n_batch, seq, SWA_KV_HEADS, HEAD_DIM),
           gdn_out[1])
    return y, new


def kernel(x_prompt, x_sample, cache_mla_ckv, cache_mla_kpe, cache_swa_k, cache_swa_v, state_gdn, c, c_ctx, w_ada, b_ada, g_pre_mix, g_post_mix, g_pre_mlp, g_post_mlp, w_in, w_out, mla_kv_norm, mla_w_ukv, swa_sink, gdn_conv, gdn_a_log, gdn_dt_bias, gdn_norm, hy_conv, hy_w1, hy_b1, hy_w2, hy_b2, hy_w3, hy_freq, hy_decay, hy_bias, mlp_w1, mlp_w2):
    n_ctx_b, seq_ctx, d = x_prompt.shape
    n_lat_b, seq_lat, _ = x_sample.shape
    past = cache_mla_ckv.shape[2]

    cond8 = jnp.concatenate([c_ctx[None, :], c, jnp.zeros((8 - 1 - n_lat_b, d), F32)], axis=0)
    mod = _modulation(cond8, w_ada, b_ada).reshape(DEPTH, 8, 6, 1, d)

    w_in_all, w_gate_t = _layout_w_in(w_in)
    gates = jnp.stack([gdn_a_log.reshape(DEPTH, 8), gdn_dt_bias.reshape(DEPTH, 8)], axis=1)
    gates_t = jnp.pad(jnp.swapaxes(gates, 1, 2), ((0, 0), (0, 8), (0, 0)))
    swa_pk, swa_pv = _swa_placement()
    P = dict(
        g_pre_mix=g_pre_mix[:, None], g_post_mix=g_post_mix[:, None],
        g_pre_mlp=g_pre_mlp[:, None], g_post_mlp=g_post_mlp[:, None],
        w_in=w_in_all, w_gate_t=w_gate_t,
        w_out=w_out.reshape(DEPTH, HEADS, GROUP_W, d).astype(BF16),
        mlp_w1=mlp_w1.astype(BF16), mlp_w2=mlp_w2.astype(BF16),
        mla_kv_norm=mla_kv_norm[:, None], w_ukv=_layout_w_ukv(mla_w_ukv),
        swa_sink=_pad_last(swa_sink, 128)[:, None], swa_pk=swa_pk, swa_pv=swa_pv,
        gdn_params=(gdn_conv, _pad_last(gates, 128), _pad_last(gates_t, 128), jnp.tile(gdn_norm, (1, HEADS))[:, None]),
        gdn_tables=_gdn_tables(),
        hy_conv=hy_conv, hy_bias=hy_bias,
    )
    hy_params = (jnp.pad(hy_w1, ((0, 0), (0, 128 - HY_EMB), (0, 0))), hy_b1[:, None], hy_w2, hy_b2[:, None], hy_w3,
                 hy_freq, hy_decay[:, None])
    for name, s in (("hy_ctx", seq_ctx), ("hy_lat", seq_lat)):
        dft = _dft_tables(s)
        P[name] = (dft,) + tuple(_hy_filter(s, *_hy_features(s), hy_params, dft))

    tables = _rope_tables(seq_lat)
    mla_ctx = jnp.concatenate([cache_mla_ckv, _pad_last(cache_mla_kpe, 127),
                               jnp.ones(cache_mla_kpe.shape[:-1] + (1,), F32)], axis=-1)
    swa_ctx = (cache_swa_k.reshape(n_lat_b, DEPTH, past, 128), cache_swa_v.reshape(n_lat_b, DEPTH, past, 128))
    lat_tiles = seq_lat // TOKEN_TILE

    xp = x_prompt.reshape(n_ctx_b * seq_ctx, d)
    xs = x_sample.reshape(n_lat_b * seq_lat, d)
    news = []
    for l in range(DEPTH):
        xp, new = _layer_pass(xp, n_ctx_b, seq_ctx, l, mod, _mod_spec(l, 0, None), P)
        news.append(new)
        xs, _ = _layer_pass(xs, n_lat_b, seq_lat, l, mod, _mod_spec(l, 1, lat_tiles), P, mla_ctx=mla_ctx,
                            swa_ctx=swa_ctx, state=state_gdn, tables=tables)

    stacked = tuple(jnp.stack([news[l][i] for l in range(DEPTH)], axis=1) for i in range(5))
    return (xp.reshape(n_ctx_b, seq_ctx, d), xs.reshape(n_lat_b, seq_lat, d)) + stacked
```
